```python
import math
import jax, jax.numpy as jnp
from jax import lax
import numpy as np

D_MODEL = 1024
BATCH = 4
SEQ = 4096
DEPTH = 1
DEC_BATCH = 32
DEC_SEQ = 1
PAST_LEN = 8192
PAGE_SIZE = 128

A_WINDOWS = (128, 512, 2048)
A_DILATIONS = (1, 4, 16)
N_GROUPS_A = 3
HEADS_PER_GROUP = 8
HEAD_DIM = 64
N_HEADS_A = N_GROUPS_A * HEADS_PER_GROUP
A_QKV = N_HEADS_A * HEAD_DIM
A_WIDTH = HEADS_PER_GROUP * HEAD_DIM
BAND = 128
N_BUCKETS = 32
MAX_DISTANCE = 2048
CHUNK = 128
B_GROUPS = 4
B_WIDTH = D_MODEL
N_KEYS = 128
N_EXPERTS = N_KEYS * N_KEYS
PEER_HEADS = 8
PEER_QDIM = 256
PEER_TOPK = 16
PEER_BLOCK = 128
PLE_DIM = 256
IN_SIZES = (A_QKV, A_QKV, A_QKV, B_WIDTH, B_WIDTH, D_MODEL, D_MODEL)
IN_TOTAL = sum(IN_SIZES)
RMS_EPS = 1e-6
NEG_INF = -1e30

kernel_name = "hybrid_dilated_attn_chunk_gmlp_peer_decode_step"


def _rmsnorm(x, g):
    xf = x.astype(jnp.float32)
    y = xf * lax.rsqrt(jnp.mean(xf * xf, axis=-1, keepdims=True) + RMS_EPS)
    return (y * g.astype(jnp.float32)).astype(x.dtype)


def _t5_bucket(dist):
    max_exact = N_BUCKETS // 2
    df = jnp.maximum(dist, max_exact).astype(jnp.float32)
    large = max_exact + (jnp.log(df / max_exact) / math.log(MAX_DISTANCE / max_exact)
                         * (N_BUCKETS - max_exact)).astype(jnp.int32)
    return jnp.where(dist < max_exact, dist, jnp.minimum(large, N_BUCKETS - 1))


def _softmax_stats(s, v, spec):
    m = jnp.max(s, axis=-1, keepdims=True)
    p = jnp.exp(s - m)
    den = jnp.sum(p, axis=-1, keepdims=True)
    o = jnp.einsum(spec, p / den, v)
    return o, (m + jnp.log(den))[..., 0]


def _branch_inputs(x, g1, w_in, q_gain, k_gain, bv_gain):
    lead = x.shape[:2]
    h = _rmsnorm(x, g1)
    offs = np.cumsum(IN_SIZES)[:-1].tolist()
    q, k, v, ub, vb, ga, gb = jnp.split(h @ w_in, offs, axis=-1)
    hs = (*lead, N_GROUPS_A, HEADS_PER_GROUP, HEAD_DIM)
    q = _rmsnorm(q.reshape(hs), q_gain[:, None, :])
    k = _rmsnorm(k.reshape(hs), k_gain[:, None, :])
    v = v.reshape(hs)
    ub = jax.nn.gelu(ub)
    vb = _rmsnorm(jax.nn.gelu(vb), bv_gain)
    return q, k, v, ub, vb, ga, gb


def _dilated_group_prompt(q, k, v, dil, win, bias_g):
    Bn, S, H, E = q.shape
    n = S // dil
    nb = -(-n // BAND)
    pad = nb * BAND - n

    def to_blocks(t):
        t = t.astype(jnp.float32).reshape(Bn, n, dil, H, E)
        t = jnp.pad(t, ((0, 0), (0, pad), (0, 0), (0, 0), (0, 0)))
        return t.reshape(Bn, nb, BAND, dil, H, E)

    def with_prev(t):
        prev = jnp.pad(t, ((0, 0), (1, 0), (0, 0), (0, 0), (0, 0), (0, 0)))[:, :-1]
        return jnp.concatenate([prev, t], axis=2)

    qb = to_blocks(q)
    kc = with_prev(to_blocks(k))
    vc = with_prev(to_blocks(v))
    qi = jnp.arange(BAND)[:, None]
    kj = jnp.arange(2 * BAND)[None, :]
    dsub = qi + BAND - kj
    key_sub = (jnp.arange(nb)[:, None, None] - 1) * BAND + kj[None]
    valid = (dsub >= 0) & (dsub <= win // dil) & (key_sub >= 0)
    bias = bias_g[_t5_bucket(jnp.maximum(dsub, 0) * dil)].astype(jnp.float32)
    bias = jnp.moveaxis(bias, -1, 0)
    add = jnp.where(valid[:, None], bias[None], NEG_INF)
    s = jnp.einsum('bnqrhe,bnkrhe->bnrhqk', qb, kc) * (HEAD_DIM ** -0.5) + add[None, :, None]
    o, lse = _softmax_stats(s, vc, 'bnrhqk,bnkrhe->bnqrhe')
    o = o.reshape(Bn, nb * BAND, dil, H, E)[:, :n].reshape(Bn, S, H, E)
    lse = jnp.transpose(lse, (0, 1, 4, 2, 3)).reshape(Bn, nb * BAND, dil, H)[:, :n].reshape(Bn, S, H)
    return o, lse


def _dilated_group_sample(q, k, v, cache_kv, dil, win, bias_g):
    L = cache_kv.shape[1]
    DS = q.shape[1]
    ext_k = jnp.concatenate([cache_kv[:, :, 0], k], axis=1).astype(jnp.float32)
    ext_v = jnp.concatenate([cache_kv[:, :, 1], v], axis=1).astype(jnp.float32)
    dist = jnp.arange(win // dil + 1) * dil
    idx = L + jnp.arange(DS)[:, None] - dist[None]
    valid = idx >= 0
    idx = jnp.maximum(idx, 0)
    kg = ext_k[:, idx]
    vg = ext_v[:, idx]
    bias = bias_g[_t5_bucket(dist)].astype(jnp.float32)
    add = jnp.where(valid[:, :, None], bias[None], NEG_INF)
    s = jnp.einsum('bqhe,bqkhe->bqhk', q.astype(jnp.float32), kg) * (HEAD_DIM ** -0.5) \
        + jnp.transpose(add, (0, 2, 1))[None]
    return _softmax_stats(s, vg, 'bqhk,bqkhe->bqhe')


def _combine_groups(outs, lses, dtype):
    w = jax.nn.softmax(jnp.stack(lses, 0), axis=0)
    o = jnp.sum(w[..., None] * jnp.stack(outs, 0), axis=0)
    return o.reshape(*o.shape[:2], A_WIDTH).astype(dtype)


def _spatial_gate_prompt(ub, vb, w_s, b_s):
    Bn, S, _ = ub.shape
    vr = vb.reshape(Bn, S // CHUNK, CHUNK, B_GROUPS, B_WIDTH // B_GROUPS)
    mixed = jnp.einsum('gts,bcsgf->bctgf', w_s, vr) + b_s.T[None, None, :, :, None]
    return ub * mixed.reshape(Bn, S, B_WIDTH)


def _spatial_gate_sample(ub, vb, w_s, b_s):
    Bn, DS, _ = ub.shape
    vr = vb.reshape(Bn, DS, B_GROUPS, B_WIDTH // B_GROUPS)
    mixed = jnp.einsum('gts,bsgf->btgf', w_s[:, :DS, :DS], vr) + b_s[:, :DS].T[None, :, :, None]
    return ub * mixed.reshape(Bn, DS, B_WIDTH)


def _peer(h, w_query, sk1, sk2, e_down, e_up):
    T = h.shape[0]
    nblk = -(-T // PEER_BLOCK)
    hp = jnp.pad(h, ((0, nblk * PEER_BLOCK - T), (0, 0))).reshape(nblk, PEER_BLOCK, D_MODEL)
    half = PEER_QDIM // 2
    kk = PEER_TOPK * PEER_TOPK

    def block(hb):
        q = (hb @ w_query).reshape(PEER_BLOCK, PEER_HEADS, 2, half)
        s1 = jnp.einsum('thc,kc->thk', q[:, :, 0], sk1).astype(jnp.float32)
        s2 = jnp.einsum('thc,kc->thk', q[:, :, 1], sk2).astype(jnp.float32)
        v1, i1 = lax.top_k(s1, PEER_TOPK)
        v2, i2 = lax.top_k(s2, PEER_TOPK)
        cand = (v1[..., :, None] + v2[..., None, :]).reshape(PEER_BLOCK, PEER_HEADS, kk)
        cidx = (i1[..., :, None] * N_KEYS + i2[..., None, :]).reshape(PEER_BLOCK, PEER_HEADS, kk)
        top_s, top_pos = lax.top_k(cand, PEER_TOPK)
        eidx = jnp.take_along_axis(cidx, top_pos, axis=-1)
        gate = jax.nn.softmax(top_s, axis=-1).astype(hb.dtype)
        act = jax.nn.gelu(jnp.einsum('thed,td->the', e_down[eidx], hb)) * gate
        return jnp.einsum('the,thed->td', act, e_up[eidx])

    return lax.map(block, hp).reshape(nblk * PEER_BLOCK, D_MODEL)[:T]


def _merge_channel_ple(x, p, a_out, b_out, ga, gb, w_up_a, w_up_b, w_out, g2,
                       w_query, sk1, sk2, e_down, e_up, g3, w_pg, w_pp):
    m = jax.nn.sigmoid(ga) * (a_out @ w_up_a) + jax.nn.sigmoid(gb) * (b_out @ w_up_b)
    x = x + m @ w_out
    h2 = _rmsnorm(x, g2)
    x = x + _peer(h2.reshape(-1, D_MODEL), w_query, sk1, sk2, e_down, e_up).reshape(x.shape)
    h3 = _rmsnorm(x, g3)
    return x + jax.nn.sigmoid(h3 @ w_pg) * (p @ w_pp)


def setup_inputs(seed: int = 0) -> dict:
    key = jax.random.key(seed)
    ks = jax.random.split(key, 32)
    f32 = jnp.float32

    def nrm(k, shape, scale):
        return jax.random.normal(k, shape, f32) * scale

    def gain(k, shape):
        return 1.0 + 0.02 * jax.random.normal(k, shape, f32)

    cl = [min(w, PAST_LEN) for w in A_WINDOWS]
    kvs = (HEADS_PER_GROUP, HEAD_DIM)
    return {
        "x_prompt": nrm(ks[0], (BATCH, SEQ, D_MODEL), 1.0),
        "x_sample": nrm(ks[1], (DEC_BATCH, DEC_SEQ, D_MODEL), 1.0),
        "cache_kv_w128": nrm(ks[2], (DEPTH, DEC_BATCH, cl[0], 2) + kvs, 1.0),
        "cache_kv_w512": nrm(ks[3], (DEPTH, DEC_BATCH, cl[1], 2) + kvs, 1.0),
        "cache_kv_w2048": nrm(ks[4], (DEPTH, DEC_BATCH, cl[2], 2) + kvs, 1.0),
        "p_prompt": nrm(ks[5], (DEPTH, BATCH, SEQ, PLE_DIM), 1.0),
        "p_sample": nrm(ks[6], (DEPTH, DEC_BATCH, DEC_SEQ, PLE_DIM), 1.0),
        "rel_bias": nrm(ks[7], (N_BUCKETS, N_HEADS_A), 0.1),
        "norm1": gain(ks[8], (DEPTH, D_MODEL)),
        "w_in": nrm(ks[9], (DEPTH, D_MODEL, IN_TOTAL), D_MODEL ** -0.5),
        "q_norm": gain(ks[10], (DEPTH, N_GROUPS_A, HEAD_DIM)),
        "k_norm": gain(ks[11], (DEPTH, N_GROUPS_A, HEAD_DIM)),
        "b_v_norm": gain(ks[12], (DEPTH, B_WIDTH)),
        "w_spatial": nrm(ks[13], (DEPTH, B_GROUPS, CHUNK, CHUNK), CHUNK ** -0.5),
        "b_spatial": 1.0 + nrm(ks[14], (DEPTH, B_GROUPS, CHUNK), 0.02),
        "w_up_a": nrm(ks[15], (DEPTH, A_WIDTH, D_MODEL), A_WIDTH ** -0.5),
        "w_up_b": nrm(ks[16], (DEPTH, B_WIDTH, D_MODEL), B_WIDTH ** -0.5),
        "w_out": nrm(ks[17], (DEPTH, D_MODEL, D_MODEL), D_MODEL ** -0.5),
        "norm2": gain(ks[18], (DEPTH, D_MODEL)),
        "w_query": nrm(ks[19], (DEPTH, D_MODEL, PEER_HEADS * PEER_QDIM), D_MODEL ** -0.5),
        "sub_keys_1": nrm(ks[20], (DEPTH, N_KEYS, PEER_QDIM // 2), (PEER_QDIM // 2) ** -0.5),
        "sub_keys_2": nrm(ks[21], (DEPTH, N_KEYS, PEER_QDIM // 2), (PEER_QDIM // 2) ** -0.5),
        "expert_down": nrm(ks[22], (DEPTH, N_EXPERTS, D_MODEL), D_MODEL ** -0.5),
        "expert_up": nrm(ks[23], (DEPTH, N_EXPERTS, D_MODEL), PEER_HEADS ** -0.5),
        "norm3": gain(ks[24], (DEPTH, D_MODEL)),
        "w_ple_gate": nrm(ks[25], (DEPTH, D_MODEL, D_MODEL), D_MODEL ** -0.5),
        "w_ple_proj": nrm(ks[26], (DEPTH, PLE_DIM, D_MODEL), PLE_DIM ** -0.5),
    }


def reference(x_prompt, x_sample, cache_kv_w128, cache_kv_w512, cache_kv_w2048, p_prompt, p_sample,
              rel_bias, norm1, w_in, q_norm, k_norm, b_v_norm, w_spatial, b_spatial, w_up_a, w_up_b,
              w_out, norm2, w_query, sub_keys_1, sub_keys_2, expert_down, expert_up, norm3,
              w_ple_gate, w_ple_proj):
    caches = (cache_kv_w128, cache_kv_w512, cache_kv_w2048)
    tri = jnp.tril(jnp.ones((CHUNK, CHUNK), dtype=bool))
    xp, xs = x_prompt, x_sample
    S = x_prompt.shape[1]
    kvp = [[] for _ in range(N_GROUPS_A)]
    kvs = [[] for _ in range(N_GROUPS_A)]
    vst = []
    for i in range(DEPTH):
        w_s = jnp.where(tri[None], w_spatial[i], 0.0)
        tail = (w_up_a[i], w_up_b[i], w_out[i], norm2[i], w_query[i], sub_keys_1[i], sub_keys_2[i],
                expert_down[i], expert_up[i], norm3[i], w_ple_gate[i], w_ple_proj[i])
        q, k, v, ub, vb, ga, gb = _branch_inputs(xp, norm1[i], w_in[i], q_norm[i], k_norm[i], b_v_norm[i])
        outs, lses = [], []
        for g in range(N_GROUPS_A):
            bg = rel_bias[:, g * HEADS_PER_GROUP:(g + 1) * HEADS_PER_GROUP]
            o, l = _dilated_group_prompt(q[:, :, g], k[:, :, g], v[:, :, g], A_DILATIONS[g], A_WINDOWS[g], bg)
            outs.append(o)
            lses.append(l)
            L = min(A_WINDOWS[g], S)
            kvp[g].append(jnp.stack([k[:, S - L:, g], v[:, S - L:, g]], axis=2))
        a_out = _combine_groups(outs, lses, xp.dtype)
        b_out = _spatial_gate_prompt(ub, vb, w_s, b_spatial[i])
        xp = _merge_channel_ple(xp, p_prompt[i], a_out, b_out, ga, gb, *tail)
        q, k, v, ub, vb, ga, gb = _branch_inputs(xs, norm1[i], w_in[i], q_norm[i], k_norm[i], b_v_norm[i])
        outs, lses = [], []
        for g in range(N_GROUPS_A):
            bg = rel_bias[:, g * HEADS_PER_GROUP:(g + 1) * HEADS_PER_GROUP]
            o, l = _dilated_group_sample(q[:, :, g], k[:, :, g], v[:, :, g], caches[g][i],
                                         A_DILATIONS[g], A_WINDOWS[g], bg)
            outs.append(o)
            lses.append(l)
            kvs[g].append(jnp.stack([k[:, :, g], v[:, :, g]], axis=2))
        a_out = _combine_groups(outs, lses, xs.dtype)
        b_out = _spatial_gate_sample(ub, vb, w_s, b_spatial[i])
        vst.append(vb)
        xs = _merge_channel_ple(xs, p_sample[i], a_out, b_out, ga, gb, *tail)
    y_prompt = xp
    y_sample = xs
    kv_prompt_w128 = jnp.stack(kvp[0], 0)
    kv_prompt_w512 = jnp.stack(kvp[1], 0)
    kv_prompt_w2048 = jnp.stack(kvp[2], 0)
    kv_sample_w128 = jnp.stack(kvs[0], 0)
    kv_sample_w512 = jnp.stack(kvs[1], 0)
    kv_sample_w2048 = jnp.stack(kvs[2], 0)
    v_sample_chunk = jnp.stack(vst, 0)
    return (y_prompt, y_sample, kv_prompt_w128, kv_prompt_w512, kv_prompt_w2048,
            kv_sample_w128, kv_sample_w512, kv_sample_w2048, v_sample_chunk)
```

```python
import functools
import math

import numpy as np
import jax
import jax.numpy as jnp
from jax import lax
from jax.experimental import pallas as pl
from jax.experimental.pallas import tpu as pltpu

F32 = jnp.float32
BF16 = jnp.bfloat16

D_MODEL = 1024
N_GROUPS = 3
HEADS = 8
HEAD_DIM = 64
GROUP_W = HEADS * HEAD_DIM
QKV_W = N_GROUPS * GROUP_W
WINDOWS = (128, 512, 2048)
DILATIONS = (1, 4, 16)
BAND = 128
N_BUCKETS = 32
MAX_DISTANCE = 2048
CHUNK = 128
B_GROUPS = 4
N_KEYS = 128
PEER_HEADS = 8
PEER_TOPK = 16
PEER_HALF = 128
PLE_DIM = 256
EPS = 1e-6
NEG_INF = -1e30
IN_OFFS = (0, QKV_W, 2 * QKV_W, 3 * QKV_W, 3 * QKV_W + D_MODEL, 3 * QKV_W + 2 * D_MODEL,
           3 * QKV_W + 3 * D_MODEL, 3 * QKV_W + 4 * D_MODEL)

V7X_VMEM_LIMIT = 56 * 1024 * 1024
G_PITCH = 72


def _cparams(sem, vmem=None):
    return pltpu.CompilerParams(dimension_semantics=sem, vmem_limit_bytes=vmem)


def _resident(shape):
    nd = len(shape)
    return pl.BlockSpec(shape, lambda *_: (0,) * nd)


def _nt_dot(a, b):
    return lax.dot_general(a, b, (((1,), (1,)), ((), ())), preferred_element_type=F32)


def _rms(x, gain):
    return x * lax.rsqrt(jnp.mean(x * x, axis=-1, keepdims=True) + EPS) * gain


def _inproj_kernel(x_ref, g1_ref, w_ref, qg_ref, kg_ref, bvg_ref, hsum_ref,
                   q_ref, k_ref, v_ref, ub_ref, vb_ref, ga_ref, gb_ref):
    hb = _rms(x_ref[...], g1_ref[...]).astype(BF16)

    def proj(c0, width):
        return jnp.dot(hb, w_ref[:, c0:c0 + width], preferred_element_type=F32)

    for g in range(N_GROUPS):
        lo = g * GROUP_W
        for off, gain_ref, out_ref, scale in ((IN_OFFS[0], qg_ref, q_ref, HEAD_DIM ** -0.5),
                                              (IN_OFFS[1], kg_ref, k_ref, 1.0)):
            y = proj(off + lo, GROUP_W)
            ss = jnp.dot((y * y).astype(BF16), hsum_ref[...], preferred_element_type=F32)
            yn = y * lax.rsqrt(ss * (1.0 / HEAD_DIM) + EPS) * gain_ref[g:g + 1, :]
            if scale != 1.0:
                yn = yn * scale
            out_ref[:, lo:lo + GROUP_W] = yn.astype(out_ref.dtype)
        v_ref[:, lo:lo + GROUP_W] = proj(IN_OFFS[2] + lo, GROUP_W)

    half = D_MODEL // 2
    for c in range(2):
        ub_ref[:, c * half:(c + 1) * half] = jax.nn.gelu(proj(IN_OFFS[3] + c * half, half)).astype(ub_ref.dtype)
    vb0 = jax.nn.gelu(proj(IN_OFFS[4], half))
    vb1 = jax.nn.gelu(proj(IN_OFFS[4] + half, half))
    ms = (jnp.sum(vb0 * vb0, axis=-1, keepdims=True) + jnp.sum(vb1 * vb1, axis=-1, keepdims=True)) * (1.0 / D_MODEL)
    r = lax.rsqrt(ms + EPS)
    vb_ref[:, :half] = (vb0 * r * bvg_ref[:, :half]).astype(vb_ref.dtype)
    vb_ref[:, half:] = (vb1 * r * bvg_ref[:, half:]).astype(vb_ref.dtype)
    for c in range(2):
        ga_ref[:, c * half:(c + 1) * half] = proj(IN_OFFS[5] + c * half, half)
        gb_ref[:, c * half:(c + 1) * half] = proj(IN_OFFS[6] + c * half, half)


def _in_projection(x2d, g1, w_in_bf, qg, kg, bvg, hsum, tm, vb_dtype):
    t = x2d.shape[0]
    row = lambda w: pl.BlockSpec((tm, w), lambda i: (i, 0))
    outs = (
        jax.ShapeDtypeStruct((t, QKV_W), BF16),
        jax.ShapeDtypeStruct((t, QKV_W), F32),
        jax.ShapeDtypeStruct((t, QKV_W), F32),
        jax.ShapeDtypeStruct((t, D_MODEL), BF16),
        jax.ShapeDtypeStruct((t, D_MODEL), vb_dtype),
        jax.ShapeDtypeStruct((t, D_MODEL), F32),
        jax.ShapeDtypeStruct((t, D_MODEL), F32),
    )
    return pl.pallas_call(
        _inproj_kernel,
        out_shape=outs,
        grid=(t // tm,),
        in_specs=[row(D_MODEL), _resident((1, D_MODEL)),
                  pl.BlockSpec(w_in_bf.shape, lambda i: (0, 0), pipeline_mode=pl.Buffered(1)),
                  _resident((N_GROUPS, GROUP_W)), _resident((N_GROUPS, GROUP_W)), _resident((1, D_MODEL)),
                  _resident((GROUP_W, GROUP_W))],
        out_specs=(row(QKV_W), row(QKV_W), row(QKV_W), row(D_MODEL), row(D_MODEL), row(D_MODEL), row(D_MODEL)),
        compiler_params=_cparams(("parallel",), V7X_VMEM_LIMIT),
        name="in_projection",
    )(x2d, g1, w_in_bf, qg, kg, bvg, hsum)


def _attn_kernel(q_ref, kp_ref, kc_ref, vp_ref, vc_ref, add_ref, o_ref, lse_ref):
    i = pl.program_id(2)
    q = q_ref[0]
    kk = jnp.concatenate([kp_ref[0], kc_ref[0]], axis=0).astype(BF16)
    vv = jnp.concatenate([vp_ref[0], vc_ref[0]], axis=0).astype(BF16)
    lane = lax.broadcasted_iota(jnp.int32, (BAND, 128), 1)
    kcol = lax.broadcasted_iota(jnp.int32, (BAND, 2 * BAND), 1)
    no_prev = kcol < jnp.where(i == 0, BAND, 0)
    for pair in range(HEADS // 2):
        sl = slice(pair * 128, (pair + 1) * 128)
        qp, kp, vp = q[:, sl].astype(F32), kk[:, sl], vv[:, sl]
        o_pair = jnp.zeros((BAND, 128), F32)
        l_pair = jnp.zeros((BAND, 128), F32)
        for hh in range(2):
            mine = (lane >= HEAD_DIM) if hh else (lane < HEAD_DIM)
            s = _nt_dot(jnp.where(mine, qp, 0.0).astype(BF16), kp) + add_ref[pair * 2 + hh]
            s = jnp.where(no_prev, NEG_INF, s)
            m = jnp.max(s, axis=-1, keepdims=True)
            p = jnp.exp(s - m)
            den = jnp.sum(p, axis=-1, keepdims=True)
            o = jnp.dot(p.astype(BF16), vp, preferred_element_type=F32) / den
            o_pair = jnp.where(mine, o, o_pair)
            l_pair = jnp.where(mine, m + jnp.log(den), l_pair)
        o_ref[0, :, sl] = o_pair.astype(o_ref.dtype)
        lse_ref[0, :, sl] = l_pair


def _prompt_attention(q3, k3, v3, add, dil, g):
    b, n, _ = q3.shape
    nb = n // BAND
    cur = pl.BlockSpec((1, BAND, GROUP_W), lambda bi, r, i: (bi, i, r * N_GROUPS + g))
    prev = pl.BlockSpec((1, BAND, GROUP_W), lambda bi, r, i: (bi, jnp.maximum(i - 1, 0), r * N_GROUPS + g))
    out = pl.BlockSpec((1, BAND, GROUP_W), lambda bi, r, i: (bi, i, r))
    oshape = (b, n, dil * GROUP_W)
    return pl.pallas_call(
        _attn_kernel,
        out_shape=(jax.ShapeDtypeStruct(oshape, BF16), jax.ShapeDtypeStruct(oshape, F32)),
        grid=(b, dil, nb),
        in_specs=[cur, prev, cur, prev, cur, _resident(add.shape)],
        out_specs=(out, out),
        compiler_params=_cparams(("parallel", "parallel", "arbitrary")),
        name=f"attn_dil{dil}",
    )(q3, k3, k3, v3, v3, add)


def _sample_attn_kernel(q_ref, kn_ref, vn_ref, kv_ref, addc_ref, addn_ref, o_ref, lse_ref):
    head_of_lane = lax.broadcasted_iota(jnp.int32, (HEADS, GROUP_W), 1) // HEAD_DIM
    mine = head_of_lane == lax.broadcasted_iota(jnp.int32, (HEADS, GROUP_W), 0)
    qb = q_ref[0]
    qm = jnp.where(mine, jnp.broadcast_to(qb.astype(F32), (HEADS, GROUP_W)), 0.0).astype(BF16)
    kc = kv_ref[0, :, :GROUP_W].astype(BF16)
    vc = kv_ref[0, :, GROUP_W:].astype(BF16)
    kn = kn_ref[0].astype(BF16).astype(F32)
    vn = vn_ref[0].astype(BF16).astype(F32)
    s_c = _nt_dot(qm, kc) + addc_ref[...]
    s_n = jnp.sum(qm.astype(F32) * kn, axis=-1, keepdims=True) + addn_ref[...]
    m = jnp.maximum(jnp.max(s_c, axis=-1, keepdims=True), s_n)
    p_c = jnp.exp(s_c - m)
    p_n = jnp.exp(s_n - m)
    den = jnp.sum(p_c, axis=-1, keepdims=True) + p_n
    o_all = (jnp.dot(p_c.astype(BF16), vc, preferred_element_type=F32)
             + p_n.astype(BF16).astype(F32) * vn) / den
    lse = m + jnp.log(den)
    o_ref[0] = jnp.sum(jnp.where(mine, o_all, 0.0), axis=0, keepdims=True)
    lse_ref[0] = jnp.sum(jnp.where(mine, jnp.broadcast_to(lse, (HEADS, GROUP_W)), 0.0), axis=0, keepdims=True)


def _sample_attention(q_s, k_s, v_s, cache_view, addc, addn, g):
    db = q_s.shape[0]
    tok = pl.BlockSpec((1, 1, GROUP_W), lambda bi: (bi, 0, g))
    out = pl.BlockSpec((1, 1, GROUP_W), lambda bi: (bi, 0, 0))
    return pl.pallas_call(
        _sample_attn_kernel,
        out_shape=(jax.ShapeDtypeStruct((db, 1, GROUP_W), F32), jax.ShapeDtypeStruct((db, 1, GROUP_W), F32)),
        grid=(db,),
        in_specs=[tok, tok, tok, pl.BlockSpec((1, BAND, 2 * GROUP_W), lambda bi: (bi, 0, 0)),
                  _resident(addc.shape), _resident(addn.shape)],
        out_specs=(out, out),
        compiler_params=_cparams(("parallel",)),
        name=f"sample_attn_g{g}",
    )(q_s, k_s, v_s, cache_view, addc, addn)


def _merge_kernel(x_ref, o0_ref, o1_ref, o2_ref, l0_ref, l1_ref, l2_ref, ub_ref, vb_ref, ga_ref, gb_ref,
                  ws_ref, bs_ref, wua_ref, wub_ref, wo_ref, g2_ref, wq_ref,
                  x1_ref, h2_ref, pq_ref, *, sample):
    tm = x_ref.shape[0]
    l0, l1, l2 = l0_ref[...], l1_ref[...], l2_ref[...]
    m = jnp.maximum(jnp.maximum(l0, l1), l2)
    e0, e1, e2 = jnp.exp(l0 - m), jnp.exp(l1 - m), jnp.exp(l2 - m)
    a_out = (e0 * o0_ref[...].astype(F32) + e1 * o1_ref[...].astype(F32) + e2 * o2_ref[...].astype(F32)) / (e0 + e1 + e2)

    gw = D_MODEL // B_GROUPS
    if sample:
        mixed = vb_ref[...].astype(F32) * ws_ref[...] + bs_ref[...]
        b_out = ub_ref[...].astype(F32) * mixed
    else:
        tri = lax.broadcasted_iota(jnp.int32, (CHUNK, CHUNK), 0) >= lax.broadcasted_iota(jnp.int32, (CHUNK, CHUNK), 1)
        parts = []
        for c in range(tm // CHUNK):
            rows = slice(c * CHUNK, (c + 1) * CHUNK)
            cols = []
            for g in range(B_GROUPS):
                w = jnp.where(tri, ws_ref[g], 0.0).astype(BF16)
                cols.append(jnp.dot(w, vb_ref[rows, g * gw:(g + 1) * gw].astype(BF16), preferred_element_type=F32))
            mixed = jnp.concatenate(cols, axis=1) + bs_ref[...]
            parts.append(ub_ref[rows, :].astype(F32) * mixed)
        b_out = jnp.concatenate(parts, axis=0)

    up_a = jnp.dot(a_out.astype(BF16), wua_ref[...], preferred_element_type=F32)
    up_b = jnp.dot(b_out.astype(BF16), wub_ref[...], preferred_element_type=F32)
    mrg = jax.nn.sigmoid(ga_ref[...]) * up_a + jax.nn.sigmoid(gb_ref[...]) * up_b
    x1 = x_ref[...] + jnp.dot(mrg.astype(BF16), wo_ref[...], preferred_element_type=F32)
    x1_ref[...] = x1
    h2 = _rms(x1, g2_ref[...]).astype(BF16)
    h2_ref[...] = h2
    pq_ref[...] = jnp.dot(h2, wq_ref[...], preferred_element_type=F32).astype(pq_ref.dtype)


def _merge(x2d, o, lse, ub, vb, ga, gb, ws, bs, wua, wub, wo, g2, wq, tm, sample):
    t = x2d.shape[0]
    row = lambda w: pl.BlockSpec((tm, w), lambda i: (i, 0))
    qw = wq.shape[1]
    return pl.pallas_call(
        functools.partial(_merge_kernel, sample=sample),
        out_shape=(jax.ShapeDtypeStruct((t, D_MODEL), F32), jax.ShapeDtypeStruct((t, D_MODEL), BF16),
                   jax.ShapeDtypeStruct((t, qw), BF16)),
        grid=(t // tm,),
        in_specs=[row(D_MODEL)] + [row(GROUP_W)] * 6 + [row(D_MODEL)] * 4
                 + [_resident(ws.shape), _resident(bs.shape), _resident(wua.shape), _resident(wub.shape),
                    _resident(wo.shape), _resident(g2.shape), _resident(wq.shape)],
        out_specs=(row(D_MODEL), row(D_MODEL), row(qw)),
        compiler_params=_cparams(("parallel",), V7X_VMEM_LIMIT),
        name="merge_sample" if sample else "merge",
    )(x2d, *o, *lse, ub, vb, ga, gb, ws, bs, wua, wub, wo, g2, wq)


def _top16_rows(s, idx_col):
    vals, idxs = [], []
    big = jnp.float32(1e9)
    for _ in range(PEER_TOPK):
        m = jnp.max(s, axis=0, keepdims=True)
        sel = jnp.min(jnp.where(s == m, idx_col, big), axis=0, keepdims=True)
        s = jnp.where(idx_col == sel, -jnp.inf, s)
        vals.append(m)
        idxs.append(sel)
    return vals, idxs


def _route_kernel(pq_ref, sk1_ref, sk2_ref, i1_ref, i2_ref, gate_ref):
    lanes = 128
    key_col = lax.broadcasted_iota(jnp.int32, (N_KEYS, lanes), 0).astype(F32)
    row8 = lax.broadcasted_iota(jnp.int32, (8, lanes), 0).astype(F32)
    row16 = lax.broadcasted_iota(jnp.int32, (PEER_TOPK, lanes), 0).astype(F32)
    for c in range(pq_ref.shape[0] // lanes):
        rows = slice(c * lanes, (c + 1) * lanes)
        qh = pq_ref[rows, :]
        s1 = _nt_dot(sk1_ref[...], qh[:, :PEER_HALF])
        s2 = _nt_dot(sk2_ref[...], qh[:, PEER_HALF:])
        v1, i1 = _top16_rows(s1, key_col)
        v2, i2 = _top16_rows(s2, key_col)
        v2a = jnp.concatenate(v2, axis=0)
        i2a = jnp.concatenate(i2, axis=0)
        cand = [v1[0] + v2a]
        flat = [row16]
        code = [i1[0] * N_KEYS + i2a]
        for j1 in range(1, 8):
            cand.append(v1[j1] + v2a[:8])
            flat.append(row8 + float(j1 * PEER_TOPK))
            code.append(i1[j1] * N_KEYS + i2a[:8])
        cand.append(jnp.concatenate(v1[8:], axis=0) + v2[0])
        flat.append((row8 + 8.0) * float(PEER_TOPK))
        code.append(jnp.concatenate(i1[8:], axis=0) * N_KEYS + i2[0])
        cand = jnp.concatenate(cand, axis=0)
        flat = jnp.concatenate(flat, axis=0)
        code = jnp.concatenate(code, axis=0)
        top_s, top_e = [], []
        for _ in range(PEER_TOPK):
            m = jnp.max(cand, axis=0, keepdims=True)
            sel = jnp.min(jnp.where(cand == m, flat, 1e9), axis=0, keepdims=True)
            hit = flat == sel
            top_e.append(jnp.max(jnp.where(hit, code, -1.0), axis=0, keepdims=True))
            cand = jnp.where(hit, -jnp.inf, cand)
            top_s.append(m)
        ts = jnp.concatenate(top_s, axis=0)
        te = jnp.concatenate(top_e, axis=0)
        e = jnp.exp(ts - ts[0:1])
        gate = e / jnp.sum(e, axis=0, keepdims=True)
        a = jnp.floor(te * (1.0 / N_KEYS))
        i1_ref[:, rows] = a
        i2_ref[:, rows] = te - a * N_KEYS
        gate_ref[:, rows] = gate


def _route(pq, sk1_bf, sk2_bf, tm):
    t = pq.shape[0]
    out = pl.BlockSpec((PEER_TOPK, tm), lambda i, h: (h, i))
    shp = jax.ShapeDtypeStruct((PEER_HEADS * PEER_TOPK, t), F32)
    return pl.pallas_call(
        _route_kernel,
        out_shape=(shp, shp, shp),
        grid=(t // tm, PEER_HEADS),
        in_specs=[pl.BlockSpec((tm, 2 * PEER_HALF), lambda i, h: (i, h)),
                  _resident(sk1_bf.shape), _resident(sk2_bf.shape)],
        out_specs=(out, out, out),
        compiler_params=_cparams(("parallel", "parallel")),
        name="peer_route",
    )(pq, sk1_bf, sk2_bf)


def _peer_kernel(h2_ref, x1_ref, i1_ref, i2_ref, gt_ref, p_ref, edl_ref, edh_ref, eul_ref, euh_ref,
                 g3_ref, wpg_ref, wpp_ref, y_ref, pk_ref, acc_ref, i1s_ref, i2s_ref, gts_ref, *, a_per_step):
    tm = h2_ref.shape[0]
    j = pl.program_id(1)

    @pl.when(j == 0)
    def _build_gates():
        i1s_ref[...] = i1_ref[...].T
        i2s_ref[...] = i2_ref[...].T
        gts_ref[...] = gt_ref[...].T
        acc_ref[...] = jnp.zeros_like(acc_ref)
        sub = lax.broadcasted_iota(jnp.int32, (N_KEYS, N_KEYS), 0).astype(F32)

        def body(t, carry):
            r1 = i1s_ref[pl.ds(t, 1), :]
            r2 = i2s_ref[pl.ds(t, 1), :]
            rg = gts_ref[pl.ds(t, 1), :]
            oat = jnp.where(sub == r1, 1.0, 0.0).astype(BF16)
            gbt = jnp.where(sub == r2, rg, 0.0).astype(BF16)
            g = _nt_dot(oat, gbt)
            lo = pltpu.bitcast(g[:N_KEYS // 2], jnp.uint32) + jnp.uint32(0x8000)
            hi = pltpu.bitcast(g[N_KEYS // 2:], jnp.uint32) + jnp.uint32(0x8000)
            packed = (lo >> 16) | (hi & jnp.uint32(0xFFFF0000))
            pk_ref[pl.ds(pl.multiple_of(t * G_PITCH, 8), N_KEYS // 2), :] = packed
            return carry

        lax.fori_loop(0, tm, body, 0)

    h2 = h2_ref[...]
    for r in range(a_per_step):
        cols = slice(r * N_KEYS, (r + 1) * N_KEYS)
        w = pk_ref[pl.ds(j * a_per_step + r, tm, stride=G_PITCH), :]
        g_lo = pltpu.bitcast(w << 16, F32)
        g_hi = pltpu.bitcast(w & jnp.uint32(0xFFFF0000), F32)
        s_lo = jnp.dot(h2, edl_ref[:, cols], preferred_element_type=F32)
        s_hi = jnp.dot(h2, edh_ref[:, cols], preferred_element_type=F32)
        a_lo = (jax.nn.gelu(s_lo) * g_lo).astype(BF16)
        a_hi = (jax.nn.gelu(s_hi) * g_hi).astype(BF16)
        acc_ref[...] += (jnp.dot(a_lo, eul_ref[cols, :], preferred_element_type=F32)
                         + jnp.dot(a_hi, euh_ref[cols, :], preferred_element_type=F32))

    @pl.when(j == pl.num_programs(1) - 1)
    def _finish():
        x2 = x1_ref[...] + acc_ref[...]
        h3 = _rms(x2, g3_ref[...]).astype(BF16)
        gate = jax.nn.sigmoid(jnp.dot(h3, wpg_ref[...], preferred_element_type=F32))
        y_ref[...] = x2 + gate * jnp.dot(p_ref[...].astype(BF16), wpp_ref[...], preferred_element_type=F32)


def _peer_ple(h2, x1, i1, i2, gt, p2d, ed_t, eu, g3, wpg, wpp, tm, a_per_step):
    t = h2.shape[0]
    steps = (N_KEYS // 2) // a_per_step
    nb = a_per_step * N_KEYS
    row = lambda w: pl.BlockSpec((tm, w), lambda i, j: (i, 0))
    pick = pl.BlockSpec((PEER_HEADS * PEER_TOPK, tm), lambda i, j: (0, i))
    return pl.pallas_call(
        functools.partial(_peer_kernel, a_per_step=a_per_step),
        out_shape=jax.ShapeDtypeStruct((t, D_MODEL), F32),
        grid=(t // tm, steps),
        in_specs=[row(D_MODEL), row(D_MODEL), pick, pick, pick, row(PLE_DIM),
                  pl.BlockSpec((D_MODEL, nb), lambda i, j: (0, j)),
                  pl.BlockSpec((D_MODEL, nb), lambda i, j: (0, j + steps)),
                  pl.BlockSpec((nb, D_MODEL), lambda i, j: (j, 0)),
                  pl.BlockSpec((nb, D_MODEL), lambda i, j: (j + steps, 0)),
                  _resident(g3.shape), _resident(wpg.shape), _resident(wpp.shape)],
        out_specs=row(D_MODEL),
        scratch_shapes=[pltpu.VMEM((tm * G_PITCH, N_KEYS), jnp.uint32),
                        pltpu.VMEM((tm, D_MODEL), F32),
                        pltpu.VMEM((tm, N_KEYS), F32), pltpu.VMEM((tm, N_KEYS), F32), pltpu.VMEM((tm, N_KEYS), F32)],
        compiler_params=_cparams(("parallel", "arbitrary"), V7X_VMEM_LIMIT),
        name="peer_ple",
    )(h2, x1, i1, i2, gt, p2d, ed_t, ed_t, eu, eu, g3, wpg, wpp)


def _t5_bucket(dist):
    max_exact = N_BUCKETS // 2
    df = jnp.maximum(dist, max_exact).astype(F32)
    large = max_exact + (jnp.log(df / max_exact) / math.log(MAX_DISTANCE / max_exact)
                         * (N_BUCKETS - max_exact)).astype(jnp.int32)
    return jnp.where(dist < max_exact, dist, jnp.minimum(large, N_BUCKETS - 1))


def _bias_tables(rel_bias, g):
    dil = DILATIONS[g]
    per_dist = rel_bias[:, g * HEADS:(g + 1) * HEADS][_t5_bucket(jnp.arange(BAND + 1) * dil)].astype(F32)
    dsub = np.arange(BAND)[:, None] + BAND - np.arange(2 * BAND)[None, :]
    valid = (dsub >= 0) & (dsub <= WINDOWS[g] // dil)
    band = jnp.moveaxis(per_dist[np.clip(dsub, 0, BAND)], -1, 0)
    add = jnp.where(jnp.asarray(valid)[None], band, NEG_INF)
    addc = per_dist[BAND - np.arange(BAND)].T
    addn = per_dist[0][:, None]
    return add, addc, addn


def kernel(x_prompt, x_sample, cache_kv_w128, cache_kv_w512, cache_kv_w2048, p_prompt, p_sample, rel_bias, norm1,
           w_in, q_norm, k_norm, b_v_norm, w_spatial, b_spatial, w_up_a, w_up_b, w_out, norm2, w_query,
           sub_keys_1, sub_keys_2, expert_down, expert_up, norm3, w_ple_gate, w_ple_proj):
    bsz, seq, _ = x_prompt.shape
    db, ds, _ = x_sample.shape
    depth = norm1.shape[0]
    assert depth == 1 and ds == 1 and seq % (BAND * DILATIONS[-1]) == 0
    caches = (cache_kv_w128, cache_kv_w512, cache_kv_w2048)
    for g in range(N_GROUPS):
        assert caches[g].shape[2] == WINDOWS[g]
    t_p = bsz * seq
    t_s = 128
    li = 0

    g1 = norm1[li][None, :]
    w_in_bf = w_in[li].astype(BF16)
    qg = jnp.tile(q_norm[li], (1, HEADS))
    kg = jnp.tile(k_norm[li], (1, HEADS))
    bvg = b_v_norm[li][None, :]
    blk = np.arange(GROUP_W) // HEAD_DIM
    hsum = jnp.asarray(blk[:, None] == blk[None, :], dtype=BF16)
    wua, wub, wo = w_up_a[li].astype(BF16), w_up_b[li].astype(BF16), w_out[li].astype(BF16)
    g2, g3 = norm2[li][None, :], norm3[li][None, :]
    wq = w_query[li].astype(BF16)
    sk1, sk2 = sub_keys_1[li].astype(BF16), sub_keys_2[li].astype(BF16)
    ed_t = expert_down[li].astype(BF16).T
    eu = expert_up[li].astype(BF16)
    wpg, wpp = w_ple_gate[li].astype(BF16), w_ple_proj[li].astype(BF16)
    bs_full = jnp.repeat(b_spatial[li].T, D_MODEL // B_GROUPS, axis=1)
    tables = [_bias_tables(rel_bias, g) for g in range(N_GROUPS)]

    xp = x_prompt.reshape(t_p, D_MODEL)
    q, k, v, ub, vb, ga, gb = _in_projection(xp, g1, w_in_bf, qg, kg, bvg, hsum, 256, BF16)
    outs, lses, kv_prompt = [], [], []
    for g in range(N_GROUPS):
        dil = DILATIONS[g]
        cols = slice(g * GROUP_W, (g + 1) * GROUP_W)
        view = lambda a: a.reshape(bsz, seq // dil, dil * QKV_W)
        o, lse = _prompt_attention(view(q), view(k), view(v), tables[g][0], dil, g)
        outs.append(o.reshape(t_p, GROUP_W))
        lses.append(lse.reshape(t_p, GROUP_W))
        ln = min(WINDOWS[g], seq)
        kt = k[:, cols].reshape(bsz, seq, HEADS, HEAD_DIM)[:, seq - ln:]
        vt = v[:, cols].reshape(bsz, seq, HEADS, HEAD_DIM)[:, seq - ln:]
        kv_prompt.append(jnp.stack([kt, vt], axis=2)[None])
    x1, h2, pq = _merge(xp, outs, lses, ub, vb, ga, gb, w_spatial[li], bs_full, wua, wub, wo, g2, wq, 256, False)
    i1, i2, gt = _route(pq, sk1, sk2, 256)
    y_prompt = _peer_ple(h2, x1, i1, i2, gt, p_prompt[li].reshape(t_p, PLE_DIM), ed_t, eu, g3, wpg, wpp, 512, 4)
    y_prompt = y_prompt.reshape(bsz, seq, D_MODEL)

    xs = jnp.pad(x_sample.reshape(db, D_MODEL), ((0, t_s - db), (0, 0)))
    q, k, v, ub, vb, ga, gb = _in_projection(xs, g1, w_in_bf, qg, kg, bvg, hsum, t_s, F32)
    outs, lses, kv_sample = [], [], []
    for g in range(N_GROUPS):
        dil = DILATIONS[g]
        cols = slice(g * GROUP_W, (g + 1) * GROUP_W)
        cache_view = caches[g][li].reshape(db, WINDOWS[g] // dil, dil * 2 * GROUP_W)
        o, lse = _sample_attention(q[:db, None, :], k[:db, None, :], v[:db, None, :], cache_view,
                                   tables[g][1], tables[g][2], g)
        pad = lambda a: jnp.pad(a.reshape(db, GROUP_W), ((0, t_s - db), (0, 0)))
        outs.append(pad(o))
        lses.append(pad(lse))
        kn = k[:db, cols].reshape(db, 1, HEADS, HEAD_DIM)
        vn = v[:db, cols].reshape(db, 1, HEADS, HEAD_DIM)
        kv_sample.append(jnp.stack([kn, vn], axis=2)[None])
    ws0 = jnp.repeat(w_spatial[li][:, 0, 0], D_MODEL // B_GROUPS)[None, :]
    bs0 = bs_full[0:1, :]
    x1, h2, pq = _merge(xs, outs, lses, ub, vb, ga, gb, ws0, bs0, wua, wub, wo, g2, wq, t_s, True)
    i1, i2, gt = _route(pq, sk1, sk2, t_s)
    ps = jnp.pad(p_sample[li].reshape(db, PLE_DIM), ((0, t_s - db), (0, 0)))
    y_sample = _peer_ple(h2, x1, i1, i2, gt, ps, ed_t, eu, g3, wpg, wpp, t_s, 4)[:db].reshape(db, ds, D_MODEL)
    v_sample_chunk = vb[:db].reshape(1, db, ds, D_MODEL)

    return (y_prompt, y_sample, kv_prompt[0], kv_prompt[1], kv_prompt[2],
            kv_sample[0], kv_sample[1], kv_sample[2], v_sample_chunk)
```

```python
import functools
import math

import numpy as np
import jax
import jax.numpy as jnp
from jax import lax
from jax.experimental import pallas as pl
from jax.experimental.pallas import tpu as pltpu

F32 = jnp.float32
BF16 = jnp.bfloat16

D_MODEL = 1024
N_GROUPS = 3
HEADS = 8
HEAD_DIM = 64
GROUP_W = HEADS * HEAD_DIM
QKV_W = N_GROUPS * GROUP_W
WINDOWS = (128, 512, 2048)
DILATIONS = (1, 4, 16)
BAND = 128
N_BUCKETS = 32
MAX_DISTANCE = 2048
CHUNK = 128
B_GROUPS = 4
N_KEYS = 128
PEER_HEADS = 8
PEER_TOPK = 16
PEER_HALF = 128
PLE_DIM = 256
EPS = 1e-6
NEG_INF = -1e30
IN_OFFS = (0, QKV_W, 2 * QKV_W, 3 * QKV_W, 3 * QKV_W + D_MODEL, 3 * QKV_W + 2 * D_MODEL,
           3 * QKV_W + 3 * D_MODEL, 3 * QKV_W + 4 * D_MODEL)

LANES = 128
SLABS = GROUP_W // LANES
V7X_VMEM_LIMIT = 56 * 1024 * 1024
G_PITCH = 72
TM_PROJ = 256
TM_PEER = 512
PAIRS_PER_STEP = 4


def _cparams(sem, vmem=None):
    return pltpu.CompilerParams(dimension_semantics=sem, vmem_limit_bytes=vmem)


def _resident(shape):
    nd = len(shape)
    return pl.BlockSpec(shape, lambda *_: (0,) * nd)


def _nt_dot(a, b):
    return lax.dot_general(a, b, (((1,), (1,)), ((), ())), preferred_element_type=F32)


def _rms(x, gain):
    return x * lax.rsqrt(jnp.mean(x * x, axis=-1, keepdims=True) + EPS) * gain


def _head_norm(y, hsum_ref, gain_row):
    ss = jnp.dot((y * y).astype(BF16), hsum_ref[...], preferred_element_type=F32)
    return y * lax.rsqrt(ss * (1.0 / HEAD_DIM) + EPS) * gain_row


def _store_residue_major(out_ref, y, scr_ref, dil):
    if dil == 1:
        out_ref[0, 0] = y.astype(out_ref.dtype)
        return
    n = y.shape[0] // dil
    for s in range(SLABS):
        scr_ref[s] = y[:, s * LANES:(s + 1) * LANES]
    for r in range(dil):
        for s in range(SLABS):
            out_ref[0, r, :, s * LANES:(s + 1) * LANES] = scr_ref[s, pl.ds(r, n, stride=dil), :].astype(out_ref.dtype)


def _gmlp_and_gates(proj, bvg_ref, ub_ref, vb_ref, ga_ref, gb_ref, st):
    half = D_MODEL // 2
    for c in range(2):
        st(ub_ref, c * half, jax.nn.gelu(proj(IN_OFFS[3] + c * half, half)))
    vb0 = jax.nn.gelu(proj(IN_OFFS[4], half))
    vb1 = jax.nn.gelu(proj(IN_OFFS[4] + half, half))
    ms = (jnp.sum(vb0 * vb0, axis=-1, keepdims=True) + jnp.sum(vb1 * vb1, axis=-1, keepdims=True)) * (1.0 / D_MODEL)
    r = lax.rsqrt(ms + EPS)
    st(vb_ref, 0, vb0 * r * bvg_ref[:, :half])
    st(vb_ref, half, vb1 * r * bvg_ref[:, half:])
    for c in range(2):
        st(ga_ref, c * half, proj(IN_OFFS[5] + c * half, half))
        st(gb_ref, c * half, proj(IN_OFFS[6] + c * half, half))


def _inproj_prompt_kernel(x_ref, g1_ref, w_ref, qg_ref, kg_ref, bvg_ref, hsum_ref,
                          q0_ref, q1_ref, q2_ref, k0_ref, k1_ref, k2_ref, v0_ref, v1_ref, v2_ref,
                          t0_ref, t1_ref, t2_ref, ub_ref, vb_ref, ga_ref, gb_ref, scr_ref, *, tail_first):
    ti = pl.program_id(1)
    tm = x_ref.shape[1]
    hb = _rms(x_ref[0], g1_ref[...]).astype(BF16)

    def proj(c0, width):
        return jnp.dot(hb, w_ref[:, c0:c0 + width], preferred_element_type=F32)

    q_refs, k_refs, v_refs, t_refs = (q0_ref, q1_ref, q2_ref), (k0_ref, k1_ref, k2_ref), (v0_ref, v1_ref, v2_ref), \
        (t0_ref, t1_ref, t2_ref)
    for g in range(N_GROUPS):
        lo = g * GROUP_W
        dil = DILATIONS[g]
        qn = _head_norm(proj(IN_OFFS[0] + lo, GROUP_W), hsum_ref, qg_ref[g:g + 1, :]) * (HEAD_DIM ** -0.5)
        _store_residue_major(q_refs[g], qn, scr_ref, dil)
        kn = _head_norm(proj(IN_OFFS[1] + lo, GROUP_W), hsum_ref, kg_ref[g:g + 1, :])
        _store_residue_major(k_refs[g], kn, scr_ref, dil)
        vv = proj(IN_OFFS[2] + lo, GROUP_W)
        _store_residue_major(v_refs[g], vv, scr_ref, dil)
        rows = t_refs[g].shape[1]

        @pl.when(ti >= tail_first[g])
        def _():
            t_refs[g][0, :, :GROUP_W] = kn[tm - rows:, :]
            t_refs[g][0, :, GROUP_W:] = vv[tm - rows:, :]

    def st(ref, c0, val):
        ref[0, :, c0:c0 + val.shape[1]] = val.astype(ref.dtype)

    _gmlp_and_gates(proj, bvg_ref, ub_ref, vb_ref, ga_ref, gb_ref, st)


def _inproj_sample_kernel(x_ref, g1_ref, w_ref, qg_ref, kg_ref, bvg_ref, hsum_ref,
                          q_ref, k_ref, v_ref, ub_ref, vb_ref, ga_ref, gb_ref):
    hb = _rms(x_ref[...], g1_ref[...]).astype(BF16)

    def proj(c0, width):
        return jnp.dot(hb, w_ref[:, c0:c0 + width], preferred_element_type=F32)

    for g in range(N_GROUPS):
        lo = g * GROUP_W
        qn = _head_norm(proj(IN_OFFS[0] + lo, GROUP_W), hsum_ref, qg_ref[g:g + 1, :]) * (HEAD_DIM ** -0.5)
        q_ref[:, lo:lo + GROUP_W] = qn.astype(q_ref.dtype)
        k_ref[:, lo:lo + GROUP_W] = _head_norm(proj(IN_OFFS[1] + lo, GROUP_W), hsum_ref, kg_ref[g:g + 1, :])
        v_ref[:, lo:lo + GROUP_W] = proj(IN_OFFS[2] + lo, GROUP_W)

    def st(ref, c0, val):
        ref[:, c0:c0 + val.shape[1]] = val.astype(ref.dtype)

    _gmlp_and_gates(proj, bvg_ref, ub_ref, vb_ref, ga_ref, gb_ref, st)


def _proj_weight_specs(w_in_bf):
    return [_resident((1, D_MODEL)),
            pl.BlockSpec(w_in_bf.shape, lambda *_: (0, 0), pipeline_mode=pl.Buffered(1)),
            _resident((N_GROUPS, GROUP_W)), _resident((N_GROUPS, GROUP_W)), _resident((1, D_MODEL)),
            _resident((GROUP_W, GROUP_W))]


def _in_projection_prompt(x, g1, w_in_bf, qg, kg, bvg, hsum, tm):
    b, s, _ = x.shape
    nt = s // tm
    row = pl.BlockSpec((1, tm, D_MODEL), lambda bi, ti: (bi, ti, 0))
    qkv_shapes, qkv_specs, tail_shapes, tail_specs, tail_first = [], [], [], [], []
    for g in range(N_GROUPS):
        dil = DILATIONS[g]
        qkv_shapes.append(jax.ShapeDtypeStruct((b, dil, s // dil, GROUP_W), BF16))
        qkv_specs.append(pl.BlockSpec((1, dil, tm // dil, GROUP_W), lambda bi, ti: (bi, 0, ti, 0)))
        ln = min(WINDOWS[g], s)
        rows = min(ln, tm)
        first = (s - ln) // tm if ln >= tm else nt - 1
        tail_first.append(first)
        tail_shapes.append(jax.ShapeDtypeStruct((b, ln, 2 * GROUP_W), F32))
        tail_specs.append(pl.BlockSpec((1, rows, 2 * GROUP_W),
                                       functools.partial(lambda bi, ti, f: (bi, jnp.maximum(ti - f, 0), 0), f=first)))
    act = jax.ShapeDtypeStruct((b, s, D_MODEL), BF16)
    gate = jax.ShapeDtypeStruct((b, s, D_MODEL), F32)
    return pl.pallas_call(
        functools.partial(_inproj_prompt_kernel, tail_first=tuple(tail_first)),
        out_shape=tuple(qkv_shapes * 3 + tail_shapes + [act, act, gate, gate]),
        grid=(b, nt),
        in_specs=[row] + _proj_weight_specs(w_in_bf),
        out_specs=tuple(qkv_specs * 3 + tail_specs + [row] * 4),
        scratch_shapes=[pltpu.VMEM((SLABS, tm, LANES), F32)],
        compiler_params=_cparams(("arbitrary", "arbitrary"), V7X_VMEM_LIMIT),
        name="in_projection",
    )(x, g1, w_in_bf, qg, kg, bvg, hsum)


def _in_projection_sample(x2d, g1, w_in_bf, qg, kg, bvg, hsum):
    t = x2d.shape[0]
    row = lambda w: pl.BlockSpec((t, w), lambda i: (0, 0))
    outs = (jax.ShapeDtypeStruct((t, QKV_W), BF16), jax.ShapeDtypeStruct((t, QKV_W), F32),
            jax.ShapeDtypeStruct((t, QKV_W), F32), jax.ShapeDtypeStruct((t, D_MODEL), BF16),
            jax.ShapeDtypeStruct((t, D_MODEL), F32), jax.ShapeDtypeStruct((t, D_MODEL), F32),
            jax.ShapeDtypeStruct((t, D_MODEL), F32))
    return pl.pallas_call(
        _inproj_sample_kernel,
        out_shape=outs,
        grid=(1,),
        in_specs=[row(D_MODEL)] + _proj_weight_specs(w_in_bf),
        out_specs=(row(QKV_W), row(QKV_W), row(QKV_W), row(D_MODEL), row(D_MODEL), row(D_MODEL), row(D_MODEL)),
        compiler_params=_cparams(("arbitrary",), V7X_VMEM_LIMIT),
        name="in_projection_sample",
    )(x2d, g1, w_in_bf, qg, kg, bvg, hsum)


def _attn_kernel(q_ref, kp_ref, kc_ref, vp_ref, vc_ref, add_ref, o_ref, lse_ref):
    i = pl.program_id(2)
    q = q_ref[0, 0]
    kk = jnp.concatenate([kp_ref[0, 0], kc_ref[0, 0]], axis=0)
    vv = jnp.concatenate([vp_ref[0, 0], vc_ref[0, 0]], axis=0)
    lane = lax.broadcasted_iota(jnp.int32, (BAND, LANES), 1)
    kcol = lax.broadcasted_iota(jnp.int32, (BAND, 2 * BAND), 1)
    no_prev = kcol < jnp.where(i == 0, BAND, 0)
    for pair in range(HEADS // 2):
        sl = slice(pair * LANES, (pair + 1) * LANES)
        qp, kp, vp = q[:, sl].astype(F32), kk[:, sl], vv[:, sl]
        o_pair = jnp.zeros((BAND, LANES), F32)
        l_pair = jnp.zeros((BAND, LANES), F32)
        for hh in range(2):
            mine = (lane >= HEAD_DIM) if hh else (lane < HEAD_DIM)
            s = _nt_dot(jnp.where(mine, qp, 0.0).astype(BF16), kp) + add_ref[pair * 2 + hh]
            s = jnp.where(no_prev, NEG_INF, s)
            m = jnp.max(s, axis=-1, keepdims=True)
            p = jnp.exp(s - m)
            den = jnp.sum(p, axis=-1, keepdims=True)
            o = jnp.dot(p.astype(BF16), vp, preferred_element_type=F32) / den
            o_pair = jnp.where(mine, o, o_pair)
            l_pair = jnp.where(mine, m + jnp.log(den), l_pair)
        o_ref[0, 0, :, sl] = o_pair.astype(o_ref.dtype)
        lse_ref[0, 0, :, sl] = l_pair


def _prompt_attention(q4, k4, v4, add, g):
    b, dil, n, _ = q4.shape
    cur = pl.BlockSpec((1, 1, BAND, GROUP_W), lambda bi, r, i: (bi, r, i, 0))
    prev = pl.BlockSpec((1, 1, BAND, GROUP_W), lambda bi, r, i: (bi, r, jnp.maximum(i - 1, 0), 0))
    return pl.pallas_call(
        _attn_kernel,
        out_shape=(jax.ShapeDtypeStruct(q4.shape, BF16), jax.ShapeDtypeStruct(q4.shape, F32)),
        grid=(b, dil, n // BAND),
        in_specs=[cur, prev, cur, prev, cur, _resident(add.shape)],
        out_specs=(cur, cur),
        compiler_params=_cparams(("parallel", "parallel", "arbitrary")),
        name=f"attn_g{g}",
    )(q4, k4, k4, v4, v4, add)


def _sample_attn_kernel(q_ref, kn_ref, vn_ref, kv_ref, addc_ref, addn_ref, o_ref, lse_ref):
    head_of_lane = lax.broadcasted_iota(jnp.int32, (HEADS, GROUP_W), 1) // HEAD_DIM
    mine = head_of_lane == lax.broadcasted_iota(jnp.int32, (HEADS, GROUP_W), 0)
    qb = q_ref[0]
    qm = jnp.where(mine, jnp.broadcast_to(qb.astype(F32), (HEADS, GROUP_W)), 0.0).astype(BF16)
    kc = kv_ref[0, :, :GROUP_W].astype(BF16)
    vc = kv_ref[0, :, GROUP_W:].astype(BF16)
    kn = kn_ref[0].astype(BF16).astype(F32)
    vn = vn_ref[0].astype(BF16).astype(F32)
    s_c = _nt_dot(qm, kc) + addc_ref[...]
    s_n = jnp.sum(qm.astype(F32) * kn, axis=-1, keepdims=True) + addn_ref[...]
    m = jnp.maximum(jnp.max(s_c, axis=-1, keepdims=True), s_n)
    p_c = jnp.exp(s_c - m)
    p_n = jnp.exp(s_n - m)
    den = jnp.sum(p_c, axis=-1, keepdims=True) + p_n
    o_all = (jnp.dot(p_c.astype(BF16), vc, preferred_element_type=F32)
             + p_n.astype(BF16).astype(F32) * vn) / den
    lse = m + jnp.log(den)
    o_ref[0] = jnp.sum(jnp.where(mine, o_all, 0.0), axis=0, keepdims=True)
    lse_ref[0] = jnp.sum(jnp.where(mine, jnp.broadcast_to(lse, (HEADS, GROUP_W)), 0.0), axis=0, keepdims=True)


def _sample_attention(q_s, k_s, v_s, cache_view, addc, addn, g):
    db = q_s.shape[0]
    tok = pl.BlockSpec((1, 1, GROUP_W), lambda bi: (bi, 0, g))
    out = pl.BlockSpec((1, 1, GROUP_W), lambda bi: (bi, 0, 0))
    return pl.pallas_call(
        _sample_attn_kernel,
        out_shape=(jax.ShapeDtypeStruct((db, 1, GROUP_W), F32), jax.ShapeDtypeStruct((db, 1, GROUP_W), F32)),
        grid=(db,),
        in_specs=[tok, tok, tok, pl.BlockSpec((1, BAND, 2 * GROUP_W), lambda bi: (bi, 0, 0)),
                  _resident(addc.shape), _resident(addn.shape)],
        out_specs=(out, out),
        compiler_params=_cparams(("parallel",)),
        name=f"sample_attn_g{g}",
    )(q_s, k_s, v_s, cache_view, addc, addn)


def _load_token_major(ref, scr_ref, dil):
    if dil == 1:
        return ref[0, 0].astype(F32)
    n = ref.shape[2]
    for r in range(dil):
        blk = ref[0, r].astype(F32)
        for s in range(SLABS):
            scr_ref[s, pl.ds(r, n, stride=dil), :] = blk[:, s * LANES:(s + 1) * LANES]
    return jnp.concatenate([scr_ref[s] for s in range(SLABS)], axis=1)


def _merge_tail(x, a_out, b_out, ga, gb, wua_ref, wub_ref, wo_ref, g2_ref, wq_ref):
    up_a = jnp.dot(a_out.astype(BF16), wua_ref[...], preferred_element_type=F32)
    up_b = jnp.dot(b_out.astype(BF16), wub_ref[...], preferred_element_type=F32)
    mrg = jax.nn.sigmoid(ga) * up_a + jax.nn.sigmoid(gb) * up_b
    x1 = x + jnp.dot(mrg.astype(BF16), wo_ref[...], preferred_element_type=F32)
    h2 = _rms(x1, g2_ref[...]).astype(BF16)
    return x1, h2, jnp.dot(h2, wq_ref[...], preferred_element_type=F32)


def _combine_groups(os, ls):
    m = jnp.maximum(jnp.maximum(ls[0], ls[1]), ls[2])
    es = [jnp.exp(l - m) for l in ls]
    return (es[0] * os[0] + es[1] * os[1] + es[2] * os[2]) / (es[0] + es[1] + es[2])


def _merge_prompt_kernel(x_ref, o0_ref, o1_ref, o2_ref, l0_ref, l1_ref, l2_ref, ub_ref, vb_ref, ga_ref, gb_ref,
                         ws_ref, bs_ref, wua_ref, wub_ref, wo_ref, g2_ref, wq_ref,
                         x1_ref, h2_ref, pq_ref, scr_ref):
    tm = x_ref.shape[1]
    os = [_load_token_major(r, scr_ref, DILATIONS[g]) for g, r in enumerate((o0_ref, o1_ref, o2_ref))]
    ls = [_load_token_major(r, scr_ref, DILATIONS[g]) for g, r in enumerate((l0_ref, l1_ref, l2_ref))]
    a_out = _combine_groups(os, ls)

    gw = D_MODEL // B_GROUPS
    tri = lax.broadcasted_iota(jnp.int32, (CHUNK, CHUNK), 0) >= lax.broadcasted_iota(jnp.int32, (CHUNK, CHUNK), 1)
    parts = []
    for c in range(tm // CHUNK):
        rows = slice(c * CHUNK, (c + 1) * CHUNK)
        cols = []
        for g in range(B_GROUPS):
            w = jnp.where(tri, ws_ref[g], 0.0).astype(BF16)
            cols.append(jnp.dot(w, vb_ref[0, rows, g * gw:(g + 1) * gw], preferred_element_type=F32))
        mixed = jnp.concatenate(cols, axis=1) + bs_ref[...]
        parts.append(ub_ref[0, rows, :].astype(F32) * mixed)
    b_out = jnp.concatenate(parts, axis=0)

    x1, h2, pq = _merge_tail(x_ref[0], a_out, b_out, ga_ref[0], gb_ref[0], wua_ref, wub_ref, wo_ref, g2_ref, wq_ref)
    x1_ref[0] = x1
    h2_ref[0] = h2
    pq_ref[0] = pq.astype(pq_ref.dtype)


def _merge_sample_kernel(x_ref, o0_ref, o1_ref, o2_ref, l0_ref, l1_ref, l2_ref, ub_ref, vb_ref, ga_ref, gb_ref,
                         ws_ref, bs_ref, wua_ref, wub_ref, wo_ref, g2_ref, wq_ref, x1_ref, h2_ref, pq_ref):
    a_out = _combine_groups([o0_ref[...], o1_ref[...], o2_ref[...]], [l0_ref[...], l1_ref[...], l2_ref[...]])
    b_out = ub_ref[...].astype(F32) * (vb_ref[...] * ws_ref[...] + bs_ref[...])
    x1, h2, pq = _merge_tail(x_ref[...], a_out, b_out, ga_ref[...], gb_ref[...], wua_ref, wub_ref, wo_ref, g2_ref, wq_ref)
    x1_ref[...] = x1
    h2_ref[...] = h2
    pq_ref[...] = pq.astype(pq_ref.dtype)


def _merge_prompt(x, o, lse, ub, vb, ga, gb, ws, bs, wua, wub, wo, g2, wq, tm):
    b, s, _ = x.shape
    row = lambda w: pl.BlockSpec((1, tm, w), lambda bi, ti: (bi, ti, 0))
    dilated = [pl.BlockSpec((1, d, tm // d, GROUP_W), lambda bi, ti: (bi, 0, ti, 0)) for d in DILATIONS]
    qw = wq.shape[1]
    weights = (ws, bs, wua, wub, wo, g2, wq)
    return pl.pallas_call(
        _merge_prompt_kernel,
        out_shape=(jax.ShapeDtypeStruct((b, s, D_MODEL), F32), jax.ShapeDtypeStruct((b, s, D_MODEL), BF16),
                   jax.ShapeDtypeStruct((b, s, qw), BF16)),
        grid=(b, s // tm),
        in_specs=[row(D_MODEL)] + dilated + dilated + [row(D_MODEL)] * 4 + [_resident(w.shape) for w in weights],
        out_specs=(row(D_MODEL), row(D_MODEL), row(qw)),
        scratch_shapes=[pltpu.VMEM((SLABS, tm, LANES), F32)],
        compiler_params=_cparams(("parallel", "parallel"), V7X_VMEM_LIMIT),
        name="merge",
    )(x, *o, *lse, ub, vb, ga, gb, *weights)


def _merge_sample(x2d, o, lse, ub, vb, ga, gb, ws0, bs0, wua, wub, wo, g2, wq):
    t = x2d.shape[0]
    row = lambda w: pl.BlockSpec((t, w), lambda i: (0, 0))
    qw = wq.shape[1]
    weights = (ws0, bs0, wua, wub, wo, g2, wq)
    return pl.pallas_call(
        _merge_sample_kernel,
        out_shape=(jax.ShapeDtypeStruct((t, D_MODEL), F32), jax.ShapeDtypeStruct((t, D_MODEL), BF16),
                   jax.ShapeDtypeStruct((t, qw), BF16)),
        grid=(1,),
        in_specs=[row(D_MODEL)] + [row(GROUP_W)] * 6 + [row(D_MODEL)] * 4 + [_resident(w.shape) for w in weights],
        out_specs=(row(D_MODEL), row(D_MODEL), row(qw)),
        compiler_params=_cparams(("arbitrary",), V7X_VMEM_LIMIT),
        name="merge_sample",
    )(x2d, *o, *lse, ub, vb, ga, gb, *weights)


def _top16_rows(s, idx_col):
    vals, idxs = [], []
    big = jnp.float32(1e9)
    for _ in range(PEER_TOPK):
        m = jnp.max(s, axis=0, keepdims=True)
        sel = jnp.min(jnp.where(s == m, idx_col, big), axis=0, keepdims=True)
        s = jnp.where(idx_col == sel, -jnp.inf, s)
        vals.append(m)
        idxs.append(sel)
    return vals, idxs


def _route_kernel(pq_ref, sk1_ref, sk2_ref, i1_ref, i2_ref, gate_ref):
    key_col = lax.broadcasted_iota(jnp.int32, (N_KEYS, LANES), 0).astype(F32)
    row8 = lax.broadcasted_iota(jnp.int32, (8, LANES), 0).astype(F32)
    row16 = lax.broadcasted_iota(jnp.int32, (PEER_TOPK, LANES), 0).astype(F32)
    for c in range(pq_ref.shape[0] // LANES):
        rows = slice(c * LANES, (c + 1) * LANES)
        qh = pq_ref[rows, :]
        s1 = _nt_dot(sk1_ref[...], qh[:, :PEER_HALF])
        s2 = _nt_dot(sk2_ref[...], qh[:, PEER_HALF:])
        v1, i1 = _top16_rows(s1, key_col)
        v2, i2 = _top16_rows(s2, key_col)
        v2a = jnp.concatenate(v2, axis=0)
        i2a = jnp.concatenate(i2, axis=0)
        cand = [v1[0] + v2a]
        flat = [row16]
        code = [i1[0] * N_KEYS + i2a]
        for j1 in range(1, 8):
            cand.append(v1[j1] + v2a[:8])
            flat.append(row8 + float(j1 * PEER_TOPK))
            code.append(i1[j1] * N_KEYS + i2a[:8])
        cand.append(jnp.concatenate(v1[8:], axis=0) + v2[0])
        flat.append((row8 + 8.0) * float(PEER_TOPK))
        code.append(jnp.concatenate(i1[8:], axis=0) * N_KEYS + i2[0])
        cand = jnp.concatenate(cand, axis=0)
        flat = jnp.concatenate(flat, axis=0)
        code = jnp.concatenate(code, axis=0)
        top_s, top_e = [], []
        for _ in range(PEER_TOPK):
            m = jnp.max(cand, axis=0, keepdims=True)
            sel = jnp.min(jnp.where(cand == m, flat, 1e9), axis=0, keepdims=True)
            hit = flat == sel
            top_e.append(jnp.max(jnp.where(hit, code, -1.0), axis=0, keepdims=True))
            cand = jnp.where(hit, -jnp.inf, cand)
            top_s.append(m)
        ts = jnp.concatenate(top_s, axis=0)
        te = jnp.concatenate(top_e, axis=0)
        e = jnp.exp(ts - ts[0:1])
        gate = e / jnp.sum(e, axis=0, keepdims=True)
        a = jnp.floor(te * (1.0 / N_KEYS))
        i1_ref[:, rows] = a
        i2_ref[:, rows] = te - a * N_KEYS
        gate_ref[:, rows] = gate


def _route(pq, sk1_bf, sk2_bf, tm):
    t = pq.shape[0]
    out = pl.BlockSpec((PEER_TOPK, tm), lambda i, h: (h, i))
    shp = jax.ShapeDtypeStruct((PEER_HEADS * PEER_TOPK, t), F32)
    return pl.pallas_call(
        _route_kernel,
        out_shape=(shp, shp, shp),
        grid=(t // tm, PEER_HEADS),
        in_specs=[pl.BlockSpec((tm, 2 * PEER_HALF), lambda i, h: (i, h)),
                  _resident(sk1_bf.shape), _resident(sk2_bf.shape)],
        out_specs=(out, out, out),
        compiler_params=_cparams(("parallel", "parallel")),
        name="peer_route",
    )(pq, sk1_bf, sk2_bf)


def _peer_kernel(h2_ref, x1_ref, i1_ref, i2_ref, gt_ref, p_ref, ed_ref, eu_ref,
                 g3_ref, wpg_ref, wpp_ref, y_ref, pk_ref, acc_ref, i1s_ref, i2s_ref, gts_ref):
    tm = h2_ref.shape[0]
    j = pl.program_id(1)
    half = N_KEYS // 2

    @pl.when(j == 0)
    def _build_gates():
        i1s_ref[...] = i1_ref[...].T
        i2s_ref[...] = i2_ref[...].T
        gts_ref[...] = gt_ref[...].T
        acc_ref[...] = jnp.zeros_like(acc_ref)
        sub = lax.broadcasted_iota(jnp.int32, (N_KEYS, N_KEYS), 0)
        b_of_row = sub.astype(F32)
        a_of_row = jnp.where(sub < half, 2 * sub, 2 * (sub - half) + 1).astype(F32)

        def body(t, carry):
            r1 = i1s_ref[pl.ds(t, 1), :]
            r2 = i2s_ref[pl.ds(t, 1), :]
            rg = gts_ref[pl.ds(t, 1), :]
            oat = jnp.where(a_of_row == r1, 1.0, 0.0).astype(BF16)
            gbt = jnp.where(b_of_row == r2, rg, 0.0).astype(BF16)
            g = _nt_dot(oat, gbt)
            lo = pltpu.bitcast(g[:half], jnp.uint32) + jnp.uint32(0x8000)
            hi = pltpu.bitcast(g[half:], jnp.uint32) + jnp.uint32(0x8000)
            packed = (lo >> 16) | (hi & jnp.uint32(0xFFFF0000))
            pk_ref[pl.ds(pl.multiple_of(t * G_PITCH, 8), half), :] = packed
            return carry

        lax.fori_loop(0, tm, body, 0, unroll=8)

    gs = []
    for r in range(PAIRS_PER_STEP):
        w = pk_ref[pl.ds(j * PAIRS_PER_STEP + r, tm, stride=G_PITCH), :]
        gs.append(pltpu.bitcast(w << 16, F32))
        gs.append(pltpu.bitcast(w & jnp.uint32(0xFFFF0000), F32))
    gates = jnp.concatenate(gs, axis=1)
    s = _nt_dot(h2_ref[...], ed_ref[...])
    act = (jax.nn.gelu(s) * gates).astype(BF16)
    acc_ref[...] += jnp.dot(act, eu_ref[...], preferred_element_type=F32)

    @pl.when(j == pl.num_programs(1) - 1)
    def _finish():
        x2 = x1_ref[...] + acc_ref[...]
        h3 = _rms(x2, g3_ref[...]).astype(BF16)
        gate = jax.nn.sigmoid(jnp.dot(h3, wpg_ref[...], preferred_element_type=F32))
        y_ref[...] = x2 + gate * jnp.dot(p_ref[...].astype(BF16), wpp_ref[...], preferred_element_type=F32)


def _peer_ple(h2, x1, i1, i2, gt, p2d, ed, eu, g3, wpg, wpp, tm):
    t = h2.shape[0]
    nb = PAIRS_PER_STEP * 2 * N_KEYS
    steps = ed.shape[0] // nb
    row = lambda w: pl.BlockSpec((tm, w), lambda i, j: (i, 0))
    pick = pl.BlockSpec((PEER_HEADS * PEER_TOPK, tm), lambda i, j: (0, i))
    table = pl.BlockSpec((nb, D_MODEL), lambda i, j: (j, 0))
    return pl.pallas_call(
        _peer_kernel,
        out_shape=jax.ShapeDtypeStruct((t, D_MODEL), F32),
        grid=(t // tm, steps),
        in_specs=[row(D_MODEL), row(D_MODEL), pick, pick, pick, row(PLE_DIM), table, table,
                  _resident(g3.shape), _resident(wpg.shape), _resident(wpp.shape)],
        out_specs=row(D_MODEL),
        scratch_shapes=[pltpu.VMEM((tm * G_PITCH, N_KEYS), jnp.uint32),
                        pltpu.VMEM((tm, D_MODEL), F32),
                        pltpu.VMEM((tm, N_KEYS), F32), pltpu.VMEM((tm, N_KEYS), F32), pltpu.VMEM((tm, N_KEYS), F32)],
        compiler_params=_cparams(("parallel", "arbitrary"), V7X_VMEM_LIMIT),
        name="peer_ple",
    )(h2, x1, i1, i2, gt, p2d, ed, eu, g3, wpg, wpp)


def _t5_bucket(dist):
    max_exact = N_BUCKETS // 2
    df = jnp.maximum(dist, max_exact).astype(F32)
    large = max_exact + (jnp.log(df / max_exact) / math.log(MAX_DISTANCE / max_exact)
                         * (N_BUCKETS - max_exact)).astype(jnp.int32)
    return jnp.where(dist < max_exact, dist, jnp.minimum(large, N_BUCKETS - 1))


def _bias_tables(rel_bias, g):
    dil = DILATIONS[g]
    hi = lax.Precision.HIGHEST
    bucket = _t5_bucket(jnp.arange(BAND + 1) * dil)
    pick = (bucket[:, None] == jnp.arange(N_BUCKETS)[None, :]).astype(F32)
    per_dist = jnp.dot(pick, rel_bias[:, g * HEADS:(g + 1) * HEADS].astype(F32), precision=hi)
    dsub = np.arange(BAND)[:, None] + BAND - np.arange(2 * BAND)[None, :]
    valid = (dsub >= 0) & (dsub <= WINDOWS[g] // dil)
    spread = (jnp.asarray(np.clip(dsub, 0, BAND).reshape(1, -1)) == jnp.arange(BAND + 1)[:, None]).astype(F32)
    band = jnp.dot(per_dist.T, spread, precision=hi).reshape(HEADS, BAND, 2 * BAND)
    add = jnp.where(jnp.asarray(valid)[None], band, NEG_INF)
    addc = per_dist[::-1][:BAND].T
    addn = per_dist[0][:, None]
    return add, addc, addn


def kernel(x_prompt, x_sample, cache_kv_w128, cache_kv_w512, cache_kv_w2048, p_prompt, p_sample, rel_bias, norm1,
           w_in, q_norm, k_norm, b_v_norm, w_spatial, b_spatial, w_up_a, w_up_b, w_out, norm2, w_query,
           sub_keys_1, sub_keys_2, expert_down, expert_up, norm3, w_ple_gate, w_ple_proj):
    bsz, seq, _ = x_prompt.shape
    db, ds, _ = x_sample.shape
    depth = norm1.shape[0]
    assert depth == 1 and ds == 1 and seq % (BAND * DILATIONS[-1]) == 0 and seq >= WINDOWS[-1]
    caches = (cache_kv_w128, cache_kv_w512, cache_kv_w2048)
    for g in range(N_GROUPS):
        assert caches[g].shape[2] == WINDOWS[g]
    t_p = bsz * seq
    t_s = 128
    li = 0

    g1 = norm1[li][None, :]
    w_in_bf = w_in[li].astype(BF16)
    qg = jnp.tile(q_norm[li], (1, HEADS))
    kg = jnp.tile(k_norm[li], (1, HEADS))
    bvg = b_v_norm[li][None, :]
    blk = np.arange(GROUP_W) // HEAD_DIM
    hsum = jnp.asarray(blk[:, None] == blk[None, :], dtype=BF16)
    wua, wub, wo = w_up_a[li].astype(BF16), w_up_b[li].astype(BF16), w_out[li].astype(BF16)
    g2, g3 = norm2[li][None, :], norm3[li][None, :]
    wq = w_query[li].astype(BF16)
    sk1, sk2 = sub_keys_1[li].astype(BF16), sub_keys_2[li].astype(BF16)
    ed = expert_down[li].astype(BF16)
    eu = expert_up[li].astype(BF16)
    wpg, wpp = w_ple_gate[li].astype(BF16), w_ple_proj[li].astype(BF16)
    bs_full = jnp.repeat(b_spatial[li].T, D_MODEL // B_GROUPS, axis=1)
    tables = [_bias_tables(rel_bias, g) for g in range(N_GROUPS)]

    res = _in_projection_prompt(x_prompt, g1, w_in_bf, qg, kg, bvg, hsum, TM_PROJ)
    qs, ks, vs, tails = res[0:3], res[3:6], res[6:9], res[9:12]
    ub, vb, ga, gb = res[12:16]
    outs, lses, kv_prompt = [], [], []
    for g in range(N_GROUPS):
        o, lse = _prompt_attention(qs[g], ks[g], vs[g], tables[g][0], g)
        outs.append(o)
        lses.append(lse)
        kv_prompt.append(tails[g].reshape(1, bsz, tails[g].shape[1], 2, HEADS, HEAD_DIM))
    x1, h2, pq = _merge_prompt(x_prompt, outs, lses, ub, vb, ga, gb, w_spatial[li], bs_full, wua, wub, wo, g2, wq,
                               TM_PROJ)
    x1, h2, pq = x1.reshape(t_p, D_MODEL), h2.reshape(t_p, D_MODEL), pq.reshape(t_p, -1)
    i1, i2, gt = _route(pq, sk1, sk2, TM_PROJ)
    y_prompt = _peer_ple(h2, x1, i1, i2, gt, p_prompt[li].reshape(t_p, PLE_DIM), ed, eu, g3, wpg, wpp, TM_PEER)
    y_prompt = y_prompt.reshape(bsz, seq, D_MODEL)

    xs = jnp.pad(x_sample.reshape(db, D_MODEL), ((0, t_s - db), (0, 0)))
    q, k, v, ub, vb, ga, gb = _in_projection_sample(xs, g1, w_in_bf, qg, kg, bvg, hsum)
    outs, lses, kv_sample = [], [], []
    for g in range(N_GROUPS):
        dil = DILATIONS[g]
        cols = slice(g * GROUP_W, (g + 1) * GROUP_W)
        cache_view = caches[g][li].reshape(db, WINDOWS[g] // dil, dil * 2 * GROUP_W)
        o, lse = _sample_attention(q[:db, None, :], k[:db, None, :], v[:db, None, :], cache_view,
                                   tables[g][1], tables[g][2], g)
        pad = lambda a: jnp.pad(a.reshape(db, GROUP_W), ((0, t_s - db), (0, 0)))
        outs.append(pad(o))
        lses.append(pad(lse))
        kn = k[:db, cols].reshape(db, 1, HEADS, HEAD_DIM)
        vn = v[:db, cols].reshape(db, 1, HEADS, HEAD_DIM)
        kv_sample.append(jnp.stack([kn, vn], axis=2)[None])
    ws0 = jnp.repeat(w_spatial[li][:, 0, 0], D_MODEL // B_GROUPS)[None, :]
    bs0 = bs_full[0:1, :]
    x1, h2, pq = _merge_sample(xs, outs, lses, ub, vb, ga, gb, ws0, bs0, wua, wub, wo, g2, wq)
    i1, i2, gt = _route(pq, sk1, sk2, t_s)
    ps = jnp.pad(p_sample[li].reshape(db, PLE_DIM), ((0, t_s - db), (0, 0)))
    y_sample = _peer_ple(h2, x1, i1, i2, gt, ps, ed, eu, g3, wpg, wpp, t_s)[:db].reshape(db, ds, D_MODEL)
    v_sample_chunk = vb[:db].reshape(1, db, ds, D_MODEL)

    return (y_prompt, y_sample, kv_prompt[0], kv_prompt[1], kv_prompt[2],
            kv_sample[0], kv_sample[1], kv_sample[2], v_sample_chunk)
```

```python
import functools
import math

import numpy as np
import jax
import jax.numpy as jnp
from jax import lax
from jax.experimental import pallas as pl
from jax.experimental.pallas import tpu as pltpu

F32 = jnp.float32
BF16 = jnp.bfloat16

D_MODEL = 1024
N_GROUPS = 3
HEADS = 8
HEAD_DIM = 64
GROUP_W = HEADS * HEAD_DIM
QKV_W = N_GROUPS * GROUP_W
WINDOWS = (128, 512, 2048)
DILATIONS = (1, 4, 16)
BAND = 128
N_BUCKETS = 32
MAX_DISTANCE = 2048
CHUNK = 128
B_GROUPS = 4
N_KEYS = 128
PEER_HEADS = 8
PEER_TOPK = 16
PEER_HALF = 128
PLE_DIM = 256
EPS = 1e-6
NEG_INF = -1e30
IN_OFFS = (0, QKV_W, 2 * QKV_W, 3 * QKV_W, 3 * QKV_W + D_MODEL, 3 * QKV_W + 2 * D_MODEL,
           3 * QKV_W + 3 * D_MODEL, 3 * QKV_W + 4 * D_MODEL)

LANES = 128
SLABS = GROUP_W // LANES
V7X_VMEM_LIMIT = 56 * 1024 * 1024
V7X_MXU_DIM = 256
G_PITCH = 72
TM_PROJ = 256
TM_PEER = 512
PAIRS_PER_STEP = 4


def _cparams(sem, vmem=None):
    return pltpu.CompilerParams(dimension_semantics=sem, vmem_limit_bytes=vmem)


def _resident(shape):
    nd = len(shape)
    return pl.BlockSpec(shape, lambda *_: (0,) * nd)


def _nt_dot(a, b):
    return lax.dot_general(a, b, (((1,), (1,)), ((), ())), preferred_element_type=F32)


def _rms(x, gain):
    return x * lax.rsqrt(jnp.mean(x * x, axis=-1, keepdims=True) + EPS) * gain


def _head_norm(y, hsum_ref, gain_row):
    y2 = (y * y).astype(BF16)
    w = hsum_ref.shape[0]
    ss = jnp.concatenate([jnp.dot(y2[:, c:c + w], hsum_ref[...], preferred_element_type=F32)
                          for c in range(0, GROUP_W, w)], axis=1)
    return y * lax.rsqrt(ss * (1.0 / HEAD_DIM) + EPS) * gain_row


def _store_residue_major(out_ref, y, scr_ref, dil):
    if dil == 1:
        out_ref[0, 0] = y.astype(out_ref.dtype)
        return
    n = y.shape[0] // dil
    for s in range(SLABS):
        scr_ref[s] = y[:, s * LANES:(s + 1) * LANES]
    for r in range(dil):
        for s in range(SLABS):
            out_ref[0, r, :, s * LANES:(s + 1) * LANES] = scr_ref[s, pl.ds(r, n, stride=dil), :].astype(out_ref.dtype)


def _gmlp_and_gates(proj, bvg_ref, ub_ref, vb_ref, ga_ref, gb_ref, st):
    half = D_MODEL // 2
    for c in range(2):
        st(ub_ref, c * half, jax.nn.gelu(proj(IN_OFFS[3] + c * half, half)))
    vb0 = jax.nn.gelu(proj(IN_OFFS[4], half))
    vb1 = jax.nn.gelu(proj(IN_OFFS[4] + half, half))
    ms = (jnp.sum(vb0 * vb0, axis=-1, keepdims=True) + jnp.sum(vb1 * vb1, axis=-1, keepdims=True)) * (1.0 / D_MODEL)
    r = lax.rsqrt(ms + EPS)
    st(vb_ref, 0, vb0 * r * bvg_ref[:, :half])
    st(vb_ref, half, vb1 * r * bvg_ref[:, half:])
    for c in range(2):
        st(ga_ref, c * half, proj(IN_OFFS[5] + c * half, half))
        st(gb_ref, c * half, proj(IN_OFFS[6] + c * half, half))


def _inproj_prompt_kernel(x_ref, g1_ref, w_ref, qg_ref, kg_ref, bvg_ref, hsum_ref,
                          q0_ref, q1_ref, q2_ref, k0_ref, k1_ref, k2_ref, v0_ref, v1_ref, v2_ref,
                          t0_ref, t1_ref, t2_ref, ub_ref, vb_ref, ga_ref, gb_ref, scr_ref, *, tail_first):
    ti = pl.program_id(1)
    tm = x_ref.shape[1]
    hb = _rms(x_ref[0], g1_ref[...]).astype(BF16)

    def proj(c0, width):
        return jnp.dot(hb, w_ref[:, c0:c0 + width], preferred_element_type=F32)

    q_refs, k_refs, v_refs, t_refs = (q0_ref, q1_ref, q2_ref), (k0_ref, k1_ref, k2_ref), (v0_ref, v1_ref, v2_ref), \
        (t0_ref, t1_ref, t2_ref)
    for g in range(N_GROUPS):
        lo = g * GROUP_W
        dil = DILATIONS[g]
        qn = _head_norm(proj(IN_OFFS[0] + lo, GROUP_W), hsum_ref, qg_ref[g:g + 1, :]) * (HEAD_DIM ** -0.5)
        _store_residue_major(q_refs[g], qn, scr_ref, dil)
        kn = _head_norm(proj(IN_OFFS[1] + lo, GROUP_W), hsum_ref, kg_ref[g:g + 1, :])
        _store_residue_major(k_refs[g], kn, scr_ref, dil)
        vv = proj(IN_OFFS[2] + lo, GROUP_W)
        _store_residue_major(v_refs[g], vv, scr_ref, dil)
        rows = t_refs[g].shape[1]

        @pl.when(ti >= tail_first[g])
        def _():
            t_refs[g][0, :, :GROUP_W] = kn[tm - rows:, :]
            t_refs[g][0, :, GROUP_W:] = vv[tm - rows:, :]

    def st(ref, c0, val):
        ref[0, :, c0:c0 + val.shape[1]] = val.astype(ref.dtype)

    _gmlp_and_gates(proj, bvg_ref, ub_ref, vb_ref, ga_ref, gb_ref, st)


def _inproj_sample_kernel(x_ref, g1_ref, w_ref, qg_ref, kg_ref, bvg_ref, hsum_ref,
                          q_ref, k_ref, v_ref, ub_ref, vb_ref, ga_ref, gb_ref):
    hb = _rms(x_ref[...], g1_ref[...]).astype(BF16)

    def proj(c0, width):
        return jnp.dot(hb, w_ref[:, c0:c0 + width], preferred_element_type=F32)

    for g in range(N_GROUPS):
        lo = g * GROUP_W
        qn = _head_norm(proj(IN_OFFS[0] + lo, GROUP_W), hsum_ref, qg_ref[g:g + 1, :]) * (HEAD_DIM ** -0.5)
        q_ref[:, lo:lo + GROUP_W] = qn.astype(q_ref.dtype)
        k_ref[:, lo:lo + GROUP_W] = _head_norm(proj(IN_OFFS[1] + lo, GROUP_W), hsum_ref, kg_ref[g:g + 1, :])
        v_ref[:, lo:lo + GROUP_W] = proj(IN_OFFS[2] + lo, GROUP_W)

    def st(ref, c0, val):
        ref[:, c0:c0 + val.shape[1]] = val.astype(ref.dtype)

    _gmlp_and_gates(proj, bvg_ref, ub_ref, vb_ref, ga_ref, gb_ref, st)


def _proj_weight_specs(w_in_bf):
    return [_resident((1, D_MODEL)),
            pl.BlockSpec(w_in_bf.shape, lambda *_: (0, 0), pipeline_mode=pl.Buffered(1)),
            _resident((N_GROUPS, GROUP_W)), _resident((N_GROUPS, GROUP_W)), _resident((1, D_MODEL)),
            _resident((V7X_MXU_DIM, V7X_MXU_DIM))]


def _in_projection_prompt(x, g1, w_in_bf, qg, kg, bvg, hsum, tm):
    b, s, _ = x.shape
    nt = s // tm
    row = pl.BlockSpec((1, tm, D_MODEL), lambda bi, ti: (bi, ti, 0))
    qkv_shapes, qkv_specs, tail_shapes, tail_specs, tail_first = [], [], [], [], []
    for g in range(N_GROUPS):
        dil = DILATIONS[g]
        qkv_shapes.append(jax.ShapeDtypeStruct((b, dil, s // dil, GROUP_W), BF16))
        qkv_specs.append(pl.BlockSpec((1, dil, tm // dil, GROUP_W), lambda bi, ti: (bi, 0, ti, 0)))
        ln = min(WINDOWS[g], s)
        rows = min(ln, tm)
        first = (s - ln) // tm if ln >= tm else nt - 1
        tail_first.append(first)
        tail_shapes.append(jax.ShapeDtypeStruct((b, ln, 2 * GROUP_W), F32))
        tail_specs.append(pl.BlockSpec((1, rows, 2 * GROUP_W),
                                       functools.partial(lambda bi, ti, f: (bi, jnp.maximum(ti - f, 0), 0), f=first)))
    act = jax.ShapeDtypeStruct((b, s, D_MODEL), BF16)
    gate = jax.ShapeDtypeStruct((b, s, D_MODEL), F32)
    return pl.pallas_call(
        functools.partial(_inproj_prompt_kernel, tail_first=tuple(tail_first)),
        out_shape=tuple(qkv_shapes * 3 + tail_shapes + [act, act, gate, gate]),
        grid=(b, nt),
        in_specs=[row] + _proj_weight_specs(w_in_bf),
        out_specs=tuple(qkv_specs * 3 + tail_specs + [row] * 4),
        scratch_shapes=[pltpu.VMEM((SLABS, tm, LANES), F32)],
        compiler_params=_cparams(("arbitrary", "arbitrary"), V7X_VMEM_LIMIT),
        name="in_projection",
    )(x, g1, w_in_bf, qg, kg, bvg, hsum)


def _in_projection_sample(x2d, g1, w_in_bf, qg, kg, bvg, hsum):
    t = x2d.shape[0]
    row = lambda w: pl.BlockSpec((t, w), lambda i: (0, 0))
    outs = (jax.ShapeDtypeStruct((t, QKV_W), BF16), jax.ShapeDtypeStruct((t, QKV_W), F32),
            jax.ShapeDtypeStruct((t, QKV_W), F32), jax.ShapeDtypeStruct((t, D_MODEL), BF16),
            jax.ShapeDtypeStruct((t, D_MODEL), F32), jax.ShapeDtypeStruct((t, D_MODEL), F32),
            jax.ShapeDtypeStruct((t, D_MODEL), F32))
    return pl.pallas_call(
        _inproj_sample_kernel,
        out_shape=outs,
        grid=(1,),
        in_specs=[row(D_MODEL)] + _proj_weight_specs(w_in_bf),
        out_specs=(row(QKV_W), row(QKV_W), row(QKV_W), row(D_MODEL), row(D_MODEL), row(D_MODEL), row(D_MODEL)),
        compiler_params=_cparams(("arbitrary",), V7X_VMEM_LIMIT),
        name="in_projection_sample",
    )(x2d, g1, w_in_bf, qg, kg, bvg, hsum)


def _attn_kernel(q_ref, kp_ref, kc_ref, vp_ref, vc_ref, add_ref, o_ref, lse_ref):
    i = pl.program_id(2)
    q = q_ref[0, 0]
    kk = jnp.concatenate([kp_ref[0, 0], kc_ref[0, 0]], axis=0)
    vv = jnp.concatenate([vp_ref[0, 0], vc_ref[0, 0]], axis=0)
    lane = lax.broadcasted_iota(jnp.int32, (BAND, LANES), 1)
    kcol = lax.broadcasted_iota(jnp.int32, (BAND, 2 * BAND), 1)
    no_prev = kcol < jnp.where(i == 0, BAND, 0)
    for pair in range(HEADS // 2):
        sl = slice(pair * LANES, (pair + 1) * LANES)
        qp, kp, vp = q[:, sl].astype(F32), kk[:, sl], vv[:, sl]
        o_pair = jnp.zeros((BAND, LANES), F32)
        l_pair = jnp.zeros((BAND, LANES), F32)
        for hh in range(2):
            mine = (lane >= HEAD_DIM) if hh else (lane < HEAD_DIM)
            s = _nt_dot(jnp.where(mine, qp, 0.0).astype(BF16), kp) + add_ref[pair * 2 + hh]
            s = jnp.where(no_prev, NEG_INF, s)
            m = jnp.max(s, axis=-1, keepdims=True)
            p = jnp.exp(s - m)
            den = jnp.sum(p, axis=-1, keepdims=True)
            o = jnp.dot(p.astype(BF16), vp, preferred_element_type=F32) / den
            o_pair = jnp.where(mine, o, o_pair)
            l_pair = jnp.where(mine, m + jnp.log(den), l_pair)
        o_ref[0, 0, :, sl] = o_pair.astype(o_ref.dtype)
        lse_ref[0, 0, :, sl] = l_pair


def _prompt_attention(q4, k4, v4, add, g):
    b, dil, n, _ = q4.shape
    cur = pl.BlockSpec((1, 1, BAND, GROUP_W), lambda bi, r, i: (bi, r, i, 0))
    prev = pl.BlockSpec((1, 1, BAND, GROUP_W), lambda bi, r, i: (bi, r, jnp.maximum(i - 1, 0), 0))
    return pl.pallas_call(
        _attn_kernel,
        out_shape=(jax.ShapeDtypeStruct(q4.shape, BF16), jax.ShapeDtypeStruct(q4.shape, F32)),
        grid=(b, dil, n // BAND),
        in_specs=[cur, prev, cur, prev, cur, _resident(add.shape)],
        out_specs=(cur, cur),
        compiler_params=_cparams(("parallel", "parallel", "arbitrary")),
        name=f"attn_g{g}",
    )(q4, k4, k4, v4, v4, add)


def _sample_attn_kernel(q_ref, kn_ref, vn_ref, kv_ref, addc_ref, addn_ref, o_ref, lse_ref):
    head_of_lane = lax.broadcasted_iota(jnp.int32, (HEADS, GROUP_W), 1) // HEAD_DIM
    mine = head_of_lane == lax.broadcasted_iota(jnp.int32, (HEADS, GROUP_W), 0)
    qb = q_ref[0]
    qm = jnp.where(mine, jnp.broadcast_to(qb.astype(F32), (HEADS, GROUP_W)), 0.0).astype(BF16)
    kc = kv_ref[0, :, :GROUP_W].astype(BF16)
    vc = kv_ref[0, :, GROUP_W:].astype(BF16)
    kn = kn_ref[0].astype(BF16).astype(F32)
    vn = vn_ref[0].astype(BF16).astype(F32)
    s_c = _nt_dot(qm, kc) + addc_ref[...]
    s_n = jnp.sum(qm.astype(F32) * kn, axis=-1, keepdims=True) + addn_ref[...]
    m = jnp.maximum(jnp.max(s_c, axis=-1, keepdims=True), s_n)
    p_c = jnp.exp(s_c - m)
    p_n = jnp.exp(s_n - m)
    den = jnp.sum(p_c, axis=-1, keepdims=True) + p_n
    o_all = (jnp.dot(p_c.astype(BF16), vc, preferred_element_type=F32)
             + p_n.astype(BF16).astype(F32) * vn) / den
    lse = m + jnp.log(den)
    o_ref[0] = jnp.sum(jnp.where(mine, o_all, 0.0), axis=0, keepdims=True)
    lse_ref[0] = jnp.sum(jnp.where(mine, jnp.broadcast_to(lse, (HEADS, GROUP_W)), 0.0), axis=0, keepdims=True)


def _sample_attention(q_s, k_s, v_s, cache_view, addc, addn, g):
    db = q_s.shape[0]
    tok = pl.BlockSpec((1, 1, GROUP_W), lambda bi: (bi, 0, g))
    out = pl.BlockSpec((1, 1, GROUP_W), lambda bi: (bi, 0, 0))
    return pl.pallas_call(
        _sample_attn_kernel,
        out_shape=(jax.ShapeDtypeStruct((db, 1, GROUP_W), F32), jax.ShapeDtypeStruct((db, 1, GROUP_W), F32)),
        grid=(db,),
        in_specs=[tok, tok, tok, pl.BlockSpec((1, BAND, 2 * GROUP_W), lambda bi: (bi, 0, 0)),
                  _resident(addc.shape), _resident(addn.shape)],
        out_specs=(out, out),
        compiler_params=_cparams(("parallel",)),
        name=f"sample_attn_g{g}",
    )(q_s, k_s, v_s, cache_view, addc, addn)


def _load_token_major(ref, scr_ref, dil):
    if dil == 1:
        return ref[0, 0].astype(F32)
    n = ref.shape[2]
    for r in range(dil):
        blk = ref[0, r].astype(F32)
        for s in range(SLABS):
            scr_ref[s, pl.ds(r, n, stride=dil), :] = blk[:, s * LANES:(s + 1) * LANES]
    return jnp.concatenate([scr_ref[s] for s in range(SLABS)], axis=1)


def _merge_tail(x, a_out, b_out, ga, gb, wua_ref, wub_ref, wo_ref, g2_ref, wq_ref):
    up_a = jnp.dot(a_out.astype(BF16), wua_ref[...], preferred_element_type=F32)
    up_b = jnp.dot(b_out.astype(BF16), wub_ref[...], preferred_element_type=F32)
    mrg = jax.nn.sigmoid(ga) * up_a + jax.nn.sigmoid(gb) * up_b
    x1 = x + jnp.dot(mrg.astype(BF16), wo_ref[...], preferred_element_type=F32)
    h2 = _rms(x1, g2_ref[...]).astype(BF16)
    return x1, h2, jnp.dot(h2, wq_ref[...], preferred_element_type=F32)


def _combine_groups(os, ls):
    m = jnp.maximum(jnp.maximum(ls[0], ls[1]), ls[2])
    es = [jnp.exp(l - m) for l in ls]
    return (es[0] * os[0] + es[1] * os[1] + es[2] * os[2]) / (es[0] + es[1] + es[2])


def _merge_prompt_kernel(x_ref, o0_ref, o1_ref, o2_ref, l0_ref, l1_ref, l2_ref, ub_ref, vb_ref, ga_ref, gb_ref,
                         ws_ref, bs_ref, wua_ref, wub_ref, wo_ref, g2_ref, wq_ref,
                         x1_ref, h2_ref, pq_ref, scr_ref):
    tm = x_ref.shape[1]
    os = [_load_token_major(r, scr_ref, DILATIONS[g]) for g, r in enumerate((o0_ref, o1_ref, o2_ref))]
    ls = [_load_token_major(r, scr_ref, DILATIONS[g]) for g, r in enumerate((l0_ref, l1_ref, l2_ref))]
    a_out = _combine_groups(os, ls)

    gw = D_MODEL // B_GROUPS
    tri = lax.broadcasted_iota(jnp.int32, (CHUNK, CHUNK), 0) >= lax.broadcasted_iota(jnp.int32, (CHUNK, CHUNK), 1)
    parts = []
    for c in range(tm // CHUNK):
        rows = slice(c * CHUNK, (c + 1) * CHUNK)
        cols = []
        for g in range(B_GROUPS):
            w = jnp.where(tri, ws_ref[g], 0.0).astype(BF16)
            cols.append(jnp.dot(w, vb_ref[0, rows, g * gw:(g + 1) * gw], preferred_element_type=F32))
        mixed = jnp.concatenate(cols, axis=1) + bs_ref[...]
        parts.append(ub_ref[0, rows, :].astype(F32) * mixed)
    b_out = jnp.concatenate(parts, axis=0)

    x1, h2, pq = _merge_tail(x_ref[0], a_out, b_out, ga_ref[0], gb_ref[0], wua_ref, wub_ref, wo_ref, g2_ref, wq_ref)
    x1_ref[0] = x1
    h2_ref[0] = h2
    pq_ref[0] = pq.astype(pq_ref.dtype)


def _merge_sample_kernel(x_ref, o0_ref, o1_ref, o2_ref, l0_ref, l1_ref, l2_ref, ub_ref, vb_ref, ga_ref, gb_ref,
                         ws_ref, bs_ref, wua_ref, wub_ref, wo_ref, g2_ref, wq_ref, x1_ref, h2_ref, pq_ref):
    a_out = _combine_groups([o0_ref[...], o1_ref[...], o2_ref[...]], [l0_ref[...], l1_ref[...], l2_ref[...]])
    b_out = ub_ref[...].astype(F32) * (vb_ref[...] * ws_ref[...] + bs_ref[...])
    x1, h2, pq = _merge_tail(x_ref[...], a_out, b_out, ga_ref[...], gb_ref[...], wua_ref, wub_ref, wo_ref, g2_ref, wq_ref)
    x1_ref[...] = x1
    h2_ref[...] = h2
    pq_ref[...] = pq.astype(pq_ref.dtype)


def _merge_prompt(x, o, lse, ub, vb, ga, gb, ws, bs, wua, wub, wo, g2, wq, tm):
    b, s, _ = x.shape
    row = lambda w: pl.BlockSpec((1, tm, w), lambda bi, ti: (bi, ti, 0))
    dilated = [pl.BlockSpec((1, d, tm // d, GROUP_W), lambda bi, ti: (bi, 0, ti, 0)) for d in DILATIONS]
    qw = wq.shape[1]
    weights = (ws, bs, wua, wub, wo, g2, wq)
    return pl.pallas_call(
        _merge_prompt_kernel,
        out_shape=(jax.ShapeDtypeStruct((b, s, D_MODEL), F32), jax.ShapeDtypeStruct((b, s, D_MODEL), BF16),
                   jax.ShapeDtypeStruct((b, s, qw), BF16)),
        grid=(b, s // tm),
        in_specs=[row(D_MODEL)] + dilated + dilated + [row(D_MODEL)] * 4 + [_resident(w.shape) for w in weights],
        out_specs=(row(D_MODEL), row(D_MODEL), row(qw)),
        scratch_shapes=[pltpu.VMEM((SLABS, tm, LANES), F32)],
        compiler_params=_cparams(("parallel", "parallel"), V7X_VMEM_LIMIT),
        name="merge",
    )(x, *o, *lse, ub, vb, ga, gb, *weights)


def _merge_sample(x2d, o, lse, ub, vb, ga, gb, ws0, bs0, wua, wub, wo, g2, wq):
    t = x2d.shape[0]
    row = lambda w: pl.BlockSpec((t, w), lambda i: (0, 0))
    qw = wq.shape[1]
    weights = (ws0, bs0, wua, wub, wo, g2, wq)
    return pl.pallas_call(
        _merge_sample_kernel,
        out_shape=(jax.ShapeDtypeStruct((t, D_MODEL), F32), jax.ShapeDtypeStruct((t, D_MODEL), BF16),
                   jax.ShapeDtypeStruct((t, qw), BF16)),
        grid=(1,),
        in_specs=[row(D_MODEL)] + [row(GROUP_W)] * 6 + [row(D_MODEL)] * 4 + [_resident(w.shape) for w in weights],
        out_specs=(row(D_MODEL), row(D_MODEL), row(qw)),
        compiler_params=_cparams(("arbitrary",), V7X_VMEM_LIMIT),
        name="merge_sample",
    )(x2d, *o, *lse, ub, vb, ga, gb, *weights)


def _top16_rows(s, idx_col):
    vals, idxs = [], []
    big = jnp.float32(1e9)
    for _ in range(PEER_TOPK):
        m = jnp.max(s, axis=0, keepdims=True)
        sel = jnp.min(jnp.where(s == m, idx_col, big), axis=0, keepdims=True)
        s = jnp.where(idx_col == sel, -jnp.inf, s)
        vals.append(m)
        idxs.append(sel)
    return vals, idxs


def _route_kernel(pq_ref, sk1_ref, sk2_ref, i1_ref, i2_ref, gate_ref):
    key_col = lax.broadcasted_iota(jnp.int32, (N_KEYS, LANES), 0).astype(F32)
    row8 = lax.broadcasted_iota(jnp.int32, (8, LANES), 0).astype(F32)
    row16 = lax.broadcasted_iota(jnp.int32, (PEER_TOPK, LANES), 0).astype(F32)
    for c in range(pq_ref.shape[0] // LANES):
        rows = slice(c * LANES, (c + 1) * LANES)
        qh = pq_ref[rows, :]
        s1 = _nt_dot(sk1_ref[...], qh[:, :PEER_HALF])
        s2 = _nt_dot(sk2_ref[...], qh[:, PEER_HALF:])
        v1, i1 = _top16_rows(s1, key_col)
        v2, i2 = _top16_rows(s2, key_col)
        v2a = jnp.concatenate(v2, axis=0)
        i2a = jnp.concatenate(i2, axis=0)
        cand = [v1[0] + v2a]
        flat = [row16]
        code = [i1[0] * N_KEYS + i2a]
        for j1 in range(1, 8):
            cand.append(v1[j1] + v2a[:8])
            flat.append(row8 + float(j1 * PEER_TOPK))
            code.append(i1[j1] * N_KEYS + i2a[:8])
        cand.append(jnp.concatenate(v1[8:], axis=0) + v2[0])
        flat.append((row8 + 8.0) * float(PEER_TOPK))
        code.append(jnp.concatenate(i1[8:], axis=0) * N_KEYS + i2[0])
        cand = jnp.concatenate(cand, axis=0)
        flat = jnp.concatenate(flat, axis=0)
        code = jnp.concatenate(code, axis=0)
        top_s, top_e = [], []
        for _ in range(PEER_TOPK):
            m = jnp.max(cand, axis=0, keepdims=True)
            sel = jnp.min(jnp.where(cand == m, flat, 1e9), axis=0, keepdims=True)
            hit = flat == sel
            top_e.append(jnp.max(jnp.where(hit, code, -1.0), axis=0, keepdims=True))
            cand = jnp.where(hit, -jnp.inf, cand)
            top_s.append(m)
        ts = jnp.concatenate(top_s, axis=0)
        te = jnp.concatenate(top_e, axis=0)
        e = jnp.exp(ts - ts[0:1])
        gate = e / jnp.sum(e, axis=0, keepdims=True)
        a = jnp.floor(te * (1.0 / N_KEYS))
        i1_ref[:, rows] = a
        i2_ref[:, rows] = te - a * N_KEYS
        gate_ref[:, rows] = gate


def _route(pq, sk1_bf, sk2_bf, tm):
    t = pq.shape[0]
    out = pl.BlockSpec((PEER_TOPK, tm), lambda i, h: (h, i))
    shp = jax.ShapeDtypeStruct((PEER_HEADS * PEER_TOPK, t), F32)
    return pl.pallas_call(
        _route_kernel,
        out_shape=(shp, shp, shp),
        grid=(t // tm, PEER_HEADS),
        in_specs=[pl.BlockSpec((tm, 2 * PEER_HALF), lambda i, h: (i, h)),
                  _resident(sk1_bf.shape), _resident(sk2_bf.shape)],
        out_specs=(out, out, out),
        compiler_params=_cparams(("parallel", "parallel")),
        name="peer_route",
    )(pq, sk1_bf, sk2_bf)


def _peer_kernel(h2_ref, x1_ref, i1_ref, i2_ref, gt_ref, p_ref, ed_ref, eu_ref,
                 g3_ref, wpg_ref, wpp_ref, y_ref, pk_ref, acc_ref, i1s_ref, i2s_ref, gts_ref):
    tm = h2_ref.shape[0]
    j = pl.program_id(1)
    half = N_KEYS // 2

    @pl.when(j == 0)
    def _build_gates():
        i1s_ref[...] = i1_ref[...].T
        i2s_ref[...] = i2_ref[...].T
        gts_ref[...] = gt_ref[...].T
        acc_ref[...] = jnp.zeros_like(acc_ref)
        sub = lax.broadcasted_iota(jnp.int32, (N_KEYS, N_KEYS), 0)
        b_of_row = sub.astype(F32)
        a_of_row = jnp.where(sub < half, 2 * sub, 2 * (sub - half) + 1).astype(F32)

        def body(t, carry):
            r1 = i1s_ref[pl.ds(t, 1), :]
            r2 = i2s_ref[pl.ds(t, 1), :]
            rg = gts_ref[pl.ds(t, 1), :]
            oat = jnp.where(a_of_row == r1, 1.0, 0.0).astype(BF16)
            gbt = jnp.where(b_of_row == r2, rg, 0.0).astype(BF16)
            g = _nt_dot(oat, gbt)
            lo = pltpu.bitcast(g[:half], jnp.uint32) + jnp.uint32(0x8000)
            hi = pltpu.bitcast(g[half:], jnp.uint32) + jnp.uint32(0x8000)
            packed = (lo >> 16) | (hi & jnp.uint32(0xFFFF0000))
            pk_ref[pl.ds(pl.multiple_of(t * G_PITCH, 8), half), :] = packed
            return carry

        lax.fori_loop(0, tm, body, 0, unroll=8)

    gs = []
    for r in range(PAIRS_PER_STEP):
        w = pk_ref[pl.ds(j * PAIRS_PER_STEP + r, tm, stride=G_PITCH), :]
        gs.append(pltpu.bitcast(w << 16, F32))
        gs.append(pltpu.bitcast(w & jnp.uint32(0xFFFF0000), F32))
    gates = jnp.concatenate(gs, axis=1)
    s = _nt_dot(h2_ref[...], ed_ref[...])
    act = (jax.nn.gelu(s) * gates).astype(BF16)
    acc_ref[...] += jnp.dot(act, eu_ref[...], preferred_element_type=F32)

    @pl.when(j == pl.num_programs(1) - 1)
    def _finish():
        x2 = x1_ref[...] + acc_ref[...]
        h3 = _rms(x2, g3_ref[...]).astype(BF16)
        gate = jax.nn.sigmoid(jnp.dot(h3, wpg_ref[...], preferred_element_type=F32))
        y_ref[...] = x2 + gate * jnp.dot(p_ref[...].astype(BF16), wpp_ref[...], preferred_element_type=F32)


def _peer_ple(h2, x1, i1, i2, gt, p2d, ed, eu, g3, wpg, wpp, tm):
    t = h2.shape[0]
    nb = PAIRS_PER_STEP * 2 * N_KEYS
    steps = ed.shape[0] // nb
    row = lambda w: pl.BlockSpec((tm, w), lambda i, j: (i, 0))
    pick = pl.BlockSpec((PEER_HEADS * PEER_TOPK, tm), lambda i, j: (0, i))
    table = pl.BlockSpec((nb, D_MODEL), lambda i, j: (j, 0))
    return pl.pallas_call(
        _peer_kernel,
        out_shape=jax.ShapeDtypeStruct((t, D_MODEL), F32),
        grid=(t // tm, steps),
        in_specs=[row(D_MODEL), row(D_MODEL), pick, pick, pick, row(PLE_DIM), table, table,
                  _resident(g3.shape), _resident(wpg.shape), _resident(wpp.shape)],
        out_specs=row(D_MODEL),
        scratch_shapes=[pltpu.VMEM((tm * G_PITCH, N_KEYS), jnp.uint32),
                        pltpu.VMEM((tm, D_MODEL), F32),
                        pltpu.VMEM((tm, N_KEYS), F32), pltpu.VMEM((tm, N_KEYS), F32), pltpu.VMEM((tm, N_KEYS), F32)],
        compiler_params=_cparams(("parallel", "arbitrary"), V7X_VMEM_LIMIT),
        name="peer_ple",
    )(h2, x1, i1, i2, gt, p2d, ed, eu, g3, wpg, wpp)


def _t5_bucket(dist):
    max_exact = N_BUCKETS // 2
    df = jnp.maximum(dist, max_exact).astype(F32)
    large = max_exact + (jnp.log(df / max_exact) / math.log(MAX_DISTANCE / max_exact)
                         * (N_BUCKETS - max_exact)).astype(jnp.int32)
    return jnp.where(dist < max_exact, dist, jnp.minimum(large, N_BUCKETS - 1))


def _bias_tables(rel_bias, g):
    dil = DILATIONS[g]
    hi = lax.Precision.HIGHEST
    bucket = _t5_bucket(jnp.arange(BAND + 1) * dil)
    pick = (bucket[:, None] == jnp.arange(N_BUCKETS)[None, :]).astype(F32)
    per_dist = jnp.dot(pick, rel_bias[:, g * HEADS:(g + 1) * HEADS].astype(F32), precision=hi)
    dsub = np.arange(BAND)[:, None] + BAND - np.arange(2 * BAND)[None, :]
    valid = (dsub >= 0) & (dsub <= WINDOWS[g] // dil)
    spread = (jnp.asarray(np.clip(dsub, 0, BAND).reshape(1, -1)) == jnp.arange(BAND + 1)[:, None]).astype(F32)
    band = jnp.dot(per_dist.T, spread, precision=hi).reshape(HEADS, BAND, 2 * BAND)
    add = jnp.where(jnp.asarray(valid)[None], band, NEG_INF)
    addc = per_dist[::-1][:BAND].T
    addn = per_dist[0][:, None]
    return add, addc, addn


def kernel(x_prompt, x_sample, cache_kv_w128, cache_kv_w512, cache_kv_w2048, p_prompt, p_sample, rel_bias, norm1,
           w_in, q_norm, k_norm, b_v_norm, w_spatial, b_spatial, w_up_a, w_up_b, w_out, norm2, w_query,
           sub_keys_1, sub_keys_2, expert_down, expert_up, norm3, w_ple_gate, w_ple_proj):
    bsz, seq, _ = x_prompt.shape
    db, ds, _ = x_sample.shape
    depth = norm1.shape[0]
    assert depth == 1 and ds == 1 and seq % (BAND * DILATIONS[-1]) == 0 and seq >= WINDOWS[-1]
    caches = (cache_kv_w128, cache_kv_w512, cache_kv_w2048)
    for g in range(N_GROUPS):
        assert caches[g].shape[2] == WINDOWS[g]
    t_p = bsz * seq
    t_s = 128
    li = 0

    g1 = norm1[li][None, :]
    w_in_bf = w_in[li].astype(BF16)
    qg = jnp.tile(q_norm[li], (1, HEADS))
    kg = jnp.tile(k_norm[li], (1, HEADS))
    bvg = b_v_norm[li][None, :]
    blk = np.arange(V7X_MXU_DIM) // HEAD_DIM
    hsum = jnp.asarray(blk[:, None] == blk[None, :], dtype=BF16)
    wua, wub, wo = w_up_a[li].astype(BF16), w_up_b[li].astype(BF16), w_out[li].astype(BF16)
    g2, g3 = norm2[li][None, :], norm3[li][None, :]
    wq = w_query[li].astype(BF16)
    sk1, sk2 = sub_keys_1[li].astype(BF16), sub_keys_2[li].astype(BF16)
    ed = expert_down[li].astype(BF16)
    eu = expert_up[li].astype(BF16)
    wpg, wpp = w_ple_gate[li].astype(BF16), w_ple_proj[li].astype(BF16)
    bs_full = jnp.repeat(b_spatial[li].T, D_MODEL // B_GROUPS, axis=1)
    tables = [_bias_tables(rel_bias, g) for g in range(N_GROUPS)]

    res = _in_projection_prompt(x_prompt, g1, w_in_bf, qg, kg, bvg, hsum, TM_PROJ)
    qs, ks, vs, tails = res[0:3], res[3:6], res[6:9], res[9:12]
    ub, vb, ga, gb = res[12:16]
    outs, lses, kv_prompt = [], [], []
    for g in range(N_GROUPS):
        o, lse = _prompt_attention(qs[g], ks[g], vs[g], tables[g][0], g)
        outs.append(o)
        lses.append(lse)
        kv_prompt.append(tails[g].reshape(1, bsz, tails[g].shape[1], 2, HEADS, HEAD_DIM))
    x1, h2, pq = _merge_prompt(x_prompt, outs, lses, ub, vb, ga, gb, w_spatial[li], bs_full, wua, wub, wo, g2, wq,
                               TM_PROJ)
    x1, h2, pq = x1.reshape(t_p, D_MODEL), h2.reshape(t_p, D_MODEL), pq.reshape(t_p, -1)
    i1, i2, gt = _route(pq, sk1, sk2, TM_PROJ)
    y_prompt = _peer_ple(h2, x1, i1, i2, gt, p_prompt[li].reshape(t_p, PLE_DIM), ed, eu, g3, wpg, wpp, TM_PEER)
    y_prompt = y_prompt.reshape(bsz, seq, D_MODEL)

    xs = jnp.pad(x_sample.reshape(db, D_MODEL), ((0, t_s - db), (0, 0)))
    q, k, v, ub, vb, ga, gb = _in_projection_sample(xs, g1, w_in_bf, qg, kg, bvg, hsum)
    outs, lses, kv_sample = [], [], []
    for g in range(N_GROUPS):
        dil = DILATIONS[g]
        cols = slice(g * GROUP_W, (g + 1) * GROUP_W)
        cache_view = caches[g][li][:, ::dil].reshape(db, WINDOWS[g] // dil, 2 * GROUP_W)
        o, lse = _sample_attention(q[:db, None, :], k[:db, None, :], v[:db, None, :], cache_view,
                                   tables[g][1], tables[g][2], g)
        pad = lambda a: jnp.pad(a.reshape(db, GROUP_W), ((0, t_s - db), (0, 0)))
        outs.append(pad(o))
        lses.append(pad(lse))
        kn = k[:db, cols].reshape(db, 1, HEADS, HEAD_DIM)
        vn = v[:db, cols].reshape(db, 1, HEADS, HEAD_DIM)
        kv_sample.append(jnp.stack([kn, vn], axis=2)[None])
    ws0 = jnp.repeat(w_spatial[li][:, 0, 0], D_MODEL // B_GROUPS)[None, :]
    bs0 = bs_full[0:1, :]
    x1, h2, pq = _merge_sample(xs, outs, lses, ub, vb, ga, gb, ws0, bs0, wua, wub, wo, g2, wq)
    i1, i2, gt = _route(pq, sk1, sk2, t_s)
    ps = jnp.pad(p_sample[li].reshape(db, PLE_DIM), ((0, t_s - db), (0, 0)))
    y_sample = _peer_ple(h2, x1, i1, i2, gt, ps, ed, eu, g3, wpg, wpp, t_s)[:db].reshape(db, ds, D_MODEL)
    v_sample_chunk = vb[:db].reshape(1, db, ds, D_MODEL)

    return (y_prompt, y_sample, kv_prompt[0], kv_prompt[1], kv_prompt[2],
            kv_sample[0], kv_sample[1], kv_sample[2], v_sample_chunk)
```

```python
import functools
import math

import numpy as np
import jax
import jax.numpy as jnp
from jax import lax
from jax.experimental import pallas as pl
from jax.experimental.pallas import tpu as pltpu

F32 = jnp.float32
BF16 = jnp.bfloat16

D_MODEL = 1024
N_GROUPS = 3
HEADS = 8
HEAD_DIM = 64
GROUP_W = HEADS * HEAD_DIM
QKV_W = N_GROUPS * GROUP_W
WINDOWS = (128, 512, 2048)
DILATIONS = (1, 4, 16)
BAND = 128
N_BUCKETS = 32
MAX_DISTANCE = 2048
CHUNK = 128
B_GROUPS = 4
N_KEYS = 128
PEER_HEADS = 8
PEER_TOPK = 16
PEER_HALF = 128
PLE_DIM = 256
EPS = 1e-6
NEG_INF = -1e30
IN_OFFS = (0, QKV_W, 2 * QKV_W, 3 * QKV_W, 3 * QKV_W + D_MODEL, 3 * QKV_W + 2 * D_MODEL,
           3 * QKV_W + 3 * D_MODEL, 3 * QKV_W + 4 * D_MODEL)

LANES = 128
SLABS = GROUP_W // LANES
V7X_VMEM_LIMIT = 56 * 1024 * 1024
V7X_MXU_DIM = 256
G_PITCH = 72
TM_PROJ = 256
TM_PEER = 512
PAIRS_PER_STEP = 4


def _cparams(sem, vmem=None):
    return pltpu.CompilerParams(dimension_semantics=sem, vmem_limit_bytes=vmem)


def _resident(shape):
    nd = len(shape)
    return pl.BlockSpec(shape, lambda *_: (0,) * nd)


def _nt_dot(a, b):
    return lax.dot_general(a, b, (((1,), (1,)), ((), ())), preferred_element_type=F32)


def _rms(x, gain):
    return x * lax.rsqrt(jnp.mean(x * x, axis=-1, keepdims=True) + EPS) * gain


def _head_norm(y, hsum_ref, gain_row):
    y2 = (y * y).astype(BF16)
    w = hsum_ref.shape[0]
    ss = jnp.concatenate([jnp.dot(y2[:, c:c + w], hsum_ref[...], preferred_element_type=F32)
                          for c in range(0, GROUP_W, w)], axis=1)
    return y * lax.rsqrt(ss * (1.0 / HEAD_DIM) + EPS) * gain_row


def _store_residue_major(out_ref, y, scr_ref, dil):
    if dil == 1:
        out_ref[0, 0] = y.astype(out_ref.dtype)
        return
    n = y.shape[0] // dil
    for s in range(SLABS):
        scr_ref[s] = y[:, s * LANES:(s + 1) * LANES]
    for r in range(dil):
        for s in range(SLABS):
            out_ref[0, r, :, s * LANES:(s + 1) * LANES] = scr_ref[s, pl.ds(r, n, stride=dil), :].astype(out_ref.dtype)


def _gmlp_and_gates(proj, bvg_ref, ub_ref, vb_ref, ga_ref, gb_ref, st):
    half = D_MODEL // 2
    for c in range(2):
        st(ub_ref, c * half, jax.nn.gelu(proj(IN_OFFS[3] + c * half, half)))
    vb0 = jax.nn.gelu(proj(IN_OFFS[4], half))
    vb1 = jax.nn.gelu(proj(IN_OFFS[4] + half, half))
    ms = (jnp.sum(vb0 * vb0, axis=-1, keepdims=True) + jnp.sum(vb1 * vb1, axis=-1, keepdims=True)) * (1.0 / D_MODEL)
    r = lax.rsqrt(ms + EPS)
    st(vb_ref, 0, vb0 * r * bvg_ref[:, :half])
    st(vb_ref, half, vb1 * r * bvg_ref[:, half:])
    for c in range(2):
        st(ga_ref, c * half, proj(IN_OFFS[5] + c * half, half))
        st(gb_ref, c * half, proj(IN_OFFS[6] + c * half, half))


def _inproj_prompt_kernel(x_ref, g1_ref, w_ref, qg_ref, kg_ref, bvg_ref, hsum_ref,
                          q0_ref, q1_ref, q2_ref, k0_ref, k1_ref, k2_ref, v0_ref, v1_ref, v2_ref,
                          t0_ref, t1_ref, t2_ref, ub_ref, vb_ref, ga_ref, gb_ref, scr_ref, *, tail_first):
    ti = pl.program_id(1)
    tm = x_ref.shape[1]
    hb = _rms(x_ref[0], g1_ref[...]).astype(BF16)

    def proj(c0, width):
        return jnp.dot(hb, w_ref[:, c0:c0 + width], preferred_element_type=F32)

    q_refs, k_refs, v_refs, t_refs = (q0_ref, q1_ref, q2_ref), (k0_ref, k1_ref, k2_ref), (v0_ref, v1_ref, v2_ref), \
        (t0_ref, t1_ref, t2_ref)
    for g in range(N_GROUPS):
        lo = g * GROUP_W
        dil = DILATIONS[g]
        qn = _head_norm(proj(IN_OFFS[0] + lo, GROUP_W), hsum_ref, qg_ref[g:g + 1, :]) * (HEAD_DIM ** -0.5)
        _store_residue_major(q_refs[g], qn, scr_ref, dil)
        kn = _head_norm(proj(IN_OFFS[1] + lo, GROUP_W), hsum_ref, kg_ref[g:g + 1, :])
        _store_residue_major(k_refs[g], kn, scr_ref, dil)
        vv = proj(IN_OFFS[2] + lo, GROUP_W)
        _store_residue_major(v_refs[g], vv, scr_ref, dil)
        rows = t_refs[g].shape[1]

        @pl.when(ti >= tail_first[g])
        def _():
            t_refs[g][0, :, :GROUP_W] = kn[tm - rows:, :]
            t_refs[g][0, :, GROUP_W:] = vv[tm - rows:, :]

    def st(ref, c0, val):
        ref[0, :, c0:c0 + val.shape[1]] = val.astype(ref.dtype)

    _gmlp_and_gates(proj, bvg_ref, ub_ref, vb_ref, ga_ref, gb_ref, st)


def _inproj_sample_kernel(x_ref, g1_ref, w_ref, qg_ref, kg_ref, bvg_ref, hsum_ref,
                          q_ref, k_ref, v_ref, ub_ref, vb_ref, ga_ref, gb_ref):
    hb = _rms(x_ref[...], g1_ref[...]).astype(BF16)

    def proj(c0, width):
        return jnp.dot(hb, w_ref[:, c0:c0 + width], preferred_element_type=F32)

    for g in range(N_GROUPS):
        lo = g * GROUP_W
        qn = _head_norm(proj(IN_OFFS[0] + lo, GROUP_W), hsum_ref, qg_ref[g:g + 1, :]) * (HEAD_DIM ** -0.5)
        q_ref[:, lo:lo + GROUP_W] = qn.astype(q_ref.dtype)
        k_ref[:, lo:lo + GROUP_W] = _head_norm(proj(IN_OFFS[1] + lo, GROUP_W), hsum_ref, kg_ref[g:g + 1, :])
        v_ref[:, lo:lo + GROUP_W] = proj(IN_OFFS[2] + lo, GROUP_W)

    def st(ref, c0, val):
        ref[:, c0:c0 + val.shape[1]] = val.astype(ref.dtype)

    _gmlp_and_gates(proj, bvg_ref, ub_ref, vb_ref, ga_ref, gb_ref, st)


def _proj_weight_specs(w_in_bf):
    return [_resident((1, D_MODEL)),
            pl.BlockSpec(w_in_bf.shape, lambda *_: (0, 0), pipeline_mode=pl.Buffered(1)),
            _resident((N_GROUPS, GROUP_W)), _resident((N_GROUPS, GROUP_W)), _resident((1, D_MODEL)),
            _resident((V7X_MXU_DIM, V7X_MXU_DIM))]


def _in_projection_prompt(x, g1, w_in_bf, qg, kg, bvg, hsum, tm):
    b, s, _ = x.shape
    nt = s // tm
    row = pl.BlockSpec((1, tm, D_MODEL), lambda bi, ti: (bi, ti, 0))
    qkv_shapes, qkv_specs, tail_shapes, tail_specs, tail_first = [], [], [], [], []
    for g in range(N_GROUPS):
        dil = DILATIONS[g]
        qkv_shapes.append(jax.ShapeDtypeStruct((b, dil, s // dil, GROUP_W), BF16))
        qkv_specs.append(pl.BlockSpec((1, dil, tm // dil, GROUP_W), lambda bi, ti: (bi, 0, ti, 0)))
        ln = min(WINDOWS[g], s)
        rows = min(ln, tm)
        first = (s - ln) // tm if ln >= tm else nt - 1
        tail_first.append(first)
        tail_shapes.append(jax.ShapeDtypeStruct((b, ln, 2 * GROUP_W), F32))
        tail_specs.append(pl.BlockSpec((1, rows, 2 * GROUP_W),
                                       functools.partial(lambda bi, ti, f: (bi, jnp.maximum(ti - f, 0), 0), f=first)))
    act = jax.ShapeDtypeStruct((b, s, D_MODEL), BF16)
    gate = jax.ShapeDtypeStruct((b, s, D_MODEL), F32)
    return pl.pallas_call(
        functools.partial(_inproj_prompt_kernel, tail_first=tuple(tail_first)),
        out_shape=tuple(qkv_shapes * 3 + tail_shapes + [act, act, gate, gate]),
        grid=(b, nt),
        in_specs=[row] + _proj_weight_specs(w_in_bf),
        out_specs=tuple(qkv_specs * 3 + tail_specs + [row] * 4),
        scratch_shapes=[pltpu.VMEM((SLABS, tm, LANES), F32)],
        compiler_params=_cparams(("arbitrary", "arbitrary"), V7X_VMEM_LIMIT),
        name="in_projection",
    )(x, g1, w_in_bf, qg, kg, bvg, hsum)


def _in_projection_sample(x2d, g1, w_in_bf, qg, kg, bvg, hsum):
    t = x2d.shape[0]
    row = lambda w: pl.BlockSpec((t, w), lambda i: (0, 0))
    outs = (jax.ShapeDtypeStruct((t, QKV_W), BF16), jax.ShapeDtypeStruct((t, QKV_W), F32),
            jax.ShapeDtypeStruct((t, QKV_W), F32), jax.ShapeDtypeStruct((t, D_MODEL), BF16),
            jax.ShapeDtypeStruct((t, D_MODEL), F32), jax.ShapeDtypeStruct((t, D_MODEL), F32),
            jax.ShapeDtypeStruct((t, D_MODEL), F32))
    return pl.pallas_call(
        _inproj_sample_kernel,
        out_shape=outs,
        grid=(1,),
        in_specs=[row(D_MODEL)] + _proj_weight_specs(w_in_bf),
        out_specs=(row(QKV_W), row(QKV_W), row(QKV_W), row(D_MODEL), row(D_MODEL), row(D_MODEL), row(D_MODEL)),
        compiler_params=_cparams(("arbitrary",), V7X_VMEM_LIMIT),
        name="in_projection_sample",
    )(x2d, g1, w_in_bf, qg, kg, bvg, hsum)


def _attn_kernel(q_ref, kp_ref, kc_ref, vp_ref, vc_ref, add_ref, o_ref, lse_ref):
    i = pl.program_id(2)
    q = q_ref[0, 0]
    kk = jnp.concatenate([kp_ref[0, 0], kc_ref[0, 0]], axis=0)
    vv = jnp.concatenate([vp_ref[0, 0], vc_ref[0, 0]], axis=0)
    lane = lax.broadcasted_iota(jnp.int32, (BAND, LANES), 1)
    kcol = lax.broadcasted_iota(jnp.int32, (BAND, 2 * BAND), 1)
    no_prev = kcol < jnp.where(i == 0, BAND, 0)
    for pair in range(HEADS // 2):
        sl = slice(pair * LANES, (pair + 1) * LANES)
        qp, kp, vp = q[:, sl].astype(F32), kk[:, sl], vv[:, sl]
        o_pair = jnp.zeros((BAND, LANES), F32)
        l_pair = jnp.zeros((BAND, LANES), F32)
        for hh in range(2):
            mine = (lane >= HEAD_DIM) if hh else (lane < HEAD_DIM)
            s = _nt_dot(jnp.where(mine, qp, 0.0).astype(BF16), kp) + add_ref[pair * 2 + hh]
            s = jnp.where(no_prev, NEG_INF, s)
            m = jnp.max(s, axis=-1, keepdims=True)
            p = jnp.exp(s - m)
            den = jnp.sum(p, axis=-1, keepdims=True)
            o = jnp.dot(p.astype(BF16), vp, preferred_element_type=F32) / den
            o_pair = jnp.where(mine, o, o_pair)
            l_pair = jnp.where(mine, m + jnp.log(den), l_pair)
        o_ref[0, 0, :, sl] = o_pair.astype(o_ref.dtype)
        lse_ref[0, 0, :, sl] = l_pair


def _prompt_attention(q4, k4, v4, add, g):
    b, dil, n, _ = q4.shape
    cur = pl.BlockSpec((1, 1, BAND, GROUP_W), lambda bi, r, i: (bi, r, i, 0))
    prev = pl.BlockSpec((1, 1, BAND, GROUP_W), lambda bi, r, i: (bi, r, jnp.maximum(i - 1, 0), 0))
    return pl.pallas_call(
        _attn_kernel,
        out_shape=(jax.ShapeDtypeStruct(q4.shape, BF16), jax.ShapeDtypeStruct(q4.shape, F32)),
        grid=(b, dil, n // BAND),
        in_specs=[cur, prev, cur, prev, cur, _resident(add.shape)],
        out_specs=(cur, cur),
        compiler_params=_cparams(("parallel", "parallel", "arbitrary")),
        name=f"attn_g{g}",
    )(q4, k4, k4, v4, v4, add)


def _sample_attn_kernel(q_ref, kn_ref, vn_ref, c0_ref, c1_ref, c2_ref, addc_ref, addn_ref, o_ref, lse_ref):
    rounded = lambda a: a.astype(BF16).astype(F32)
    for g, c_ref in enumerate((c0_ref, c1_ref, c2_ref)):
        hs = slice(g * HEADS, (g + 1) * HEADS)
        q = q_ref[0, hs, :]
        kc, vc = rounded(c_ref[:, 0]), rounded(c_ref[:, 1])
        kn, vn = rounded(kn_ref[0, hs, :]), rounded(vn_ref[0, hs, :])
        s_c = jnp.sum(kc * q[None], axis=-1, keepdims=True) + addc_ref[g]
        s_n = jnp.sum(kn * q, axis=-1, keepdims=True) + addn_ref[g]
        m = jnp.maximum(jnp.max(s_c, axis=0), s_n)
        p_c = jnp.exp(s_c - m[None])
        p_n = jnp.exp(s_n - m)
        den = jnp.sum(p_c, axis=0) + p_n
        o = (jnp.sum(rounded(p_c) * vc, axis=0) + rounded(p_n) * vn) / den
        o_ref[0, hs, :] = o
        lse_ref[0, hs, :] = jnp.broadcast_to(m + jnp.log(den), (HEADS, HEAD_DIM))


def _sample_attention(q3, k3, v3, cache_views, addc, addn):
    db = q3.shape[0]
    tok = pl.BlockSpec((1, N_GROUPS * HEADS, HEAD_DIM), lambda bi: (bi, 0, 0))
    cache = pl.BlockSpec((None, BAND, None, 2, HEADS, HEAD_DIM), lambda bi: (bi, 0, 0, 0, 0, 0))
    shp = jax.ShapeDtypeStruct(q3.shape, F32)
    return pl.pallas_call(
        _sample_attn_kernel,
        out_shape=(shp, shp),
        grid=(db,),
        in_specs=[tok, tok, tok, cache, cache, cache, _resident(addc.shape), _resident(addn.shape)],
        out_specs=(tok, tok),
        compiler_params=_cparams(("parallel",)),
        name="sample_attn",
    )(q3, k3, v3, *cache_views, addc, addn)


def _load_token_major(ref, scr_ref, dil):
    if dil == 1:
        return ref[0, 0].astype(F32)
    n = ref.shape[2]
    for r in range(dil):
        blk = ref[0, r].astype(F32)
        for s in range(SLABS):
            scr_ref[s, pl.ds(r, n, stride=dil), :] = blk[:, s * LANES:(s + 1) * LANES]
    return jnp.concatenate([scr_ref[s] for s in range(SLABS)], axis=1)


def _merge_tail(x, a_out, b_out, ga, gb, wua_ref, wub_ref, wo_ref, g2_ref, wq_ref):
    up_a = jnp.dot(a_out.astype(BF16), wua_ref[...], preferred_element_type=F32)
    up_b = jnp.dot(b_out.astype(BF16), wub_ref[...], preferred_element_type=F32)
    mrg = jax.nn.sigmoid(ga) * up_a + jax.nn.sigmoid(gb) * up_b
    x1 = x + jnp.dot(mrg.astype(BF16), wo_ref[...], preferred_element_type=F32)
    h2 = _rms(x1, g2_ref[...]).astype(BF16)
    return x1, h2, jnp.dot(h2, wq_ref[...], preferred_element_type=F32)


def _combine_groups(os, ls):
    m = jnp.maximum(jnp.maximum(ls[0], ls[1]), ls[2])
    es = [jnp.exp(l - m) for l in ls]
    return (es[0] * os[0] + es[1] * os[1] + es[2] * os[2]) / (es[0] + es[1] + es[2])


def _merge_prompt_kernel(x_ref, o0_ref, o1_ref, o2_ref, l0_ref, l1_ref, l2_ref, ub_ref, vb_ref, ga_ref, gb_ref,
                         ws_ref, bs_ref, wua_ref, wub_ref, wo_ref, g2_ref, wq_ref,
                         x1_ref, h2_ref, pq_ref, scr_ref):
    tm = x_ref.shape[1]
    os = [_load_token_major(r, scr_ref, DILATIONS[g]) for g, r in enumerate((o0_ref, o1_ref, o2_ref))]
    ls = [_load_token_major(r, scr_ref, DILATIONS[g]) for g, r in enumerate((l0_ref, l1_ref, l2_ref))]
    a_out = _combine_groups(os, ls)

    gw = D_MODEL // B_GROUPS
    tri = lax.broadcasted_iota(jnp.int32, (CHUNK, CHUNK), 0) >= lax.broadcasted_iota(jnp.int32, (CHUNK, CHUNK), 1)
    parts = []
    for c in range(tm // CHUNK):
        rows = slice(c * CHUNK, (c + 1) * CHUNK)
        cols = []
        for g in range(B_GROUPS):
            w = jnp.where(tri, ws_ref[g], 0.0).astype(BF16)
            cols.append(jnp.dot(w, vb_ref[0, rows, g * gw:(g + 1) * gw], preferred_element_type=F32))
        mixed = jnp.concatenate(cols, axis=1) + bs_ref[...]
        parts.append(ub_ref[0, rows, :].astype(F32) * mixed)
    b_out = jnp.concatenate(parts, axis=0)

    x1, h2, pq = _merge_tail(x_ref[0], a_out, b_out, ga_ref[0], gb_ref[0], wua_ref, wub_ref, wo_ref, g2_ref, wq_ref)
    x1_ref[0] = x1
    h2_ref[0] = h2
    pq_ref[0] = pq.astype(pq_ref.dtype)


def _merge_sample_kernel(x_ref, o0_ref, o1_ref, o2_ref, l0_ref, l1_ref, l2_ref, ub_ref, vb_ref, ga_ref, gb_ref,
                         ws_ref, bs_ref, wua_ref, wub_ref, wo_ref, g2_ref, wq_ref, x1_ref, h2_ref, pq_ref):
    a_out = _combine_groups([o0_ref[...], o1_ref[...], o2_ref[...]], [l0_ref[...], l1_ref[...], l2_ref[...]])
    b_out = ub_ref[...].astype(F32) * (vb_ref[...] * ws_ref[...] + bs_ref[...])
    x1, h2, pq = _merge_tail(x_ref[...], a_out, b_out, ga_ref[...], gb_ref[...], wua_ref, wub_ref, wo_ref, g2_ref, wq_ref)
    x1_ref[...] = x1
    h2_ref[...] = h2
    pq_ref[...] = pq.astype(pq_ref.dtype)


def _merge_prompt(x, o, lse, ub, vb, ga, gb, ws, bs, wua, wub, wo, g2, wq, tm):
    b, s, _ = x.shape
    row = lambda w: pl.BlockSpec((1, tm, w), lambda bi, ti: (bi, ti, 0))
    dilated = [pl.BlockSpec((1, d, tm // d, GROUP_W), lambda bi, ti: (bi, 0, ti, 0)) for d in DILATIONS]
    qw = wq.shape[1]
    weights = (ws, bs, wua, wub, wo, g2, wq)
    return pl.pallas_call(
        _merge_prompt_kernel,
        out_shape=(jax.ShapeDtypeStruct((b, s, D_MODEL), F32), jax.ShapeDtypeStruct((b, s, D_MODEL), BF16),
                   jax.ShapeDtypeStruct((b, s, qw), BF16)),
        grid=(b, s // tm),
        in_specs=[row(D_MODEL)] + dilated + dilated + [row(D_MODEL)] * 4 + [_resident(w.shape) for w in weights],
        out_specs=(row(D_MODEL), row(D_MODEL), row(qw)),
        scratch_shapes=[pltpu.VMEM((SLABS, tm, LANES), F32)],
        compiler_params=_cparams(("parallel", "parallel"), V7X_VMEM_LIMIT),
        name="merge",
    )(x, *o, *lse, ub, vb, ga, gb, *weights)


def _merge_sample(x2d, o, lse, ub, vb, ga, gb, ws0, bs0, wua, wub, wo, g2, wq):
    t = x2d.shape[0]
    row = lambda w: pl.BlockSpec((t, w), lambda i: (0, 0))
    qw = wq.shape[1]
    weights = (ws0, bs0, wua, wub, wo, g2, wq)
    return pl.pallas_call(
        _merge_sample_kernel,
        out_shape=(jax.ShapeDtypeStruct((t, D_MODEL), F32), jax.ShapeDtypeStruct((t, D_MODEL), BF16),
                   jax.ShapeDtypeStruct((t, qw), BF16)),
        grid=(1,),
        in_specs=[row(D_MODEL)] + [row(GROUP_W)] * 6 + [row(D_MODEL)] * 4 + [_resident(w.shape) for w in weights],
        out_specs=(row(D_MODEL), row(D_MODEL), row(qw)),
        compiler_params=_cparams(("arbitrary",), V7X_VMEM_LIMIT),
        name="merge_sample",
    )(x2d, *o, *lse, ub, vb, ga, gb, *weights)


def _top16_rows(s, idx_col):
    vals, idxs = [], []
    big = jnp.float32(1e9)
    for _ in range(PEER_TOPK):
        m = jnp.max(s, axis=0, keepdims=True)
        sel = jnp.min(jnp.where(s == m, idx_col, big), axis=0, keepdims=True)
        s = jnp.where(idx_col == sel, -jnp.inf, s)
        vals.append(m)
        idxs.append(sel)
    return vals, idxs


def _route_kernel(pq_ref, sk1_ref, sk2_ref, i1_ref, i2_ref, gate_ref):
    key_col = lax.broadcasted_iota(jnp.int32, (N_KEYS, LANES), 0).astype(F32)
    row8 = lax.broadcasted_iota(jnp.int32, (8, LANES), 0).astype(F32)
    row16 = lax.broadcasted_iota(jnp.int32, (PEER_TOPK, LANES), 0).astype(F32)
    for c in range(pq_ref.shape[0] // LANES):
        rows = slice(c * LANES, (c + 1) * LANES)
        qh = pq_ref[rows, :]
        s1 = _nt_dot(sk1_ref[...], qh[:, :PEER_HALF])
        s2 = _nt_dot(sk2_ref[...], qh[:, PEER_HALF:])
        v1, i1 = _top16_rows(s1, key_col)
        v2, i2 = _top16_rows(s2, key_col)
        v2a = jnp.concatenate(v2, axis=0)
        i2a = jnp.concatenate(i2, axis=0)
        cand = [v1[0] + v2a]
        flat = [row16]
        code = [i1[0] * N_KEYS + i2a]
        for j1 in range(1, 8):
            cand.append(v1[j1] + v2a[:8])
            flat.append(row8 + float(j1 * PEER_TOPK))
            code.append(i1[j1] * N_KEYS + i2a[:8])
        cand.append(jnp.concatenate(v1[8:], axis=0) + v2[0])
        flat.append((row8 + 8.0) * float(PEER_TOPK))
        code.append(jnp.concatenate(i1[8:], axis=0) * N_KEYS + i2[0])
        cand = jnp.concatenate(cand, axis=0)
        flat = jnp.concatenate(flat, axis=0)
        code = jnp.concatenate(code, axis=0)
        top_s, top_e = [], []
        for _ in range(PEER_TOPK):
            m = jnp.max(cand, axis=0, keepdims=True)
            sel = jnp.min(jnp.where(cand == m, flat, 1e9), axis=0, keepdims=True)
            hit = flat == sel
            top_e.append(jnp.max(jnp.where(hit, code, -1.0), axis=0, keepdims=True))
            cand = jnp.where(hit, -jnp.inf, cand)
            top_s.append(m)
        ts = jnp.concatenate(top_s, axis=0)
        te = jnp.concatenate(top_e, axis=0)
        e = jnp.exp(ts - ts[0:1])
        gate = e / jnp.sum(e, axis=0, keepdims=True)
        a = jnp.floor(te * (1.0 / N_KEYS))
        i1_ref[:, rows] = a
        i2_ref[:, rows] = te - a * N_KEYS
        gate_ref[:, rows] = gate


def _route(pq, sk1_bf, sk2_bf, tm):
    t = pq.shape[0]
    out = pl.BlockSpec((PEER_TOPK, tm), lambda i, h: (h, i))
    shp = jax.ShapeDtypeStruct((PEER_HEADS * PEER_TOPK, t), F32)
    return pl.pallas_call(
        _route_kernel,
        out_shape=(shp, shp, shp),
        grid=(t // tm, PEER_HEADS),
        in_specs=[pl.BlockSpec((tm, 2 * PEER_HALF), lambda i, h: (i, h)),
                  _resident(sk1_bf.shape), _resident(sk2_bf.shape)],
        out_specs=(out, out, out),
        compiler_params=_cparams(("parallel", "parallel")),
        name="peer_route",
    )(pq, sk1_bf, sk2_bf)


def _peer_kernel(h2_ref, x1_ref, i1_ref, i2_ref, gt_ref, p_ref, ed_ref, eu_ref,
                 g3_ref, wpg_ref, wpp_ref, y_ref, pk_ref, acc_ref, i1s_ref, i2s_ref, gts_ref):
    tm = h2_ref.shape[0]
    j = pl.program_id(1)
    half = N_KEYS // 2

    @pl.when(j == 0)
    def _build_gates():
        i1s_ref[...] = i1_ref[...].T
        i2s_ref[...] = i2_ref[...].T
        gts_ref[...] = gt_ref[...].T
        acc_ref[...] = jnp.zeros_like(acc_ref)
        sub = lax.broadcasted_iota(jnp.int32, (N_KEYS, N_KEYS), 0)
        b_of_row = sub.astype(F32)
        a_of_row = jnp.where(sub < half, 2 * sub, 2 * (sub - half) + 1).astype(F32)

        def body(t, carry):
            r1 = i1s_ref[pl.ds(t, 1), :]
            r2 = i2s_ref[pl.ds(t, 1), :]
            rg = gts_ref[pl.ds(t, 1), :]
            oat = jnp.where(a_of_row == r1, 1.0, 0.0).astype(BF16)
            gbt = jnp.where(b_of_row == r2, rg, 0.0).astype(BF16)
            g = _nt_dot(oat, gbt)
            lo = pltpu.bitcast(g[:half], jnp.uint32) + jnp.uint32(0x8000)
            hi = pltpu.bitcast(g[half:], jnp.uint32) + jnp.uint32(0x8000)
            packed = (lo >> 16) | (hi & jnp.uint32(0xFFFF0000))
            pk_ref[pl.ds(pl.multiple_of(t * G_PITCH, 8), half), :] = packed
            return carry

        lax.fori_loop(0, tm, body, 0, unroll=8)

    gs = []
    for r in range(PAIRS_PER_STEP):
        w = pk_ref[pl.ds(j * PAIRS_PER_STEP + r, tm, stride=G_PITCH), :]
        gs.append(pltpu.bitcast(w << 16, F32))
        gs.append(pltpu.bitcast(w & jnp.uint32(0xFFFF0000), F32))
    gates = jnp.concatenate(gs, axis=1)
    s = _nt_dot(h2_ref[...], ed_ref[...])
    act = (jax.nn.gelu(s) * gates).astype(BF16)
    acc_ref[...] += jnp.dot(act, eu_ref[...], preferred_element_type=F32)

    @pl.when(j == pl.num_programs(1) - 1)
    def _finish():
        x2 = x1_ref[...] + acc_ref[...]
        h3 = _rms(x2, g3_ref[...]).astype(BF16)
        gate = jax.nn.sigmoid(jnp.dot(h3, wpg_ref[...], preferred_element_type=F32))
        y_ref[...] = x2 + gate * jnp.dot(p_ref[...].astype(BF16), wpp_ref[...], preferred_element_type=F32)


def _peer_ple(h2, x1, i1, i2, gt, p2d, ed, eu, g3, wpg, wpp, tm):
    t = h2.shape[0]
    nb = PAIRS_PER_STEP * 2 * N_KEYS
    steps = ed.shape[0] // nb
    row = lambda w: pl.BlockSpec((tm, w), lambda i, j: (i, 0))
    pick = pl.BlockSpec((PEER_HEADS * PEER_TOPK, tm), lambda i, j: (0, i))
    table = pl.BlockSpec((nb, D_MODEL), lambda i, j: (j, 0))
    return pl.pallas_call(
        _peer_kernel,
        out_shape=jax.ShapeDtypeStruct((t, D_MODEL), F32),
        grid=(t // tm, steps),
        in_specs=[row(D_MODEL), row(D_MODEL), pick, pick, pick, row(PLE_DIM), table, table,
                  _resident(g3.shape), _resident(wpg.shape), _resident(wpp.shape)],
        out_specs=row(D_MODEL),
        scratch_shapes=[pltpu.VMEM((tm * G_PITCH, N_KEYS), jnp.uint32),
                        pltpu.VMEM((tm, D_MODEL), F32),
                        pltpu.VMEM((tm, N_KEYS), F32), pltpu.VMEM((tm, N_KEYS), F32), pltpu.VMEM((tm, N_KEYS), F32)],
        compiler_params=_cparams(("parallel", "arbitrary"), V7X_VMEM_LIMIT),
        name="peer_ple",
    )(h2, x1, i1, i2, gt, p2d, ed, eu, g3, wpg, wpp)


def _t5_bucket(dist):
    max_exact = N_BUCKETS // 2
    df = jnp.maximum(dist, max_exact).astype(F32)
    large = max_exact + (jnp.log(df / max_exact) / math.log(MAX_DISTANCE / max_exact)
                         * (N_BUCKETS - max_exact)).astype(jnp.int32)
    return jnp.where(dist < max_exact, dist, jnp.minimum(large, N_BUCKETS - 1))


def _bias_tables(rel_bias, g):
    dil = DILATIONS[g]
    hi = lax.Precision.HIGHEST
    bucket = _t5_bucket(jnp.arange(BAND + 1) * dil)
    pick = (bucket[:, None] == jnp.arange(N_BUCKETS)[None, :]).astype(F32)
    per_dist = jnp.dot(pick, rel_bias[:, g * HEADS:(g + 1) * HEADS].astype(F32), precision=hi)
    dsub = np.arange(BAND)[:, None] + BAND - np.arange(2 * BAND)[None, :]
    valid = (dsub >= 0) & (dsub <= WINDOWS[g] // dil)
    spread = (jnp.asarray(np.clip(dsub, 0, BAND).reshape(1, -1)) == jnp.arange(BAND + 1)[:, None]).astype(F32)
    band = jnp.dot(per_dist.T, spread, precision=hi).reshape(HEADS, BAND, 2 * BAND)
    add = jnp.where(jnp.asarray(valid)[None], band, NEG_INF)
    addc = per_dist[::-1][:BAND].T
    addn = per_dist[0][:, None]
    return add, addc, addn


def kernel(x_prompt, x_sample, cache_kv_w128, cache_kv_w512, cache_kv_w2048, p_prompt, p_sample, rel_bias, norm1,
           w_in, q_norm, k_norm, b_v_norm, w_spatial, b_spatial, w_up_a, w_up_b, w_out, norm2, w_query,
           sub_keys_1, sub_keys_2, expert_down, expert_up, norm3, w_ple_gate, w_ple_proj):
    bsz, seq, _ = x_prompt.shape
    db, ds, _ = x_sample.shape
    depth = norm1.shape[0]
    assert depth == 1 and ds == 1 and seq % (BAND * DILATIONS[-1]) == 0 and seq >= WINDOWS[-1]
    caches = (cache_kv_w128, cache_kv_w512, cache_kv_w2048)
    for g in range(N_GROUPS):
        assert caches[g].shape[2] == WINDOWS[g]
    t_p = bsz * seq
    t_s = 128
    li = 0

    g1 = norm1[li][None, :]
    w_in_bf = w_in[li].astype(BF16)
    qg = jnp.tile(q_norm[li], (1, HEADS))
    kg = jnp.tile(k_norm[li], (1, HEADS))
    bvg = b_v_norm[li][None, :]
    blk = np.arange(V7X_MXU_DIM) // HEAD_DIM
    hsum = jnp.asarray(blk[:, None] == blk[None, :], dtype=BF16)
    wua, wub, wo = w_up_a[li].astype(BF16), w_up_b[li].astype(BF16), w_out[li].astype(BF16)
    g2, g3 = norm2[li][None, :], norm3[li][None, :]
    wq = w_query[li].astype(BF16)
    sk1, sk2 = sub_keys_1[li].astype(BF16), sub_keys_2[li].astype(BF16)
    ed = expert_down[li].astype(BF16)
    eu = expert_up[li].astype(BF16)
    wpg, wpp = w_ple_gate[li].astype(BF16), w_ple_proj[li].astype(BF16)
    bs_full = jnp.repeat(b_spatial[li].T, D_MODEL // B_GROUPS, axis=1)
    tables = [_bias_tables(rel_bias, g) for g in range(N_GROUPS)]

    res = _in_projection_prompt(x_prompt, g1, w_in_bf, qg, kg, bvg, hsum, TM_PROJ)
    qs, ks, vs, tails = res[0:3], res[3:6], res[6:9], res[9:12]
    ub, vb, ga, gb = res[12:16]
    outs, lses, kv_prompt = [], [], []
    for g in range(N_GROUPS):
        o, lse = _prompt_attention(qs[g], ks[g], vs[g], tables[g][0], g)
        outs.append(o)
        lses.append(lse)
        kv_prompt.append(tails[g].reshape(1, bsz, tails[g].shape[1], 2, HEADS, HEAD_DIM))
    x1, h2, pq = _merge_prompt(x_prompt, outs, lses, ub, vb, ga, gb, w_spatial[li], bs_full, wua, wub, wo, g2, wq,
                               TM_PROJ)
    x1, h2, pq = x1.reshape(t_p, D_MODEL), h2.reshape(t_p, D_MODEL), pq.reshape(t_p, -1)
    i1, i2, gt = _route(pq, sk1, sk2, TM_PROJ)
    y_prompt = _peer_ple(h2, x1, i1, i2, gt, p_prompt[li].reshape(t_p, PLE_DIM), ed, eu, g3, wpg, wpp, TM_PEER)
    y_prompt = y_prompt.reshape(bsz, seq, D_MODEL)

    xs = jnp.pad(x_sample.reshape(db, D_MODEL), ((0, t_s - db), (0, 0)))
    q, k, v, ub, vb, ga, gb = _in_projection_sample(xs, g1, w_in_bf, qg, kg, bvg, hsum)
    nh = N_GROUPS * HEADS
    q3, k3, v3 = (a[:db].astype(F32).reshape(db, nh, HEAD_DIM) for a in (q, k, v))
    cache_views = [caches[g][li].reshape(db, BAND, DILATIONS[g], 2, HEADS, HEAD_DIM) for g in range(N_GROUPS)]
    addc = jnp.stack([tables[g][1].T[:, :, None] for g in range(N_GROUPS)])
    addn = jnp.stack([tables[g][2] for g in range(N_GROUPS)])
    o3, lse3 = _sample_attention(q3, k3, v3, cache_views, addc, addn)
    pad = lambda a: jnp.pad(a, ((0, t_s - db), (0, 0)))
    o2, lse2 = o3.reshape(db, QKV_W), lse3.reshape(db, QKV_W)
    outs = [pad(o2[:, g * GROUP_W:(g + 1) * GROUP_W]) for g in range(N_GROUPS)]
    lses = [pad(lse2[:, g * GROUP_W:(g + 1) * GROUP_W]) for g in range(N_GROUPS)]
    kv_sample = [jnp.stack([k3[:, None, g * HEADS:(g + 1) * HEADS], v3[:, None, g * HEADS:(g + 1) * HEADS]],
                           axis=2)[None] for g in range(N_GROUPS)]
    ws0 = jnp.repeat(w_spatial[li][:, 0, 0], D_MODEL // B_GROUPS)[None, :]
    bs0 = bs_full[0:1, :]
    x1, h2, pq = _merge_sample(xs, outs, lses, ub, vb, ga, gb, ws0, bs0, wua, wub, wo, g2, wq)
    i1, i2, gt = _route(pq, sk1, sk2, t_s)
    ps = jnp.pad(p_sample[li].reshape(db, PLE_DIM), ((0, t_s - db), (0, 0)))
    y_sample = _peer_ple(h2, x1, i1, i2, gt, ps, ed, eu, g3, wpg, wpp, t_s)[:db].reshape(db, ds, D_MODEL)
    v_sample_chunk = vb[:db].reshape(1, db, ds, D_MODEL)

    return (y_prompt, y_sample, kv_prompt[0], kv_prompt[1], kv_prompt[2],
            kv_sample[0], kv_sample[1], kv_sample[2], v_sample_chunk)
```

```python
import functools
import math

import numpy as np
import jax
import jax.numpy as jnp
from jax import lax
from jax.experimental import pallas as pl
from jax.experimental.pallas import tpu as pltpu

F32 = jnp.float32
BF16 = jnp.bfloat16

D_MODEL = 1024
N_GROUPS = 3
HEADS = 8
HEAD_DIM = 64
GROUP_W = HEADS * HEAD_DIM
QKV_W = N_GROUPS * GROUP_W
WINDOWS = (128, 512, 2048)
DILATIONS = (1, 4, 16)
BAND = 128
N_BUCKETS = 32
MAX_DISTANCE = 2048
CHUNK = 128
B_GROUPS = 4
N_KEYS = 128
PEER_HEADS = 8
PEER_TOPK = 16
PEER_HALF = 128
PLE_DIM = 256
EPS = 1e-6
NEG_INF = -1e30
IN_OFFS = (0, QKV_W, 2 * QKV_W, 3 * QKV_W, 3 * QKV_W + D_MODEL, 3 * QKV_W + 2 * D_MODEL,
           3 * QKV_W + 3 * D_MODEL, 3 * QKV_W + 4 * D_MODEL)

LANES = 128
SLABS = GROUP_W // LANES
V7X_VMEM_LIMIT = 56 * 1024 * 1024
V7X_MXU_DIM = 256
G_PITCH = 72
TM_PROJ = 256
TM_PEER = 512
PAIRS_PER_STEP = 4

def _cparams(sem, vmem=None):
    return pltpu.CompilerParams(dimension_semantics=sem, vmem_limit_bytes=vmem)


def _resident(shape):
    nd = len(shape)
    return pl.BlockSpec(shape, lambda *_: (0,) * nd)


def _nt_dot(a, b):
    return lax.dot_general(a, b, (((1,), (1,)), ((), ())), preferred_element_type=F32)


def _rms(x, gain):
    return x * lax.rsqrt(jnp.mean(x * x, axis=-1, keepdims=True) + EPS) * gain


def _head_norm(y, hsum_ref, gain_row):
    y2 = (y * y).astype(BF16)
    w = hsum_ref.shape[0]
    ss = jnp.concatenate([jnp.dot(y2[:, c:c + w], hsum_ref[...], preferred_element_type=F32)
                          for c in range(0, GROUP_W, w)], axis=1)
    return y * lax.rsqrt(ss * (1.0 / HEAD_DIM) + EPS) * gain_row


def _store_residue_major(out_ref, y, scr_ref, dil):
    if dil == 1:
        out_ref[0, 0] = y.astype(out_ref.dtype)
        return
    n = y.shape[0] // dil
    for s in range(SLABS):
        scr_ref[s] = y[:, s * LANES:(s + 1) * LANES]
    for r in range(dil):
        for s in range(SLABS):
            out_ref[0, r, :, s * LANES:(s + 1) * LANES] = scr_ref[s, pl.ds(r, n, stride=dil), :].astype(out_ref.dtype)


def _gmlp_and_gates(proj, bvg_ref, ub_ref, vb_ref, ga_ref, gb_ref, st):
    half = D_MODEL // 2
    for c in range(2):
        st(ub_ref, c * half, jax.nn.gelu(proj(IN_OFFS[3] + c * half, half)))
    vb0 = jax.nn.gelu(proj(IN_OFFS[4], half))
    vb1 = jax.nn.gelu(proj(IN_OFFS[4] + half, half))
    ms = (jnp.sum(vb0 * vb0, axis=-1, keepdims=True) + jnp.sum(vb1 * vb1, axis=-1, keepdims=True)) * (1.0 / D_MODEL)
    r = lax.rsqrt(ms + EPS)
    st(vb_ref, 0, vb0 * r * bvg_ref[:, :half])
    st(vb_ref, half, vb1 * r * bvg_ref[:, half:])
    for c in range(2):
        st(ga_ref, c * half, proj(IN_OFFS[5] + c * half, half))
        st(gb_ref, c * half, proj(IN_OFFS[6] + c * half, half))


def _inproj_prompt_kernel(x_ref, g1_ref, w_ref, qg_ref, kg_ref, bvg_ref, hsum_ref,
                          q0_ref, q1_ref, q2_ref, k0_ref, k1_ref, k2_ref, v0_ref, v1_ref, v2_ref,
                          t0_ref, t1_ref, t2_ref, ub_ref, vb_ref, ga_ref, gb_ref, scr_ref, *, tail_first):
    ti = pl.program_id(1)
    tm = x_ref.shape[1]
    hb = _rms(x_ref[0], g1_ref[...]).astype(BF16)

    def proj(c0, width):
        return jnp.dot(hb, w_ref[:, c0:c0 + width], preferred_element_type=F32)

    q_refs, k_refs, v_refs, t_refs = (q0_ref, q1_ref, q2_ref), (k0_ref, k1_ref, k2_ref), (v0_ref, v1_ref, v2_ref), \
        (t0_ref, t1_ref, t2_ref)
    for g in range(N_GROUPS):
        lo = g * GROUP_W
        dil = DILATIONS[g]
        qn = _head_norm(proj(IN_OFFS[0] + lo, GROUP_W), hsum_ref, qg_ref[g:g + 1, :]) * (HEAD_DIM ** -0.5)
        _store_residue_major(q_refs[g], qn, scr_ref, dil)
        kn = _head_norm(proj(IN_OFFS[1] + lo, GROUP_W), hsum_ref, kg_ref[g:g + 1, :])
        _store_residue_major(k_refs[g], kn, scr_ref, dil)
        vv = proj(IN_OFFS[2] + lo, GROUP_W)
        _store_residue_major(v_refs[g], vv, scr_ref, dil)
        rows = t_refs[g].shape[1]

        @pl.when(ti >= tail_first[g])
        def _():
            t_refs[g][0, :, :GROUP_W] = kn[tm - rows:, :]
            t_refs[g][0, :, GROUP_W:] = vv[tm - rows:, :]

    def st(ref, c0, val):
        ref[0, :, c0:c0 + val.shape[1]] = val.astype(ref.dtype)

    _gmlp_and_gates(proj, bvg_ref, ub_ref, vb_ref, ga_ref, gb_ref, st)


def _inproj_sample_kernel(x_ref, g1_ref, w_ref, qg_ref, kg_ref, bvg_ref, hsum_ref,
                          q_ref, k_ref, v_ref, ub_ref, vb_ref, ga_ref, gb_ref):
    hb = _rms(x_ref[...], g1_ref[...]).astype(BF16)

    def proj(c0, width):
        return jnp.dot(hb, w_ref[:, c0:c0 + width], preferred_element_type=F32)

    for g in range(N_GROUPS):
        lo = g * GROUP_W
        qn = _head_norm(proj(IN_OFFS[0] + lo, GROUP_W), hsum_ref, qg_ref[g:g + 1, :]) * (HEAD_DIM ** -0.5)
        q_ref[:, lo:lo + GROUP_W] = qn.astype(q_ref.dtype)
        k_ref[:, lo:lo + GROUP_W] = _head_norm(proj(IN_OFFS[1] + lo, GROUP_W), hsum_ref, kg_ref[g:g + 1, :])
        v_ref[:, lo:lo + GROUP_W] = proj(IN_OFFS[2] + lo, GROUP_W)

    def st(ref, c0, val):
        ref[:, c0:c0 + val.shape[1]] = val.astype(ref.dtype)

    _gmlp_and_gates(proj, bvg_ref, ub_ref, vb_ref, ga_ref, gb_ref, st)


def _proj_weight_specs(w_in_bf):
    return [_resident((1, D_MODEL)),
            pl.BlockSpec(w_in_bf.shape, lambda *_: (0, 0), pipeline_mode=pl.Buffered(1)),
            _resident((N_GROUPS, GROUP_W)), _resident((N_GROUPS, GROUP_W)), _resident((1, D_MODEL)),
            _resident((V7X_MXU_DIM, V7X_MXU_DIM))]


def _in_projection_prompt(x, g1, w_in_bf, qg, kg, bvg, hsum, tm):
    b, s, _ = x.shape
    nt = s // tm
    row = pl.BlockSpec((1, tm, D_MODEL), lambda bi, ti: (bi, ti, 0))
    qkv_shapes, qkv_specs, tail_shapes, tail_specs, tail_first = [], [], [], [], []
    for g in range(N_GROUPS):
        dil = DILATIONS[g]
        qkv_shapes.append(jax.ShapeDtypeStruct((b, dil, s // dil, GROUP_W), BF16))
        qkv_specs.append(pl.BlockSpec((1, dil, tm // dil, GROUP_W), lambda bi, ti: (bi, 0, ti, 0)))
        ln = min(WINDOWS[g], s)
        rows = min(ln, tm)
        first = (s - ln) // tm if ln >= tm else nt - 1
        tail_first.append(first)
        tail_shapes.append(jax.ShapeDtypeStruct((b, ln, 2 * GROUP_W), F32))
        tail_specs.append(pl.BlockSpec((1, rows, 2 * GROUP_W),
                                       functools.partial(lambda bi, ti, f: (bi, jnp.maximum(ti - f, 0), 0), f=first)))
    act = jax.ShapeDtypeStruct((b, s, D_MODEL), BF16)
    gate = jax.ShapeDtypeStruct((b, s, D_MODEL), F32)
    return pl.pallas_call(
        functools.partial(_inproj_prompt_kernel, tail_first=tuple(tail_first)),
        out_shape=tuple(qkv_shapes * 3 + tail_shapes + [act, act, gate, gate]),
        grid=(b, nt),
        in_specs=[row] + _proj_weight_specs(w_in_bf),
        out_specs=tuple(qkv_specs * 3 + tail_specs + [row] * 4),
        scratch_shapes=[pltpu.VMEM((SLABS, tm, LANES), F32)],
        compiler_params=_cparams(("arbitrary", "arbitrary"), V7X_VMEM_LIMIT),
        name="in_projection",
    )(x, g1, w_in_bf, qg, kg, bvg, hsum)


def _in_projection_sample(x2d, g1, w_in_bf, qg, kg, bvg, hsum):
    t = x2d.shape[0]
    row = lambda w: pl.BlockSpec((t, w), lambda i: (0, 0))
    outs = (jax.ShapeDtypeStruct((t, QKV_W), BF16), jax.ShapeDtypeStruct((t, QKV_W), F32),
            jax.ShapeDtypeStruct((t, QKV_W), F32), jax.ShapeDtypeStruct((t, D_MODEL), BF16),
            jax.ShapeDtypeStruct((t, D_MODEL), F32), jax.ShapeDtypeStruct((t, D_MODEL), F32),
            jax.ShapeDtypeStruct((t, D_MODEL), F32))
    return pl.pallas_call(
        _inproj_sample_kernel,
        out_shape=outs,
        grid=(1,),
        in_specs=[row(D_MODEL)] + _proj_weight_specs(w_in_bf),
        out_specs=(row(QKV_W), row(QKV_W), row(QKV_W), row(D_MODEL), row(D_MODEL), row(D_MODEL), row(D_MODEL)),
        compiler_params=_cparams(("arbitrary",), V7X_VMEM_LIMIT),
        name="in_projection_sample",
    )(x2d, g1, w_in_bf, qg, kg, bvg, hsum)


def _attn_kernel(q_ref, kp_ref, kc_ref, vp_ref, vc_ref, add_ref, o_ref, lse_ref):
    i = pl.program_id(2)
    q = q_ref[0, 0]
    kk = jnp.concatenate([kp_ref[0, 0], kc_ref[0, 0]], axis=0)
    vv = jnp.concatenate([vp_ref[0, 0], vc_ref[0, 0]], axis=0)
    lane = lax.broadcasted_iota(jnp.int32, (BAND, LANES), 1)
    kcol = lax.broadcasted_iota(jnp.int32, (BAND, 2 * BAND), 1)
    no_prev = kcol < jnp.where(i == 0, BAND, 0)
    for pair in range(HEADS // 2):
        sl = slice(pair * LANES, (pair + 1) * LANES)
        qp, kp, vp = q[:, sl].astype(F32), kk[:, sl], vv[:, sl]
        o_pair = jnp.zeros((BAND, LANES), F32)
        l_pair = jnp.zeros((BAND, LANES), F32)
        for hh in range(2):
            mine = (lane >= HEAD_DIM) if hh else (lane < HEAD_DIM)
            s = _nt_dot(jnp.where(mine, qp, 0.0).astype(BF16), kp) + add_ref[pair * 2 + hh]
            s = jnp.where(no_prev, NEG_INF, s)
            m = jnp.max(s, axis=-1, keepdims=True)
            p = jnp.exp(s - m)
            den = jnp.sum(p, axis=-1, keepdims=True)
            o = jnp.dot(p.astype(BF16), vp, preferred_element_type=F32) / den
            o_pair = jnp.where(mine, o, o_pair)
            l_pair = jnp.where(mine, m + jnp.log(den), l_pair)
        o_ref[0, 0, :, sl] = o_pair.astype(o_ref.dtype)
        lse_ref[0, 0, :, sl] = l_pair


def _prompt_attention(q4, k4, v4, add, g):
    b, dil, n, _ = q4.shape
    cur = pl.BlockSpec((1, 1, BAND, GROUP_W), lambda bi, r, i: (bi, r, i, 0))
    prev = pl.BlockSpec((1, 1, BAND, GROUP_W), lambda bi, r, i: (bi, r, jnp.maximum(i - 1, 0), 0))
    return pl.pallas_call(
        _attn_kernel,
        out_shape=(jax.ShapeDtypeStruct(q4.shape, BF16), jax.ShapeDtypeStruct(q4.shape, F32)),
        grid=(b, dil, n // BAND),
        in_specs=[cur, prev, cur, prev, cur, _resident(add.shape)],
        out_specs=(cur, cur),
        compiler_params=_cparams(("parallel", "parallel", "arbitrary")),
        name=f"attn_g{g}",
    )(q4, k4, k4, v4, v4, add)


def _sample_attn_kernel(q_ref, kn_ref, vn_ref, c0_ref, c1_ref, c2_ref, a0_ref, a1_ref, a2_ref, addn_ref,
                        o_ref, lse_ref):
    rounded = lambda a: a.astype(BF16).astype(F32)
    head_of_lane = lax.broadcasted_iota(jnp.int32, (HEADS, GROUP_W), 1) // HEAD_DIM
    mine = head_of_lane == lax.broadcasted_iota(jnp.int32, (HEADS, GROUP_W), 0)
    for g, (c_ref, a_ref) in enumerate(((c0_ref, a0_ref), (c1_ref, a1_ref), (c2_ref, a2_ref))):
        cols = slice(g * GROUP_W, (g + 1) * GROUP_W)
        ln = c_ref.shape[-1]
        qm = jnp.where(mine, jnp.broadcast_to(q_ref[0, :, cols], (HEADS, GROUP_W)), 0.0)
        kt = c_ref[0].reshape(GROUP_W, ln).astype(BF16)
        vt = c_ref[1].reshape(GROUP_W, ln).astype(BF16)
        kn, vn = rounded(kn_ref[0, :, cols]), rounded(vn_ref[0, :, cols])
        s_c = jnp.dot(qm.astype(BF16), kt, preferred_element_type=F32) + a_ref[...]
        s_n = jnp.sum(qm * kn, axis=-1, keepdims=True) + addn_ref[g]
        m = jnp.maximum(jnp.max(s_c, axis=-1, keepdims=True), s_n)
        p_c = jnp.exp(s_c - m)
        p_n = jnp.exp(s_n - m)
        den = jnp.sum(p_c, axis=-1, keepdims=True) + p_n
        o_all = (_nt_dot(p_c.astype(BF16), vt) + rounded(p_n) * vn) / den
        lse = jnp.broadcast_to(m + jnp.log(den), (HEADS, GROUP_W))
        o_ref[0, :, cols] = jnp.sum(jnp.where(mine, o_all, 0.0), axis=0, keepdims=True)
        lse_ref[0, :, cols] = jnp.sum(jnp.where(mine, lse, 0.0), axis=0, keepdims=True)


def _sample_attention(q_s, k_s, v_s, caches_t, adds, addn):
    db = q_s.shape[0]
    tok = pl.BlockSpec((1, 1, QKV_W), lambda bi: (bi, 0, 0))
    cache_specs = [pl.BlockSpec((None,) + c.shape[1:], lambda bi: (bi, 0, 0, 0, 0)) for c in caches_t]
    shp = jax.ShapeDtypeStruct(q_s.shape, F32)
    return pl.pallas_call(
        _sample_attn_kernel,
        out_shape=(shp, shp),
        grid=(db,),
        in_specs=[tok, tok, tok] + cache_specs + [_resident(a.shape) for a in adds] + [_resident(addn.shape)],
        out_specs=(tok, tok),
        compiler_params=_cparams(("parallel",), V7X_VMEM_LIMIT),
        name="sample_attn",
    )(q_s, k_s, v_s, *caches_t, *adds, addn)


def _load_token_major(ref, scr_ref, dil):
    if dil == 1:
        return ref[0, 0].astype(F32)
    n = ref.shape[2]
    for r in range(dil):
        blk = ref[0, r].astype(F32)
        for s in range(SLABS):
            scr_ref[s, pl.ds(r, n, stride=dil), :] = blk[:, s * LANES:(s + 1) * LANES]
    return jnp.concatenate([scr_ref[s] for s in range(SLABS)], axis=1)


def _merge_tail(x, a_out, b_out, ga, gb, wua_ref, wub_ref, wo_ref, g2_ref, wq_ref):
    up_a = jnp.dot(a_out.astype(BF16), wua_ref[...], preferred_element_type=F32)
    up_b = jnp.dot(b_out.astype(BF16), wub_ref[...], preferred_element_type=F32)
    mrg = jax.nn.sigmoid(ga) * up_a + jax.nn.sigmoid(gb) * up_b
    x1 = x + jnp.dot(mrg.astype(BF16), wo_ref[...], preferred_element_type=F32)
    h2 = _rms(x1, g2_ref[...]).astype(BF16)
    return x1, h2, jnp.dot(h2, wq_ref[...], preferred_element_type=F32)


def _combine_groups(os, ls):
    m = jnp.maximum(jnp.maximum(ls[0], ls[1]), ls[2])
    es = [jnp.exp(l - m) for l in ls]
    return (es[0] * os[0] + es[1] * os[1] + es[2] * os[2]) / (es[0] + es[1] + es[2])


def _merge_prompt_kernel(x_ref, o0_ref, o1_ref, o2_ref, l0_ref, l1_ref, l2_ref, ub_ref, vb_ref, ga_ref, gb_ref,
                         ws_ref, bs_ref, wua_ref, wub_ref, wo_ref, g2_ref, wq_ref,
                         x1_ref, h2_ref, pq_ref, scr_ref):
    tm = x_ref.shape[1]
    os = [_load_token_major(r, scr_ref, DILATIONS[g]) for g, r in enumerate((o0_ref, o1_ref, o2_ref))]
    ls = [_load_token_major(r, scr_ref, DILATIONS[g]) for g, r in enumerate((l0_ref, l1_ref, l2_ref))]
    a_out = _combine_groups(os, ls)

    gw = D_MODEL // B_GROUPS
    tri = lax.broadcasted_iota(jnp.int32, (CHUNK, CHUNK), 0) >= lax.broadcasted_iota(jnp.int32, (CHUNK, CHUNK), 1)
    parts = []
    for c in range(tm // CHUNK):
        rows = slice(c * CHUNK, (c + 1) * CHUNK)
        cols = []
        for g in range(B_GROUPS):
            w = jnp.where(tri, ws_ref[g], 0.0).astype(BF16)
            cols.append(jnp.dot(w, vb_ref[0, rows, g * gw:(g + 1) * gw], preferred_element_type=F32))
        mixed = jnp.concatenate(cols, axis=1) + bs_ref[...]
        parts.append(ub_ref[0, rows, :].astype(F32) * mixed)
    b_out = jnp.concatenate(parts, axis=0)

    x1, h2, pq = _merge_tail(x_ref[0], a_out, b_out, ga_ref[0], gb_ref[0], wua_ref, wub_ref, wo_ref, g2_ref, wq_ref)
    x1_ref[0] = x1
    h2_ref[0] = h2
    pq_ref[0] = pq.astype(pq_ref.dtype)


def _merge_sample_kernel(x_ref, o0_ref, o1_ref, o2_ref, l0_ref, l1_ref, l2_ref, ub_ref, vb_ref, ga_ref, gb_ref,
                         ws_ref, bs_ref, wua_ref, wub_ref, wo_ref, g2_ref, wq_ref, x1_ref, h2_ref, pq_ref):
    a_out = _combine_groups([o0_ref[...], o1_ref[...], o2_ref[...]], [l0_ref[...], l1_ref[...], l2_ref[...]])
    b_out = ub_ref[...].astype(F32) * (vb_ref[...] * ws_ref[...] + bs_ref[...])
    x1, h2, pq = _merge_tail(x_ref[...], a_out, b_out, ga_ref[...], gb_ref[...], wua_ref, wub_ref, wo_ref, g2_ref, wq_ref)
    x1_ref[...] = x1
    h2_ref[...] = h2
    pq_ref[...] = pq.astype(pq_ref.dtype)


def _merge_prompt(x, o, lse, ub, vb, ga, gb, ws, bs, wua, wub, wo, g2, wq, tm):
    b, s, _ = x.shape
    row = lambda w: pl.BlockSpec((1, tm, w), lambda bi, ti: (bi, ti, 0))
    dilated = [pl.BlockSpec((1, d, tm // d, GROUP_W), lambda bi, ti: (bi, 0, ti, 0)) for d in DILATIONS]
    qw = wq.shape[1]
    weights = (ws, bs, wua, wub, wo, g2, wq)
    return pl.pallas_call(
        _merge_prompt_kernel,
        out_shape=(jax.ShapeDtypeStruct((b, s, D_MODEL), F32), jax.ShapeDtypeStruct((b, s, D_MODEL), BF16),
                   jax.ShapeDtypeStruct((b, s, qw), BF16)),
        grid=(b, s // tm),
        in_specs=[row(D_MODEL)] + dilated + dilated + [row(D_MODEL)] * 4 + [_resident(w.shape) for w in weights],
        out_specs=(row(D_MODEL), row(D_MODEL), row(qw)),
        scratch_shapes=[pltpu.VMEM((SLABS, tm, LANES), F32)],
        compiler_params=_cparams(("parallel", "parallel"), V7X_VMEM_LIMIT),
        name="merge",
    )(x, *o, *lse, ub, vb, ga, gb, *weights)


def _merge_sample(x2d, o, lse, ub, vb, ga, gb, ws0, bs0, wua, wub, wo, g2, wq):
    t = x2d.shape[0]
    row = lambda w: pl.BlockSpec((t, w), lambda i: (0, 0))
    qw = wq.shape[1]
    weights = (ws0, bs0, wua, wub, wo, g2, wq)
    return pl.pallas_call(
        _merge_sample_kernel,
        out_shape=(jax.ShapeDtypeStruct((t, D_MODEL), F32), jax.ShapeDtypeStruct((t, D_MODEL), BF16),
                   jax.ShapeDtypeStruct((t, qw), BF16)),
        grid=(1,),
        in_specs=[row(D_MODEL)] + [row(GROUP_W)] * 6 + [row(D_MODEL)] * 4 + [_resident(w.shape) for w in weights],
        out_specs=(row(D_MODEL), row(D_MODEL), row(qw)),
        compiler_params=_cparams(("arbitrary",), V7X_VMEM_LIMIT),
        name="merge_sample",
    )(x2d, *o, *lse, ub, vb, ga, gb, *weights)


def _top16_rows(s, idx_col):
    vals, idxs = [], []
    big = jnp.float32(1e9)
    for _ in range(PEER_TOPK):
        m = jnp.max(s, axis=0, keepdims=True)
        sel = jnp.min(jnp.where(s == m, idx_col, big), axis=0, keepdims=True)
        s = jnp.where(idx_col == sel, -jnp.inf, s)
        vals.append(m)
        idxs.append(sel)
    return vals, idxs


def _route_kernel(pq_ref, sk1_ref, sk2_ref, i1_ref, i2_ref, gate_ref):
    key_col = lax.broadcasted_iota(jnp.int32, (N_KEYS, LANES), 0).astype(F32)
    row8 = lax.broadcasted_iota(jnp.int32, (8, LANES), 0).astype(F32)
    row16 = lax.broadcasted_iota(jnp.int32, (PEER_TOPK, LANES), 0).astype(F32)
    for c in range(pq_ref.shape[0] // LANES):
        rows = slice(c * LANES, (c + 1) * LANES)
        qh = pq_ref[rows, :]
        s1 = _nt_dot(sk1_ref[...], qh[:, :PEER_HALF])
        s2 = _nt_dot(sk2_ref[...], qh[:, PEER_HALF:])
        v1, i1 = _top16_rows(s1, key_col)
        v2, i2 = _top16_rows(s2, key_col)
        v2a = jnp.concatenate(v2, axis=0)
        i2a = jnp.concatenate(i2, axis=0)
        cand = [v1[0] + v2a]
        flat = [row16]
        code = [i1[0] * N_KEYS + i2a]
        for j1 in range(1, 8):
            cand.append(v1[j1] + v2a[:8])
            flat.append(row8 + float(j1 * PEER_TOPK))
            code.append(i1[j1] * N_KEYS + i2a[:8])
        cand.append(jnp.concatenate(v1[8:], axis=0) + v2[0])
        flat.append((row8 + 8.0) * float(PEER_TOPK))
        code.append(jnp.concatenate(i1[8:], axis=0) * N_KEYS + i2[0])
        cand = jnp.concatenate(cand, axis=0)
        flat = jnp.concatenate(flat, axis=0)
        code = jnp.concatenate(code, axis=0)
        top_s, top_e = [], []
        for _ in range(PEER_TOPK):
            m = jnp.max(cand, axis=0, keepdims=True)
            sel = jnp.min(jnp.where(cand == m, flat, 1e9), axis=0, keepdims=True)
            hit = flat == sel
            top_e.append(jnp.max(jnp.where(hit, code, -1.0), axis=0, keepdims=True))
            cand = jnp.where(hit, -jnp.inf, cand)
            top_s.append(m)
        ts = jnp.concatenate(top_s, axis=0)
        te = jnp.concatenate(top_e, axis=0)
        e = jnp.exp(ts - ts[0:1])
        gate = e / jnp.sum(e, axis=0, keepdims=True)
        a = jnp.floor(te * (1.0 / N_KEYS))
        i1_ref[:, rows] = a
        i2_ref[:, rows] = te - a * N_KEYS
        gate_ref[:, rows] = gate


def _route(pq, sk1_bf, sk2_bf, tm):
    t = pq.shape[0]
    out = pl.BlockSpec((PEER_TOPK, tm), lambda i, h: (h, i))
    shp = jax.ShapeDtypeStruct((PEER_HEADS * PEER_TOPK, t), F32)
    return pl.pallas_call(
        _route_kernel,
        out_shape=(shp, shp, shp),
        grid=(t // tm, PEER_HEADS),
        in_specs=[pl.BlockSpec((tm, 2 * PEER_HALF), lambda i, h: (i, h)),
                  _resident(sk1_bf.shape), _resident(sk2_bf.shape)],
        out_specs=(out, out, out),
        compiler_params=_cparams(("parallel", "parallel")),
        name="peer_route",
    )(pq, sk1_bf, sk2_bf)


def _peer_kernel(h2_ref, x1_ref, i1_ref, i2_ref, gt_ref, p_ref, ed_ref, eu_ref,
                 g3_ref, wpg_ref, wpp_ref, y_ref, pk_ref, acc_ref, i1s_ref, i2s_ref, gts_ref):
    tm = h2_ref.shape[0]
    j = pl.program_id(1)
    half = N_KEYS // 2

    @pl.when(j == 0)
    def _build_gates():
        i1s_ref[...] = i1_ref[...].T
        i2s_ref[...] = i2_ref[...].T
        gts_ref[...] = gt_ref[...].T
        acc_ref[...] = jnp.zeros_like(acc_ref)
        sub = lax.broadcasted_iota(jnp.int32, (N_KEYS, N_KEYS), 0)
        b_of_row = sub.astype(F32)
        a_of_row = jnp.where(sub < half, 2 * sub, 2 * (sub - half) + 1).astype(F32)

        def body(t, carry):
            r1 = i1s_ref[pl.ds(t, 1), :]
            r2 = i2s_ref[pl.ds(t, 1), :]
            rg = gts_ref[pl.ds(t, 1), :]
            oat = jnp.where(a_of_row == r1, 1.0, 0.0).astype(BF16)
            gbt = jnp.where(b_of_row == r2, rg, 0.0).astype(BF16)
            g = _nt_dot(oat, gbt)
            lo = pltpu.bitcast(g[:half], jnp.uint32) + jnp.uint32(0x8000)
            hi = pltpu.bitcast(g[half:], jnp.uint32) + jnp.uint32(0x8000)
            packed = (lo >> 16) | (hi & jnp.uint32(0xFFFF0000))
            pk_ref[pl.ds(pl.multiple_of(t * G_PITCH, 8), half), :] = packed
            return carry

        lax.fori_loop(0, tm, body, 0, unroll=8)

    gs = []
    for r in range(PAIRS_PER_STEP):
        w = pk_ref[pl.ds(j * PAIRS_PER_STEP + r, tm, stride=G_PITCH), :]
        gs.append(pltpu.bitcast(w << 16, F32))
        gs.append(pltpu.bitcast(w & jnp.uint32(0xFFFF0000), F32))
    gates = jnp.concatenate(gs, axis=1)
    s = _nt_dot(h2_ref[...], ed_ref[...])
    act = (jax.nn.gelu(s) * gates).astype(BF16)
    acc_ref[...] += jnp.dot(act, eu_ref[...], preferred_element_type=F32)

    @pl.when(j == pl.num_programs(1) - 1)
    def _finish():
        x2 = x1_ref[...] + acc_ref[...]
        h3 = _rms(x2, g3_ref[...]).astype(BF16)
        gate = jax.nn.sigmoid(jnp.dot(h3, wpg_ref[...], preferred_element_type=F32))
        y_ref[...] = x2 + gate * jnp.dot(p_ref[...].astype(BF16), wpp_ref[...], preferred_element_type=F32)


def _peer_ple(h2, x1, i1, i2, gt, p2d, ed, eu, g3, wpg, wpp, tm):
    t = h2.shape[0]
    nb = PAIRS_PER_STEP * 2 * N_KEYS
    steps = ed.shape[0] // nb
    row = lambda w: pl.BlockSpec((tm, w), lambda i, j: (i, 0))
    pick = pl.BlockSpec((PEER_HEADS * PEER_TOPK, tm), lambda i, j: (0, i))
    table = pl.BlockSpec((nb, D_MODEL), lambda i, j: (j, 0))
    return pl.pallas_call(
        _peer_kernel,
        out_shape=jax.ShapeDtypeStruct((t, D_MODEL), F32),
        grid=(t // tm, steps),
        in_specs=[row(D_MODEL), row(D_MODEL), pick, pick, pick, row(PLE_DIM), table, table,
                  _resident(g3.shape), _resident(wpg.shape), _resident(wpp.shape)],
        out_specs=row(D_MODEL),
        scratch_shapes=[pltpu.VMEM((tm * G_PITCH, N_KEYS), jnp.uint32),
                        pltpu.VMEM((tm, D_MODEL), F32),
                        pltpu.VMEM((tm, N_KEYS), F32), pltpu.VMEM((tm, N_KEYS), F32), pltpu.VMEM((tm, N_KEYS), F32)],
        compiler_params=_cparams(("parallel", "arbitrary"), V7X_VMEM_LIMIT),
        name="peer_ple",
    )(h2, x1, i1, i2, gt, p2d, ed, eu, g3, wpg, wpp)


def _t5_bucket(dist):
    max_exact = N_BUCKETS // 2
    df = jnp.maximum(dist, max_exact).astype(F32)
    large = max_exact + (jnp.log(df / max_exact) / math.log(MAX_DISTANCE / max_exact)
                         * (N_BUCKETS - max_exact)).astype(jnp.int32)
    return jnp.where(dist < max_exact, dist, jnp.minimum(large, N_BUCKETS - 1))


def _bias_tables(rel_bias, g):
    dil = DILATIONS[g]
    hi = lax.Precision.HIGHEST
    bucket = _t5_bucket(jnp.arange(BAND + 1) * dil)
    pick = (bucket[:, None] == jnp.arange(N_BUCKETS)[None, :]).astype(F32)
    per_dist = jnp.dot(pick, rel_bias[:, g * HEADS:(g + 1) * HEADS].astype(F32), precision=hi)
    dsub = np.arange(BAND)[:, None] + BAND - np.arange(2 * BAND)[None, :]
    valid = (dsub >= 0) & (dsub <= WINDOWS[g] // dil)
    spread = (jnp.asarray(np.clip(dsub, 0, BAND).reshape(1, -1)) == jnp.arange(BAND + 1)[:, None]).astype(F32)
    band = jnp.dot(per_dist.T, spread, precision=hi).reshape(HEADS, BAND, 2 * BAND)
    add = jnp.where(jnp.asarray(valid)[None], band, NEG_INF)
    addc = per_dist[::-1][:BAND].T
    addn = per_dist[0][:, None]
    return add, addc, addn


def kernel(x_prompt, x_sample, cache_kv_w128, cache_kv_w512, cache_kv_w2048, p_prompt, p_sample, rel_bias, norm1,
           w_in, q_norm, k_norm, b_v_norm, w_spatial, b_spatial, w_up_a, w_up_b, w_out, norm2, w_query,
           sub_keys_1, sub_keys_2, expert_down, expert_up, norm3, w_ple_gate, w_ple_proj):
    bsz, seq, _ = x_prompt.shape
    db, ds, _ = x_sample.shape
    depth = norm1.shape[0]
    assert depth == 1 and ds == 1 and seq % (BAND * DILATIONS[-1]) == 0 and seq >= WINDOWS[-1]
    caches = (cache_kv_w128, cache_kv_w512, cache_kv_w2048)
    for g in range(N_GROUPS):
        assert caches[g].shape[2] == WINDOWS[g]
    t_p = bsz * seq
    t_s = 128
    li = 0

    g1 = norm1[li][None, :]
    w_in_bf = w_in[li].astype(BF16)
    qg = jnp.tile(q_norm[li], (1, HEADS))
    kg = jnp.tile(k_norm[li], (1, HEADS))
    bvg = b_v_norm[li][None, :]
    blk = np.arange(V7X_MXU_DIM) // HEAD_DIM
    hsum = jnp.asarray(blk[:, None] == blk[None, :], dtype=BF16)
    wua, wub, wo = w_up_a[li].astype(BF16), w_up_b[li].astype(BF16), w_out[li].astype(BF16)
    g2, g3 = norm2[li][None, :], norm3[li][None, :]
    wq = w_query[li].astype(BF16)
    sk1, sk2 = sub_keys_1[li].astype(BF16), sub_keys_2[li].astype(BF16)
    ed = expert_down[li].astype(BF16)
    eu = expert_up[li].astype(BF16)
    wpg, wpp = w_ple_gate[li].astype(BF16), w_ple_proj[li].astype(BF16)
    bs_full = jnp.repeat(b_spatial[li].T, D_MODEL // B_GROUPS, axis=1)
    tables = [_bias_tables(rel_bias, g) for g in range(N_GROUPS)]

    res = _in_projection_prompt(x_prompt, g1, w_in_bf, qg, kg, bvg, hsum, TM_PROJ)
    qs, ks, vs, tails = res[0:3], res[3:6], res[6:9], res[9:12]
    ub, vb, ga, gb = res[12:16]
    outs, lses, kv_prompt = [], [], []
    for g in range(N_GROUPS):
        o, lse = _prompt_attention(qs[g], ks[g], vs[g], tables[g][0], g)
        outs.append(o)
        lses.append(lse)
        kv_prompt.append(tails[g].reshape(1, bsz, tails[g].shape[1], 2, HEADS, HEAD_DIM))
    x1, h2, pq = _merge_prompt(x_prompt, outs, lses, ub, vb, ga, gb, w_spatial[li], bs_full, wua, wub, wo, g2, wq,
                               TM_PROJ)
    x1, h2, pq = x1.reshape(t_p, D_MODEL), h2.reshape(t_p, D_MODEL), pq.reshape(t_p, -1)
    i1, i2, gt = _route(pq, sk1, sk2, TM_PROJ)
    y_prompt = _peer_ple(h2, x1, i1, i2, gt, p_prompt[li].reshape(t_p, PLE_DIM), ed, eu, g3, wpg, wpp, TM_PEER)
    y_prompt = y_prompt.reshape(bsz, seq, D_MODEL)

    xs = jnp.pad(x_sample.reshape(db, D_MODEL), ((0, t_s - db), (0, 0)))
    q, k, v, ub, vb, ga, gb = _in_projection_sample(xs, g1, w_in_bf, qg, kg, bvg, hsum)
    nh = N_GROUPS * HEADS
    k3, v3 = k[:db].reshape(db, nh, HEAD_DIM), v[:db].reshape(db, nh, HEAD_DIM)
    caches_t = [jnp.transpose(caches[g], (0, 1, 3, 4, 5, 2))[li] for g in range(N_GROUPS)]
    adds = []
    for g in range(N_GROUPS):
        dil = DILATIONS[g]
        full = jnp.full((HEADS, BAND, dil), NEG_INF, F32).at[:, :, 0].set(tables[g][1])
        adds.append(full.reshape(HEADS, BAND * dil))
    addn = jnp.stack([tables[g][2] for g in range(N_GROUPS)])
    o3, lse3 = _sample_attention(q[:db, None, :].astype(F32), k[:db, None, :], v[:db, None, :], caches_t, adds, addn)
    pad = lambda a: jnp.pad(a, ((0, t_s - db), (0, 0)))
    o2, lse2 = o3.reshape(db, QKV_W), lse3.reshape(db, QKV_W)
    outs = [pad(o2[:, g * GROUP_W:(g + 1) * GROUP_W]) for g in range(N_GROUPS)]
    lses = [pad(lse2[:, g * GROUP_W:(g + 1) * GROUP_W]) for g in range(N_GROUPS)]
    kv_sample = [jnp.stack([k3[:, None, g * HEADS:(g + 1) * HEADS], v3[:, None, g * HEADS:(g + 1) * HEADS]],
                           axis=2)[None] for g in range(N_GROUPS)]
    ws0 = jnp.repeat(w_spatial[li][:, 0, 0], D_MODEL // B_GROUPS)[None, :]
    bs0 = bs_full[0:1, :]
    x1, h2, pq = _merge_sample(xs, outs, lses, ub, vb, ga, gb, ws0, bs0, wua, wub, wo, g2, wq)
    i1, i2, gt = _route(pq, sk1, sk2, t_s)
    ps = jnp.pad(p_sample[li].reshape(db, PLE_DIM), ((0, t_s - db), (0, 0)))
    y_sample = _peer_ple(h2, x1, i1, i2, gt, ps, ed, eu, g3, wpg, wpp, t_s)[:db].reshape(db, ds, D_MODEL)
    v_sample_chunk = vb[:db].reshape(1, db, ds, D_MODEL)

    return (y_prompt, y_sample, kv_prompt[0], kv_prompt[1], kv_prompt[2],
            kv_sample[0], kv_sample[1], kv_sample[2], v_sample_chunk)
```

```python
import functools
import math

import numpy as np
import jax
import jax.numpy as jnp
from jax import lax
from jax.experimental import pallas as pl
from jax.experimental.pallas import tpu as pltpu

F32 = jnp.float32
BF16 = jnp.bfloat16

D_MODEL = 1024
N_GROUPS = 3
HEADS = 8
HEAD_DIM = 64
GROUP_W = HEADS * HEAD_DIM
QKV_W = N_GROUPS * GROUP_W
WINDOWS = (128, 512, 2048)
DILATIONS = (1, 4, 16)
BAND = 128
N_BUCKETS = 32
MAX_DISTANCE = 2048
CHUNK = 128
B_GROUPS = 4
N_KEYS = 128
PEER_HEADS = 8
PEER_TOPK = 16
PEER_HALF = 128
PLE_DIM = 256
EPS = 1e-6
NEG_INF = -1e30
IN_OFFS = (0, QKV_W, 2 * QKV_W, 3 * QKV_W, 3 * QKV_W + D_MODEL, 3 * QKV_W + 2 * D_MODEL,
           3 * QKV_W + 3 * D_MODEL, 3 * QKV_W + 4 * D_MODEL)

LANES = 128
SLABS = GROUP_W // LANES
V7X_VMEM_LIMIT = 56 * 1024 * 1024
V7X_MXU_DIM = 256
G_PITCH = 72
TM_PROJ = 256
TM_PEER = 512
PAIRS_PER_STEP = 4

def _cparams(sem, vmem=None):
    return pltpu.CompilerParams(dimension_semantics=sem, vmem_limit_bytes=vmem)


def _resident(shape):
    nd = len(shape)
    return pl.BlockSpec(shape, lambda *_: (0,) * nd)


def _nt_dot(a, b):
    return lax.dot_general(a, b, (((1,), (1,)), ((), ())), preferred_element_type=F32)


def _rms(x, gain):
    return x * lax.rsqrt(jnp.mean(x * x, axis=-1, keepdims=True) + EPS) * gain


def _head_norm(y, hsum_ref, gain_row):
    y2 = (y * y).astype(BF16)
    w = hsum_ref.shape[0]
    ss = jnp.concatenate([jnp.dot(y2[:, c:c + w], hsum_ref[...], preferred_element_type=F32)
                          for c in range(0, GROUP_W, w)], axis=1)
    return y * lax.rsqrt(ss * (1.0 / HEAD_DIM) + EPS) * gain_row


def _store_residue_major(out_ref, y, scr_ref, dil):
    if dil == 1:
        out_ref[0, 0] = y.astype(out_ref.dtype)
        return
    n = y.shape[0] // dil
    for s in range(SLABS):
        scr_ref[s] = y[:, s * LANES:(s + 1) * LANES]
    for r in range(dil):
        for s in range(SLABS):
            out_ref[0, r, :, s * LANES:(s + 1) * LANES] = scr_ref[s, pl.ds(r, n, stride=dil), :].astype(out_ref.dtype)


def _gmlp_and_gates(proj, bvg_ref, ub_ref, vb_ref, ga_ref, gb_ref, st):
    half = D_MODEL // 2
    for c in range(2):
        st(ub_ref, c * half, jax.nn.gelu(proj(IN_OFFS[3] + c * half, half)))
    vb0 = jax.nn.gelu(proj(IN_OFFS[4], half))
    vb1 = jax.nn.gelu(proj(IN_OFFS[4] + half, half))
    ms = (jnp.sum(vb0 * vb0, axis=-1, keepdims=True) + jnp.sum(vb1 * vb1, axis=-1, keepdims=True)) * (1.0 / D_MODEL)
    r = lax.rsqrt(ms + EPS)
    st(vb_ref, 0, vb0 * r * bvg_ref[:, :half])
    st(vb_ref, half, vb1 * r * bvg_ref[:, half:])
    for c in range(2):
        st(ga_ref, c * half, proj(IN_OFFS[5] + c * half, half))
        st(gb_ref, c * half, proj(IN_OFFS[6] + c * half, half))


def _inproj_prompt_kernel(x_ref, g1_ref, w_ref, qg_ref, kg_ref, bvg_ref, hsum_ref,
                          q0_ref, q1_ref, q2_ref, k0_ref, k1_ref, k2_ref, v0_ref, v1_ref, v2_ref,
                          t0_ref, t1_ref, t2_ref, ub_ref, vb_ref, ga_ref, gb_ref, scr_ref, *, tail_first):
    ti = pl.program_id(1)
    tm = x_ref.shape[1]
    hb = _rms(x_ref[0], g1_ref[...]).astype(BF16)

    def proj(c0, width):
        return jnp.dot(hb, w_ref[:, c0:c0 + width], preferred_element_type=F32)

    q_refs, k_refs, v_refs, t_refs = (q0_ref, q1_ref, q2_ref), (k0_ref, k1_ref, k2_ref), (v0_ref, v1_ref, v2_ref), \
        (t0_ref, t1_ref, t2_ref)
    for g in range(N_GROUPS):
        lo = g * GROUP_W
        dil = DILATIONS[g]
        qn = _head_norm(proj(IN_OFFS[0] + lo, GROUP_W), hsum_ref, qg_ref[g:g + 1, :]) * (HEAD_DIM ** -0.5)
        _store_residue_major(q_refs[g], qn, scr_ref, dil)
        kn = _head_norm(proj(IN_OFFS[1] + lo, GROUP_W), hsum_ref, kg_ref[g:g + 1, :])
        _store_residue_major(k_refs[g], kn, scr_ref, dil)
        vv = proj(IN_OFFS[2] + lo, GROUP_W)
        _store_residue_major(v_refs[g], vv, scr_ref, dil)
        rows = t_refs[g].shape[1]

        @pl.when(ti >= tail_first[g])
        def _():
            t_refs[g][0, :, :GROUP_W] = kn[tm - rows:, :]
            t_refs[g][0, :, GROUP_W:] = vv[tm - rows:, :]

    def st(ref, c0, val):
        ref[0, :, c0:c0 + val.shape[1]] = val.astype(ref.dtype)

    _gmlp_and_gates(proj, bvg_ref, ub_ref, vb_ref, ga_ref, gb_ref, st)


def _inproj_sample_kernel(x_ref, g1_ref, w_ref, qg_ref, kg_ref, bvg_ref, hsum_ref,
                          q_ref, k_ref, v_ref, ub_ref, vb_ref, ga_ref, gb_ref):
    hb = _rms(x_ref[...], g1_ref[...]).astype(BF16)

    def proj(c0, width):
        return jnp.dot(hb, w_ref[:, c0:c0 + width], preferred_element_type=F32)

    for g in range(N_GROUPS):
        lo = g * GROUP_W
        qn = _head_norm(proj(IN_OFFS[0] + lo, GROUP_W), hsum_ref, qg_ref[g:g + 1, :]) * (HEAD_DIM ** -0.5)
        q_ref[:, lo:lo + GROUP_W] = qn.astype(q_ref.dtype)
        k_ref[:, lo:lo + GROUP_W] = _head_norm(proj(IN_OFFS[1] + lo, GROUP_W), hsum_ref, kg_ref[g:g + 1, :])
        v_ref[:, lo:lo + GROUP_W] = proj(IN_OFFS[2] + lo, GROUP_W)

    def st(ref, c0, val):
        ref[:, c0:c0 + val.shape[1]] = val.astype(ref.dtype)

    _gmlp_and_gates(proj, bvg_ref, ub_ref, vb_ref, ga_ref, gb_ref, st)


def _proj_weight_specs(w_in_bf):
    return [_resident((1, D_MODEL)),
            pl.BlockSpec(w_in_bf.shape, lambda *_: (0, 0), pipeline_mode=pl.Buffered(1)),
            _resident((N_GROUPS, GROUP_W)), _resident((N_GROUPS, GROUP_W)), _resident((1, D_MODEL)),
            _resident((V7X_MXU_DIM, V7X_MXU_DIM))]


def _in_projection_prompt(x, g1, w_in_bf, qg, kg, bvg, hsum, tm):
    b, s, _ = x.shape
    nt = s // tm
    row = pl.BlockSpec((1, tm, D_MODEL), lambda bi, ti: (bi, ti, 0))
    qkv_shapes, qkv_specs, tail_shapes, tail_specs, tail_first = [], [], [], [], []
    for g in range(N_GROUPS):
        dil = DILATIONS[g]
        qkv_shapes.append(jax.ShapeDtypeStruct((b, dil, s // dil, GROUP_W), BF16))
        qkv_specs.append(pl.BlockSpec((1, dil, tm // dil, GROUP_W), lambda bi, ti: (bi, 0, ti, 0)))
        ln = min(WINDOWS[g], s)
        rows = min(ln, tm)
        first = (s - ln) // tm if ln >= tm else nt - 1
        tail_first.append(first)
        tail_shapes.append(jax.ShapeDtypeStruct((b, ln, 2 * GROUP_W), F32))
        tail_specs.append(pl.BlockSpec((1, rows, 2 * GROUP_W),
                                       functools.partial(lambda bi, ti, f: (bi, jnp.maximum(ti - f, 0), 0), f=first)))
    act = jax.ShapeDtypeStruct((b, s, D_MODEL), BF16)
    gate = jax.ShapeDtypeStruct((b, s, D_MODEL), F32)
    return pl.pallas_call(
        functools.partial(_inproj_prompt_kernel, tail_first=tuple(tail_first)),
        out_shape=tuple(qkv_shapes * 3 + tail_shapes + [act, act, gate, gate]),
        grid=(b, nt),
        in_specs=[row] + _proj_weight_specs(w_in_bf),
        out_specs=tuple(qkv_specs * 3 + tail_specs + [row] * 4),
        scratch_shapes=[pltpu.VMEM((SLABS, tm, LANES), F32)],
        compiler_params=_cparams(("arbitrary", "arbitrary"), V7X_VMEM_LIMIT),
        name="in_projection",
    )(x, g1, w_in_bf, qg, kg, bvg, hsum)


def _in_projection_sample(x2d, g1, w_in_bf, qg, kg, bvg, hsum):
    t = x2d.shape[0]
    row = lambda w: pl.BlockSpec((t, w), lambda i: (0, 0))
    outs = (jax.ShapeDtypeStruct((t, QKV_W), BF16), jax.ShapeDtypeStruct((t, QKV_W), F32),
            jax.ShapeDtypeStruct((t, QKV_W), F32), jax.ShapeDtypeStruct((t, D_MODEL), BF16),
            jax.ShapeDtypeStruct((t, D_MODEL), F32), jax.ShapeDtypeStruct((t, D_MODEL), F32),
            jax.ShapeDtypeStruct((t, D_MODEL), F32))
    return pl.pallas_call(
        _inproj_sample_kernel,
        out_shape=outs,
        grid=(1,),
        in_specs=[row(D_MODEL)] + _proj_weight_specs(w_in_bf),
        out_specs=(row(QKV_W), row(QKV_W), row(QKV_W), row(D_MODEL), row(D_MODEL), row(D_MODEL), row(D_MODEL)),
        compiler_params=_cparams(("arbitrary",), V7X_VMEM_LIMIT),
        name="in_projection_sample",
    )(x2d, g1, w_in_bf, qg, kg, bvg, hsum)


def _attn_kernel(q_ref, kp_ref, kc_ref, vp_ref, vc_ref, add_ref, o_ref, lse_ref):
    i = pl.program_id(2)
    q = q_ref[0, 0]
    kk = jnp.concatenate([kp_ref[0, 0], kc_ref[0, 0]], axis=0)
    vv = jnp.concatenate([vp_ref[0, 0], vc_ref[0, 0]], axis=0)
    lane = lax.broadcasted_iota(jnp.int32, (BAND, LANES), 1)
    kcol = lax.broadcasted_iota(jnp.int32, (BAND, 2 * BAND), 1)
    no_prev = kcol < jnp.where(i == 0, BAND, 0)
    for pair in range(HEADS // 2):
        sl = slice(pair * LANES, (pair + 1) * LANES)
        qp, kp, vp = q[:, sl].astype(F32), kk[:, sl], vv[:, sl]
        o_pair = jnp.zeros((BAND, LANES), F32)
        l_pair = jnp.zeros((BAND, LANES), F32)
        for hh in range(2):
            mine = (lane >= HEAD_DIM) if hh else (lane < HEAD_DIM)
            s = _nt_dot(jnp.where(mine, qp, 0.0).astype(BF16), kp) + add_ref[pair * 2 + hh]
            s = jnp.where(no_prev, NEG_INF, s)
            m = jnp.max(s, axis=-1, keepdims=True)
            p = jnp.exp(s - m)
            den = jnp.sum(p, axis=-1, keepdims=True)
            o = jnp.dot(p.astype(BF16), vp, preferred_element_type=F32) / den
            o_pair = jnp.where(mine, o, o_pair)
            l_pair = jnp.where(mine, m + jnp.log(den), l_pair)
        o_ref[0, 0, :, sl] = o_pair.astype(o_ref.dtype)
        lse_ref[0, 0, :, sl] = l_pair


def _prompt_attention(q4, k4, v4, add, g):
    b, dil, n, _ = q4.shape
    cur = pl.BlockSpec((1, 1, BAND, GROUP_W), lambda bi, r, i: (bi, r, i, 0))
    prev = pl.BlockSpec((1, 1, BAND, GROUP_W), lambda bi, r, i: (bi, r, jnp.maximum(i - 1, 0), 0))
    return pl.pallas_call(
        _attn_kernel,
        out_shape=(jax.ShapeDtypeStruct(q4.shape, BF16), jax.ShapeDtypeStruct(q4.shape, F32)),
        grid=(b, dil, n // BAND),
        in_specs=[cur, prev, cur, prev, cur, _resident(add.shape)],
        out_specs=(cur, cur),
        compiler_params=_cparams(("parallel", "parallel", "arbitrary")),
        name=f"attn_g{g}",
    )(q4, k4, k4, v4, v4, add)


def _sample_attn_kernel(q_ref, kn_ref, vn_ref, c0_ref, c1_ref, c2_ref, a0_ref, a1_ref, a2_ref, addn_ref,
                        o_ref, lse_ref):
    rounded = lambda a: a.astype(BF16).astype(F32)
    head_of_lane = lax.broadcasted_iota(jnp.int32, (HEADS, GROUP_W), 1) // HEAD_DIM
    mine = head_of_lane == lax.broadcasted_iota(jnp.int32, (HEADS, GROUP_W), 0)
    for g, (c_ref, a_ref) in enumerate(((c0_ref, a0_ref), (c1_ref, a1_ref), (c2_ref, a2_ref))):
        cols = slice(g * GROUP_W, (g + 1) * GROUP_W)
        ln = c_ref.shape[-1]
        qm = jnp.where(mine, jnp.broadcast_to(q_ref[0, :, cols], (HEADS, GROUP_W)), 0.0)
        kt = c_ref[0].reshape(GROUP_W, ln).astype(BF16)
        vt = c_ref[1].reshape(GROUP_W, ln).astype(BF16)
        kn, vn = rounded(kn_ref[0, :, cols]), rounded(vn_ref[0, :, cols])
        s_c = jnp.dot(qm.astype(BF16), kt, preferred_element_type=F32) + a_ref[...]
        s_n = jnp.sum(qm * kn, axis=-1, keepdims=True) + addn_ref[g]
        m = jnp.maximum(jnp.max(s_c, axis=-1, keepdims=True), s_n)
        p_c = jnp.exp(s_c - m)
        p_n = jnp.exp(s_n - m)
        den = jnp.sum(p_c, axis=-1, keepdims=True) + p_n
        o_all = (_nt_dot(p_c.astype(BF16), vt) + rounded(p_n) * vn) / den
        lse = jnp.broadcast_to(m + jnp.log(den), (HEADS, GROUP_W))
        o_ref[0, :, cols] = jnp.sum(jnp.where(mine, o_all, 0.0), axis=0, keepdims=True)
        lse_ref[0, :, cols] = jnp.sum(jnp.where(mine, lse, 0.0), axis=0, keepdims=True)


def _sample_attention(q_s, k_s, v_s, caches_t, adds, addn):
    db = q_s.shape[0]
    tok = pl.BlockSpec((1, 1, QKV_W), lambda bi: (bi, 0, 0))
    cache_specs = [pl.BlockSpec((None,) + c.shape[1:], lambda bi: (bi, 0, 0, 0, 0)) for c in caches_t]
    shp = jax.ShapeDtypeStruct(q_s.shape, F32)
    return pl.pallas_call(
        _sample_attn_kernel,
        out_shape=(shp, shp),
        grid=(db,),
        in_specs=[tok, tok, tok] + cache_specs + [_resident(a.shape) for a in adds] + [_resident(addn.shape)],
        out_specs=(tok, tok),
        compiler_params=_cparams(("parallel",), V7X_VMEM_LIMIT),
        name="sample_attn",
    )(q_s, k_s, v_s, *caches_t, *adds, addn)


def _load_token_major(ref, scr_ref, dil):
    if dil == 1:
        return ref[0, 0].astype(F32)
    n = ref.shape[2]
    for r in range(dil):
        blk = ref[0, r].astype(F32)
        for s in range(SLABS):
            scr_ref[s, pl.ds(r, n, stride=dil), :] = blk[:, s * LANES:(s + 1) * LANES]
    return jnp.concatenate([scr_ref[s] for s in range(SLABS)], axis=1)


def _merge_tail(x, a_out, b_out, ga, gb, wua_ref, wub_ref, wo_ref, g2_ref, wq_ref):
    up_a = jnp.dot(a_out.astype(BF16), wua_ref[...], preferred_element_type=F32)
    up_b = jnp.dot(b_out.astype(BF16), wub_ref[...], preferred_element_type=F32)
    mrg = jax.nn.sigmoid(ga) * up_a + jax.nn.sigmoid(gb) * up_b
    x1 = x + jnp.dot(mrg.astype(BF16), wo_ref[...], preferred_element_type=F32)
    h2 = _rms(x1, g2_ref[...]).astype(BF16)
    return x1, h2, jnp.dot(h2, wq_ref[...], preferred_element_type=F32)


def _combine_groups(os, ls):
    m = jnp.maximum(jnp.maximum(ls[0], ls[1]), ls[2])
    es = [jnp.exp(l - m) for l in ls]
    return (es[0] * os[0] + es[1] * os[1] + es[2] * os[2]) / (es[0] + es[1] + es[2])


def _merge_prompt_kernel(x_ref, o0_ref, o1_ref, o2_ref, l0_ref, l1_ref, l2_ref, ub_ref, vb_ref, ga_ref, gb_ref,
                         ws_ref, bs_ref, wua_ref, wub_ref, wo_ref, g2_ref, wq_ref,
                         x1_ref, h2_ref, pq_ref, scr_ref):
    tm = x_ref.shape[1]
    os = [_load_token_major(r, scr_ref, DILATIONS[g]) for g, r in enumerate((o0_ref, o1_ref, o2_ref))]
    ls = [_load_token_major(r, scr_ref, DILATIONS[g]) for g, r in enumerate((l0_ref, l1_ref, l2_ref))]
    a_out = _combine_groups(os, ls)

    gw = D_MODEL // B_GROUPS
    tri = lax.broadcasted_iota(jnp.int32, (CHUNK, CHUNK), 0) >= lax.broadcasted_iota(jnp.int32, (CHUNK, CHUNK), 1)
    parts = []
    for c in range(tm // CHUNK):
        rows = slice(c * CHUNK, (c + 1) * CHUNK)
        cols = []
        for g in range(B_GROUPS):
            w = jnp.where(tri, ws_ref[g], 0.0).astype(BF16)
            cols.append(jnp.dot(w, vb_ref[0, rows, g * gw:(g + 1) * gw], preferred_element_type=F32))
        mixed = jnp.concatenate(cols, axis=1) + bs_ref[...]
        parts.append(ub_ref[0, rows, :].astype(F32) * mixed)
    b_out = jnp.concatenate(parts, axis=0)

    x1, h2, pq = _merge_tail(x_ref[0], a_out, b_out, ga_ref[0], gb_ref[0], wua_ref, wub_ref, wo_ref, g2_ref, wq_ref)
    x1_ref[0] = x1
    h2_ref[0] = h2
    for h in range(PEER_HEADS):
        pq_ref[h, 0] = pq[:, h * 2 * PEER_HALF:(h + 1) * 2 * PEER_HALF].astype(pq_ref.dtype)


def _merge_sample_kernel(x_ref, o0_ref, o1_ref, o2_ref, l0_ref, l1_ref, l2_ref, ub_ref, vb_ref, ga_ref, gb_ref,
                         ws_ref, bs_ref, wua_ref, wub_ref, wo_ref, g2_ref, wq_ref, x1_ref, h2_ref, pq_ref):
    a_out = _combine_groups([o0_ref[...], o1_ref[...], o2_ref[...]], [l0_ref[...], l1_ref[...], l2_ref[...]])
    b_out = ub_ref[...].astype(F32) * (vb_ref[...] * ws_ref[...] + bs_ref[...])
    x1, h2, pq = _merge_tail(x_ref[...], a_out, b_out, ga_ref[...], gb_ref[...], wua_ref, wub_ref, wo_ref, g2_ref, wq_ref)
    x1_ref[...] = x1
    h2_ref[...] = h2
    for h in range(PEER_HEADS):
        pq_ref[h] = pq[:, h * 2 * PEER_HALF:(h + 1) * 2 * PEER_HALF].astype(pq_ref.dtype)


def _merge_prompt(x, o, lse, ub, vb, ga, gb, ws, bs, wua, wub, wo, g2, wq, tm):
    b, s, _ = x.shape
    row = lambda w: pl.BlockSpec((1, tm, w), lambda bi, ti: (bi, ti, 0))
    dilated = [pl.BlockSpec((1, d, tm // d, GROUP_W), lambda bi, ti: (bi, 0, ti, 0)) for d in DILATIONS]
    weights = (ws, bs, wua, wub, wo, g2, wq)
    return pl.pallas_call(
        _merge_prompt_kernel,
        out_shape=(jax.ShapeDtypeStruct((b, s, D_MODEL), F32), jax.ShapeDtypeStruct((b, s, D_MODEL), BF16),
                   jax.ShapeDtypeStruct((PEER_HEADS, b, s, 2 * PEER_HALF), BF16)),
        grid=(b, s // tm),
        in_specs=[row(D_MODEL)] + dilated + dilated + [row(D_MODEL)] * 4 + [_resident(w.shape) for w in weights],
        out_specs=(row(D_MODEL), row(D_MODEL),
                   pl.BlockSpec((PEER_HEADS, 1, tm, 2 * PEER_HALF), lambda bi, ti: (0, bi, ti, 0))),
        scratch_shapes=[pltpu.VMEM((SLABS, tm, LANES), F32)],
        compiler_params=_cparams(("parallel", "parallel"), V7X_VMEM_LIMIT),
        name="merge",
    )(x, *o, *lse, ub, vb, ga, gb, *weights)


def _merge_sample(x2d, o, lse, ub, vb, ga, gb, ws0, bs0, wua, wub, wo, g2, wq):
    t = x2d.shape[0]
    row = lambda w: pl.BlockSpec((t, w), lambda i: (0, 0))
    weights = (ws0, bs0, wua, wub, wo, g2, wq)
    pq_shape = (PEER_HEADS, t, 2 * PEER_HALF)
    return pl.pallas_call(
        _merge_sample_kernel,
        out_shape=(jax.ShapeDtypeStruct((t, D_MODEL), F32), jax.ShapeDtypeStruct((t, D_MODEL), BF16),
                   jax.ShapeDtypeStruct(pq_shape, BF16)),
        grid=(1,),
        in_specs=[row(D_MODEL)] + [row(GROUP_W)] * 6 + [row(D_MODEL)] * 4 + [_resident(w.shape) for w in weights],
        out_specs=(row(D_MODEL), row(D_MODEL), _resident(pq_shape)),
        compiler_params=_cparams(("arbitrary",), V7X_VMEM_LIMIT),
        name="merge_sample",
    )(x2d, *o, *lse, ub, vb, ga, gb, *weights)


def _top16_rows(s, idx_col):
    vals, idxs = [], []
    big = jnp.float32(1e9)
    for _ in range(PEER_TOPK):
        m = jnp.max(s, axis=0, keepdims=True)
        sel = jnp.min(jnp.where(s == m, idx_col, big), axis=0, keepdims=True)
        s = jnp.where(idx_col == sel, -jnp.inf, s)
        vals.append(m)
        idxs.append(sel)
    return vals, idxs


def _route_chunk(qh, sk1, sk2):
    key_col = lax.broadcasted_iota(jnp.int32, (N_KEYS, LANES), 0).astype(F32)
    row8 = lax.broadcasted_iota(jnp.int32, (8, LANES), 0).astype(F32)
    row16 = lax.broadcasted_iota(jnp.int32, (PEER_TOPK, LANES), 0).astype(F32)
    s1 = _nt_dot(sk1, qh[:, :PEER_HALF])
    s2 = _nt_dot(sk2, qh[:, PEER_HALF:])
    v1, i1 = _top16_rows(s1, key_col)
    v2, i2 = _top16_rows(s2, key_col)
    v2a = jnp.concatenate(v2, axis=0)
    i2a = jnp.concatenate(i2, axis=0)
    cand = [v1[0] + v2a]
    flat = [row16]
    code = [i1[0] * N_KEYS + i2a]
    for j1 in range(1, 8):
        cand.append(v1[j1] + v2a[:8])
        flat.append(row8 + float(j1 * PEER_TOPK))
        code.append(i1[j1] * N_KEYS + i2a[:8])
    cand.append(jnp.concatenate(v1[8:], axis=0) + v2[0])
    flat.append((row8 + 8.0) * float(PEER_TOPK))
    code.append(jnp.concatenate(i1[8:], axis=0) * N_KEYS + i2[0])
    cand = jnp.concatenate(cand, axis=0)
    flat = jnp.concatenate(flat, axis=0)
    code = jnp.concatenate(code, axis=0)
    top_s, top_e = [], []
    for _ in range(PEER_TOPK):
        m = jnp.max(cand, axis=0, keepdims=True)
        sel = jnp.min(jnp.where(cand == m, flat, 1e9), axis=0, keepdims=True)
        hit = flat == sel
        top_e.append(jnp.max(jnp.where(hit, code, -1.0), axis=0, keepdims=True))
        cand = jnp.where(hit, -jnp.inf, cand)
        top_s.append(m)
    ts = jnp.concatenate(top_s, axis=0)
    te = jnp.concatenate(top_e, axis=0)
    e = jnp.exp(ts - ts[0:1])
    gate = e / jnp.sum(e, axis=0, keepdims=True)
    a = jnp.floor(te * (1.0 / N_KEYS))
    return a, te - a * N_KEYS, gate


def _route_kernel(pq_ref, sk1_ref, sk2_ref, i1_ref, i2_ref, gate_ref):
    for c in range(pq_ref.shape[0] // LANES):
        rows = slice(c * LANES, (c + 1) * LANES)
        a, b, gate = _route_chunk(pq_ref[rows, :], sk1_ref[...], sk2_ref[...])
        i1_ref[:, rows] = a
        i2_ref[:, rows] = b
        gate_ref[:, rows] = gate


def _route(pq3, sk1_bf, sk2_bf, tm, n_tokens):
    out = pl.BlockSpec((PEER_TOPK, tm), lambda i, h: (h, i))
    shp = jax.ShapeDtypeStruct((PEER_HEADS * PEER_TOPK, n_tokens), F32)
    return pl.pallas_call(
        _route_kernel,
        out_shape=(shp, shp, shp),
        grid=(n_tokens // tm, PEER_HEADS),
        in_specs=[pl.BlockSpec((None, tm, 2 * PEER_HALF), lambda i, h: (h, i, 0)),
                  _resident(sk1_bf.shape), _resident(sk2_bf.shape)],
        out_specs=(out, out, out),
        compiler_params=_cparams(("parallel", "parallel")),
        name="peer_route",
    )(pq3, sk1_bf, sk2_bf)


def _peer_kernel(h2_ref, x1_ref, i10_ref, i20_ref, gt0_ref, pqn_ref, sk1_ref, sk2_ref, ed_ref, eu_ref,
                 x2_ref, pk_ref, acc_ref, rows_ref, picks_ref, *, phase_units, units_per_step):
    tm = h2_ref.shape[0]
    i = pl.program_id(0)
    j = pl.program_id(1)
    half = N_KEYS // 2
    n_chunks = tm // LANES

    def route_unit(u):
        c = u // PEER_HEADS
        h = u % PEER_HEADS
        qh = pqn_ref[h, pl.ds(pl.multiple_of(c * LANES, LANES), LANES), :]
        picks = _route_chunk(qh, sk1_ref[...], sk2_ref[...])
        r0 = pl.multiple_of(h * PEER_TOPK, PEER_TOPK)
        for q in range(3):
            picks_ref[q, c, pl.ds(r0, PEER_TOPK), :] = picks[q]

    @pl.when(j == 0)
    def _start_tile():
        @pl.when(i == 0)
        def _():
            for q, src in enumerate((i10_ref, i20_ref, gt0_ref)):
                for c in range(n_chunks):
                    rows_ref[q, c * LANES:(c + 1) * LANES, :] = src[:, c * LANES:(c + 1) * LANES].T

        @pl.when(i > 0)
        def _():
            for q in range(3):
                for c in range(n_chunks):
                    rows_ref[q, c * LANES:(c + 1) * LANES, :] = picks_ref[q, c].T

        acc_ref[...] = jnp.zeros_like(acc_ref)
        sub = lax.broadcasted_iota(jnp.int32, (N_KEYS, N_KEYS), 0)
        b_of_row = sub.astype(F32)
        a_of_row = jnp.where(sub < half, 2 * sub, 2 * (sub - half) + 1).astype(F32)
        tok = tm // phase_units

        def body(u, carry):
            route_unit(u)
            for t in range(tok):
                tt = u * tok + t
                r1 = rows_ref[0, pl.ds(tt, 1), :]
                r2 = rows_ref[1, pl.ds(tt, 1), :]
                rg = rows_ref[2, pl.ds(tt, 1), :]
                oat = jnp.where(a_of_row == r1, 1.0, 0.0).astype(BF16)
                gbt = jnp.where(b_of_row == r2, rg, 0.0).astype(BF16)
                g = _nt_dot(oat, gbt)
                packed = pltpu.pack_elementwise([g[:half], g[half:]], packed_dtype=BF16)
                pk_ref[pl.ds(pl.multiple_of(tt * G_PITCH, 8), half), :] = packed
            return carry

        lax.fori_loop(0, phase_units, body, 0)

    for k in range(units_per_step):
        route_unit(phase_units + j * units_per_step + k)
    gs = []
    for r in range(PAIRS_PER_STEP):
        w = pk_ref[pl.ds(j * PAIRS_PER_STEP + r, tm, stride=G_PITCH), :]
        gs.append(pltpu.unpack_elementwise(w, index=0, packed_dtype=BF16, unpacked_dtype=F32))
        gs.append(pltpu.unpack_elementwise(w, index=1, packed_dtype=BF16, unpacked_dtype=F32))
    gates = jnp.concatenate(gs, axis=1)
    s = _nt_dot(h2_ref[...], ed_ref[...])
    act = (jax.nn.gelu(s) * gates).astype(BF16)
    acc_ref[...] += jnp.dot(act, eu_ref[...], preferred_element_type=F32)

    @pl.when(j == pl.num_programs(1) - 1)
    def _finish():
        x2_ref[...] = x1_ref[...] + acc_ref[...]


def _peer(h2, x1, picks0, pq3, sk1, sk2, ed, eu, tm):
    t = h2.shape[0]
    nt = t // tm
    nb = PAIRS_PER_STEP * 2 * N_KEYS
    steps = ed.shape[0] // nb
    n_units = (tm // LANES) * PEER_HEADS
    units_per_step = n_units // (2 * steps)
    phase_units = n_units - steps * units_per_step
    row = lambda w: pl.BlockSpec((tm, w), lambda i, j: (i, 0))
    table = pl.BlockSpec((nb, D_MODEL), lambda i, j: (j, 0))
    return pl.pallas_call(
        functools.partial(_peer_kernel, phase_units=phase_units, units_per_step=units_per_step),
        out_shape=jax.ShapeDtypeStruct((t, D_MODEL), F32),
        grid=(nt, steps),
        in_specs=[row(D_MODEL), row(D_MODEL)] + [_resident(p.shape) for p in picks0]
                 + [pl.BlockSpec((PEER_HEADS, tm, 2 * PEER_HALF), lambda i, j: (0, jnp.minimum(i + 1, nt - 1), 0)),
                    _resident(sk1.shape), _resident(sk2.shape), table, table],
        out_specs=row(D_MODEL),
        scratch_shapes=[pltpu.VMEM((tm * G_PITCH, N_KEYS), jnp.uint32),
                        pltpu.VMEM((tm, D_MODEL), F32),
                        pltpu.VMEM((3, tm, N_KEYS), F32),
                        pltpu.VMEM((3, tm // LANES, PEER_HEADS * PEER_TOPK, LANES), F32)],
        compiler_params=_cparams(("arbitrary", "arbitrary"), V7X_VMEM_LIMIT),
        name="peer",
    )(h2, x1, *picks0, pq3, sk1, sk2, ed, eu)


def _ple_kernel(x2_ref, p_ref, g3_ref, wpg_ref, wpp_ref, y_ref):
    x2 = x2_ref[...]
    h3 = _rms(x2, g3_ref[...]).astype(BF16)
    gate = jax.nn.sigmoid(jnp.dot(h3, wpg_ref[...], preferred_element_type=F32))
    y_ref[...] = x2 + gate * jnp.dot(p_ref[...].astype(BF16), wpp_ref[...], preferred_element_type=F32)


def _ple(x2, p2d, g3, wpg, wpp, tm):
    t = x2.shape[0]
    row = lambda w: pl.BlockSpec((tm, w), lambda i: (i, 0))
    return pl.pallas_call(
        _ple_kernel,
        out_shape=jax.ShapeDtypeStruct((t, D_MODEL), F32),
        grid=(t // tm,),
        in_specs=[row(D_MODEL), row(PLE_DIM), _resident(g3.shape), _resident(wpg.shape), _resident(wpp.shape)],
        out_specs=row(D_MODEL),
        compiler_params=_cparams(("parallel",)),
        name="ple",
    )(x2, p2d, g3, wpg, wpp)


def _t5_bucket(dist):
    max_exact = N_BUCKETS // 2
    df = jnp.maximum(dist, max_exact).astype(F32)
    large = max_exact + (jnp.log(df / max_exact) / math.log(MAX_DISTANCE / max_exact)
                         * (N_BUCKETS - max_exact)).astype(jnp.int32)
    return jnp.where(dist < max_exact, dist, jnp.minimum(large, N_BUCKETS - 1))


def _bias_tables(rel_bias, g):
    dil = DILATIONS[g]
    hi = lax.Precision.HIGHEST
    bucket = _t5_bucket(jnp.arange(BAND + 1) * dil)
    per_dist = rel_bias[:, g * HEADS:(g + 1) * HEADS][bucket].astype(F32)
    dsub = np.arange(BAND)[:, None] + BAND - np.arange(2 * BAND)[None, :]
    valid = (dsub >= 0) & (dsub <= WINDOWS[g] // dil)
    spread = (jnp.asarray(np.clip(dsub, 0, BAND).reshape(1, -1)) == jnp.arange(BAND + 1)[:, None]).astype(F32)
    band = jnp.dot(per_dist.T, spread, precision=hi).reshape(HEADS, BAND, 2 * BAND)
    add = jnp.where(jnp.asarray(valid)[None], band, NEG_INF)
    addc = per_dist[::-1][:BAND].T
    addn = per_dist[0][:, None]
    return add, addc, addn


def kernel(x_prompt, x_sample, cache_kv_w128, cache_kv_w512, cache_kv_w2048, p_prompt, p_sample, rel_bias, norm1,
           w_in, q_norm, k_norm, b_v_norm, w_spatial, b_spatial, w_up_a, w_up_b, w_out, norm2, w_query,
           sub_keys_1, sub_keys_2, expert_down, expert_up, norm3, w_ple_gate, w_ple_proj):
    bsz, seq, _ = x_prompt.shape
    db, ds, _ = x_sample.shape
    depth = norm1.shape[0]
    assert depth == 1 and ds == 1 and seq % (BAND * DILATIONS[-1]) == 0 and seq >= WINDOWS[-1]
    caches = (cache_kv_w128, cache_kv_w512, cache_kv_w2048)
    for g in range(N_GROUPS):
        assert caches[g].shape[2] == WINDOWS[g]
    t_p = bsz * seq
    t_s = 128
    li = 0

    g1 = norm1[li][None, :]
    w_in_bf = w_in[li].astype(BF16)
    qg = jnp.tile(q_norm[li], (1, HEADS))
    kg = jnp.tile(k_norm[li], (1, HEADS))
    bvg = b_v_norm[li][None, :]
    blk = np.arange(V7X_MXU_DIM) // HEAD_DIM
    hsum = jnp.asarray(blk[:, None] == blk[None, :], dtype=BF16)
    wua, wub, wo = w_up_a[li].astype(BF16), w_up_b[li].astype(BF16), w_out[li].astype(BF16)
    g2, g3 = norm2[li][None, :], norm3[li][None, :]
    wq = w_query[li].astype(BF16)
    sk1, sk2 = sub_keys_1[li].astype(BF16), sub_keys_2[li].astype(BF16)
    ed = expert_down[li].astype(BF16)
    eu = expert_up[li].astype(BF16)
    wpg, wpp = w_ple_gate[li].astype(BF16), w_ple_proj[li].astype(BF16)
    bs_full = jnp.repeat(b_spatial[li].T, D_MODEL // B_GROUPS, axis=1)
    tables = [_bias_tables(rel_bias, g) for g in range(N_GROUPS)]

    res = _in_projection_prompt(x_prompt, g1, w_in_bf, qg, kg, bvg, hsum, TM_PROJ)
    qs, ks, vs, tails = res[0:3], res[3:6], res[6:9], res[9:12]
    ub, vb, ga, gb = res[12:16]
    outs, lses, kv_prompt = [], [], []
    for g in range(N_GROUPS):
        o, lse = _prompt_attention(qs[g], ks[g], vs[g], tables[g][0], g)
        outs.append(o)
        lses.append(lse)
        kv_prompt.append(tails[g].reshape(1, bsz, tails[g].shape[1], 2, HEADS, HEAD_DIM))
    x1, h2, pq = _merge_prompt(x_prompt, outs, lses, ub, vb, ga, gb, w_spatial[li], bs_full, wua, wub, wo, g2, wq,
                               TM_PROJ)
    x1, h2 = x1.reshape(t_p, D_MODEL), h2.reshape(t_p, D_MODEL)
    pq = pq.reshape(PEER_HEADS, t_p, 2 * PEER_HALF)
    picks0 = _route(pq, sk1, sk2, TM_PROJ, TM_PEER)
    x2 = _peer(h2, x1, picks0, pq, sk1, sk2, ed, eu, TM_PEER)
    y_prompt = _ple(x2, p_prompt[li].reshape(t_p, PLE_DIM), g3, wpg, wpp, TM_PEER).reshape(bsz, seq, D_MODEL)

    xs = jnp.pad(x_sample.reshape(db, D_MODEL), ((0, t_s - db), (0, 0)))
    q, k, v, ub, vb, ga, gb = _in_projection_sample(xs, g1, w_in_bf, qg, kg, bvg, hsum)
    nh = N_GROUPS * HEADS
    k3, v3 = k[:db].reshape(db, nh, HEAD_DIM), v[:db].reshape(db, nh, HEAD_DIM)
    caches_t = [jnp.transpose(caches[g], (0, 1, 3, 4, 5, 2))[li] for g in range(N_GROUPS)]
    adds = []
    for g in range(N_GROUPS):
        dil = DILATIONS[g]
        full = jnp.full((HEADS, BAND, dil), NEG_INF, F32).at[:, :, 0].set(tables[g][1])
        adds.append(full.reshape(HEADS, BAND * dil))
    addn = jnp.stack([tables[g][2] for g in range(N_GROUPS)])
    o3, lse3 = _sample_attention(q[:db, None, :].astype(F32), k[:db, None, :], v[:db, None, :], caches_t, adds, addn)
    pad = lambda a: jnp.pad(a, ((0, t_s - db), (0, 0)))
    o2, lse2 = o3.reshape(db, QKV_W), lse3.reshape(db, QKV_W)
    outs = [pad(o2[:, g * GROUP_W:(g + 1) * GROUP_W]) for g in range(N_GROUPS)]
    lses = [pad(lse2[:, g * GROUP_W:(g + 1) * GROUP_W]) for g in range(N_GROUPS)]
    kv_sample = [jnp.stack([k3[:, None, g * HEADS:(g + 1) * HEADS], v3[:, None, g * HEADS:(g + 1) * HEADS]],
                           axis=2)[None] for g in range(N_GROUPS)]
    ws0 = jnp.repeat(w_spatial[li][:, 0, 0], D_MODEL // B_GROUPS)[None, :]
    bs0 = bs_full[0:1, :]
    x1, h2, pq = _merge_sample(xs, outs, lses, ub, vb, ga, gb, ws0, bs0, wua, wub, wo, g2, wq)
    picks0 = _route(pq, sk1, sk2, t_s, t_s)
    ps = jnp.pad(p_sample[li].reshape(db, PLE_DIM), ((0, t_s - db), (0, 0)))
    x2 = _peer(h2, x1, picks0, pq, sk1, sk2, ed, eu, t_s)
    y_sample = _ple(x2, ps, g3, wpg, wpp, t_s)[:db].reshape(db, ds, D_MODEL)
    v_sample_chunk = vb[:db].reshape(1, db, ds, D_MODEL)

    return (y_prompt, y_sample, kv_prompt[0], kv_prompt[1], kv_prompt[2],
            kv_sample[0], kv_sample[1], kv_sample[2], v_sample_chunk)
```

```python
import functools
import math

import numpy as np
import jax
import jax.numpy as jnp
from jax import lax
from jax.experimental import pallas as pl
from jax.experimental.pallas import tpu as pltpu

F32 = jnp.float32
BF16 = jnp.bfloat16

D_MODEL = 1024
N_GROUPS = 3
HEADS = 8
HEAD_DIM = 64
GROUP_W = HEADS * HEAD_DIM
QKV_W = N_GROUPS * GROUP_W
WINDOWS = (128, 512, 2048)
DILATIONS = (1, 4, 16)
BAND = 128
N_BUCKETS = 32
MAX_DISTANCE = 2048
CHUNK = 128
B_GROUPS = 4
N_KEYS = 128
PEER_HEADS = 8
PEER_TOPK = 16
PEER_HALF = 128
PLE_DIM = 256
EPS = 1e-6
NEG_INF = -1e30
IN_OFFS = (0, QKV_W, 2 * QKV_W, 3 * QKV_W, 3 * QKV_W + D_MODEL, 3 * QKV_W + 2 * D_MODEL,
           3 * QKV_W + 3 * D_MODEL, 3 * QKV_W + 4 * D_MODEL)

LANES = 128
SLABS = GROUP_W // LANES
V7X_VMEM_LIMIT = 56 * 1024 * 1024
V7X_MXU_DIM = 256
G_PITCH = 72
TM_PROJ = 256
TM_PEER = 512
PAIRS_PER_STEP = 4

def _cparams(sem, vmem=None):
    return pltpu.CompilerParams(dimension_semantics=sem, vmem_limit_bytes=vmem)


def _resident(shape):
    nd = len(shape)
    return pl.BlockSpec(shape, lambda *_: (0,) * nd)


def _nt_dot(a, b):
    return lax.dot_general(a, b, (((1,), (1,)), ((), ())), preferred_element_type=F32)


def _rms(x, gain):
    return x * lax.rsqrt(jnp.mean(x * x, axis=-1, keepdims=True) + EPS) * gain


def _head_norm(y, hsum_ref, gain_row):
    y2 = (y * y).astype(BF16)
    w = hsum_ref.shape[0]
    ss = jnp.concatenate([jnp.dot(y2[:, c:c + w], hsum_ref[...], preferred_element_type=F32)
                          for c in range(0, GROUP_W, w)], axis=1)
    return y * lax.rsqrt(ss * (1.0 / HEAD_DIM) + EPS) * gain_row


def _store_residue_major(out_ref, y, perm_ref, dil):
    if dil == 1:
        out_ref[0, 0] = y.astype(out_ref.dtype)
        return
    n = y.shape[0] // dil
    yp = jnp.dot(perm_ref[...], y.astype(BF16), preferred_element_type=F32).astype(out_ref.dtype)
    for r in range(dil):
        out_ref[0, r] = yp[r * n:(r + 1) * n]


def _gmlp_and_gates(proj, bvg_ref, ub_ref, vb_ref, ga_ref, gb_ref, st):
    half = D_MODEL // 2
    for c in range(2):
        st(ub_ref, c * half, jax.nn.gelu(proj(IN_OFFS[3] + c * half, half)))
    vb0 = jax.nn.gelu(proj(IN_OFFS[4], half))
    vb1 = jax.nn.gelu(proj(IN_OFFS[4] + half, half))
    ms = (jnp.sum(vb0 * vb0, axis=-1, keepdims=True) + jnp.sum(vb1 * vb1, axis=-1, keepdims=True)) * (1.0 / D_MODEL)
    r = lax.rsqrt(ms + EPS)
    st(vb_ref, 0, vb0 * r * bvg_ref[:, :half])
    st(vb_ref, half, vb1 * r * bvg_ref[:, half:])
    for c in range(2):
        st(ga_ref, c * half, proj(IN_OFFS[5] + c * half, half))
        st(gb_ref, c * half, proj(IN_OFFS[6] + c * half, half))


def _inproj_prompt_kernel(x_ref, g1_ref, w_ref, qg_ref, kg_ref, bvg_ref, hsum_ref, perm_ref,
                          q0_ref, q1_ref, q2_ref, k0_ref, k1_ref, k2_ref, v0_ref, v1_ref, v2_ref,
                          t0_ref, t1_ref, t2_ref, ub_ref, vb_ref, ga_ref, gb_ref, *, tail_first):
    ti = pl.program_id(1)
    tm = x_ref.shape[1]
    hb = _rms(x_ref[0], g1_ref[...]).astype(BF16)

    def proj(c0, width):
        return jnp.dot(hb, w_ref[:, c0:c0 + width], preferred_element_type=F32)

    q_refs, k_refs, v_refs, t_refs = (q0_ref, q1_ref, q2_ref), (k0_ref, k1_ref, k2_ref), (v0_ref, v1_ref, v2_ref), \
        (t0_ref, t1_ref, t2_ref)
    for g in range(N_GROUPS):
        lo = g * GROUP_W
        dil = DILATIONS[g]
        qn = _head_norm(proj(IN_OFFS[0] + lo, GROUP_W), hsum_ref, qg_ref[g:g + 1, :]) * (HEAD_DIM ** -0.5)
        _store_residue_major(q_refs[g], qn, perm_ref.at[g], dil)
        kn = _head_norm(proj(IN_OFFS[1] + lo, GROUP_W), hsum_ref, kg_ref[g:g + 1, :])
        _store_residue_major(k_refs[g], kn, perm_ref.at[g], dil)
        vv = proj(IN_OFFS[2] + lo, GROUP_W)
        _store_residue_major(v_refs[g], vv, perm_ref.at[g], dil)
        rows = t_refs[g].shape[1]

        @pl.when(ti >= tail_first[g])
        def _():
            t_refs[g][0, :, :GROUP_W] = kn[tm - rows:, :]
            t_refs[g][0, :, GROUP_W:] = vv[tm - rows:, :]

    def st(ref, c0, val):
        ref[0, :, c0:c0 + val.shape[1]] = val.astype(ref.dtype)

    _gmlp_and_gates(proj, bvg_ref, ub_ref, vb_ref, ga_ref, gb_ref, st)


def _inproj_sample_kernel(x_ref, g1_ref, w_ref, qg_ref, kg_ref, bvg_ref, hsum_ref,
                          q_ref, k_ref, v_ref, ub_ref, vb_ref, ga_ref, gb_ref):
    hb = _rms(x_ref[...], g1_ref[...]).astype(BF16)

    def proj(c0, width):
        return jnp.dot(hb, w_ref[:, c0:c0 + width], preferred_element_type=F32)

    for g in range(N_GROUPS):
        lo = g * GROUP_W
        qn = _head_norm(proj(IN_OFFS[0] + lo, GROUP_W), hsum_ref, qg_ref[g:g + 1, :]) * (HEAD_DIM ** -0.5)
        q_ref[:, lo:lo + GROUP_W] = qn.astype(q_ref.dtype)
        k_ref[:, lo:lo + GROUP_W] = _head_norm(proj(IN_OFFS[1] + lo, GROUP_W), hsum_ref, kg_ref[g:g + 1, :])
        v_ref[:, lo:lo + GROUP_W] = proj(IN_OFFS[2] + lo, GROUP_W)

    def st(ref, c0, val):
        ref[:, c0:c0 + val.shape[1]] = val.astype(ref.dtype)

    _gmlp_and_gates(proj, bvg_ref, ub_ref, vb_ref, ga_ref, gb_ref, st)


def _proj_weight_specs(w_in_bf):
    return [_resident((1, D_MODEL)),
            pl.BlockSpec(w_in_bf.shape, lambda *_: (0, 0), pipeline_mode=pl.Buffered(1)),
            _resident((N_GROUPS, GROUP_W)), _resident((N_GROUPS, GROUP_W)), _resident((1, D_MODEL)),
            _resident((V7X_MXU_DIM, V7X_MXU_DIM))]


def _in_projection_prompt(x, g1, w_in_bf, qg, kg, bvg, hsum, tm):
    b, s, _ = x.shape
    nt = s // tm
    row = pl.BlockSpec((1, tm, D_MODEL), lambda bi, ti: (bi, ti, 0))
    qkv_shapes, qkv_specs, tail_shapes, tail_specs, tail_first = [], [], [], [], []
    for g in range(N_GROUPS):
        dil = DILATIONS[g]
        qkv_shapes.append(jax.ShapeDtypeStruct((b, dil, s // dil, GROUP_W), BF16))
        qkv_specs.append(pl.BlockSpec((1, dil, tm // dil, GROUP_W), lambda bi, ti: (bi, 0, ti, 0)))
        ln = min(WINDOWS[g], s)
        rows = min(ln, tm)
        first = (s - ln) // tm if ln >= tm else nt - 1
        tail_first.append(first)
        tail_shapes.append(jax.ShapeDtypeStruct((b, ln, 2 * GROUP_W), F32))
        tail_specs.append(pl.BlockSpec((1, rows, 2 * GROUP_W),
                                       functools.partial(lambda bi, ti, f: (bi, jnp.maximum(ti - f, 0), 0), f=first)))
    act = jax.ShapeDtypeStruct((b, s, D_MODEL), BF16)
    gate = jax.ShapeDtypeStruct((b, s, D_MODEL), F32)
    perm = np.zeros((N_GROUPS, tm, tm), np.float32)
    for g, dil in enumerate(DILATIONS):
        src = np.arange(tm).reshape(tm // dil, dil).T.reshape(-1)
        perm[g, np.arange(tm), src] = 1.0
    perm = jnp.asarray(perm, dtype=BF16)
    return pl.pallas_call(
        functools.partial(_inproj_prompt_kernel, tail_first=tuple(tail_first)),
        out_shape=tuple(qkv_shapes * 3 + tail_shapes + [act, act, gate, gate]),
        grid=(b, nt),
        in_specs=[row] + _proj_weight_specs(w_in_bf) + [_resident(perm.shape)],
        out_specs=tuple(qkv_specs * 3 + tail_specs + [row] * 4),
        compiler_params=_cparams(("arbitrary", "arbitrary"), V7X_VMEM_LIMIT),
        name="in_projection",
    )(x, g1, w_in_bf, qg, kg, bvg, hsum, perm)


def _in_projection_sample(x2d, g1, w_in_bf, qg, kg, bvg, hsum):
    t = x2d.shape[0]
    row = lambda w: pl.BlockSpec((t, w), lambda i: (0, 0))
    outs = (jax.ShapeDtypeStruct((t, QKV_W), BF16), jax.ShapeDtypeStruct((t, QKV_W), F32),
            jax.ShapeDtypeStruct((t, QKV_W), F32), jax.ShapeDtypeStruct((t, D_MODEL), BF16),
            jax.ShapeDtypeStruct((t, D_MODEL), F32), jax.ShapeDtypeStruct((t, D_MODEL), F32),
            jax.ShapeDtypeStruct((t, D_MODEL), F32))
    return pl.pallas_call(
        _inproj_sample_kernel,
        out_shape=outs,
        grid=(1,),
        in_specs=[row(D_MODEL)] + _proj_weight_specs(w_in_bf),
        out_specs=(row(QKV_W), row(QKV_W), row(QKV_W), row(D_MODEL), row(D_MODEL), row(D_MODEL), row(D_MODEL)),
        compiler_params=_cparams(("arbitrary",), V7X_VMEM_LIMIT),
        name="in_projection_sample",
    )(x2d, g1, w_in_bf, qg, kg, bvg, hsum)


def _attn_kernel(q_ref, kp_ref, kc_ref, vp_ref, vc_ref, add_ref, o_ref, lse_ref):
    i = pl.program_id(2)
    q = q_ref[0, 0]
    kk = jnp.concatenate([kp_ref[0, 0], kc_ref[0, 0]], axis=0)
    vv = jnp.concatenate([vp_ref[0, 0], vc_ref[0, 0]], axis=0)
    lane = lax.broadcasted_iota(jnp.int32, (BAND, LANES), 1)
    first = jnp.where(i == 0, 1, 0)
    for pair in range(HEADS // 2):
        sl = slice(pair * LANES, (pair + 1) * LANES)
        qp, kp, vp = q[:, sl].astype(F32), kk[:, sl], vv[:, sl]
        o_pair = jnp.zeros((BAND, LANES), F32)
        l_pair = jnp.zeros((BAND, LANES), F32)
        for hh in range(2):
            mine = (lane >= HEAD_DIM) if hh else (lane < HEAD_DIM)
            s = _nt_dot(jnp.where(mine, qp, 0.0).astype(BF16), kp) + add_ref[first, pair * 2 + hh]
            m = jnp.max(s, axis=-1, keepdims=True)
            p = jnp.exp(s - m)
            den = jnp.sum(p, axis=-1, keepdims=True)
            o = jnp.dot(p.astype(BF16), vp, preferred_element_type=F32) / den
            o_pair = jnp.where(mine, o, o_pair)
            l_pair = jnp.where(mine, m + jnp.log(den), l_pair)
        o_ref[0, 0, :, sl] = o_pair.astype(o_ref.dtype)
        lse_ref[0, 0, :, sl] = l_pair


def _prompt_attention(q4, k4, v4, add, g):
    b, dil, n, _ = q4.shape
    cur = pl.BlockSpec((1, 1, BAND, GROUP_W), lambda bi, r, i: (bi, r, i, 0))
    prev = pl.BlockSpec((1, 1, BAND, GROUP_W), lambda bi, r, i: (bi, r, jnp.maximum(i - 1, 0), 0))
    return pl.pallas_call(
        _attn_kernel,
        out_shape=(jax.ShapeDtypeStruct(q4.shape, BF16), jax.ShapeDtypeStruct(q4.shape, F32)),
        grid=(b, dil, n // BAND),
        in_specs=[cur, prev, cur, prev, cur, _resident(add.shape)],
        out_specs=(cur, cur),
        compiler_params=_cparams(("parallel", "parallel", "arbitrary")),
        name=f"attn_g{g}",
    )(q4, k4, k4, v4, v4, add)


def _sample_attn_kernel(q_ref, kn_ref, vn_ref, c0_ref, c1_ref, c2_ref, a0_ref, a1_ref, a2_ref, addn_ref,
                        o_ref, lse_ref):
    rounded = lambda a: a.astype(BF16).astype(F32)
    head_of_lane = lax.broadcasted_iota(jnp.int32, (HEADS, GROUP_W), 1) // HEAD_DIM
    mine = head_of_lane == lax.broadcasted_iota(jnp.int32, (HEADS, GROUP_W), 0)
    for g, (c_ref, a_ref) in enumerate(((c0_ref, a0_ref), (c1_ref, a1_ref), (c2_ref, a2_ref))):
        cols = slice(g * GROUP_W, (g + 1) * GROUP_W)
        ln = c_ref.shape[-1]
        qm = jnp.where(mine, jnp.broadcast_to(q_ref[0, :, cols], (HEADS, GROUP_W)), 0.0)
        kt = c_ref[0].reshape(GROUP_W, ln).astype(BF16)
        vt = c_ref[1].reshape(GROUP_W, ln).astype(BF16)
        kn, vn = rounded(kn_ref[0, :, cols]), rounded(vn_ref[0, :, cols])
        s_c = jnp.dot(qm.astype(BF16), kt, preferred_element_type=F32) + a_ref[...]
        s_n = jnp.sum(qm * kn, axis=-1, keepdims=True) + addn_ref[g]
        m = jnp.maximum(jnp.max(s_c, axis=-1, keepdims=True), s_n)
        p_c = jnp.exp(s_c - m)
        p_n = jnp.exp(s_n - m)
        den = jnp.sum(p_c, axis=-1, keepdims=True) + p_n
        o_all = (_nt_dot(p_c.astype(BF16), vt) + rounded(p_n) * vn) / den
        lse = jnp.broadcast_to(m + jnp.log(den), (HEADS, GROUP_W))
        o_ref[0, :, cols] = jnp.sum(jnp.where(mine, o_all, 0.0), axis=0, keepdims=True)
        lse_ref[0, :, cols] = jnp.sum(jnp.where(mine, lse, 0.0), axis=0, keepdims=True)


def _sample_attention(q_s, k_s, v_s, caches_t, adds, addn):
    db = q_s.shape[0]
    tok = pl.BlockSpec((1, 1, QKV_W), lambda bi: (bi, 0, 0))
    cache_specs = [pl.BlockSpec((None,) + c.shape[1:], lambda bi: (bi, 0, 0, 0, 0)) for c in caches_t]
    shp = jax.ShapeDtypeStruct(q_s.shape, F32)
    return pl.pallas_call(
        _sample_attn_kernel,
        out_shape=(shp, shp),
        grid=(db,),
        in_specs=[tok, tok, tok] + cache_specs + [_resident(a.shape) for a in adds] + [_resident(addn.shape)],
        out_specs=(tok, tok),
        compiler_params=_cparams(("parallel",), V7X_VMEM_LIMIT),
        name="sample_attn",
    )(q_s, k_s, v_s, *caches_t, *adds, addn)


def _load_token_major(ref, scr_ref, dil):
    if dil == 1:
        return ref[0, 0].astype(F32)
    n = ref.shape[2]
    for r in range(dil):
        blk = ref[0, r].astype(F32)
        for s in range(SLABS):
            scr_ref[s, pl.ds(r, n, stride=dil), :] = blk[:, s * LANES:(s + 1) * LANES]
    return jnp.concatenate([scr_ref[s] for s in range(SLABS)], axis=1)


def _merge_tail(x, a_out, b_out, ga, gb, wua_ref, wub_ref, wo_ref, g2_ref, wq_ref):
    up_a = jnp.dot(a_out.astype(BF16), wua_ref[...], preferred_element_type=F32)
    up_b = jnp.dot(b_out.astype(BF16), wub_ref[...], preferred_element_type=F32)
    mrg = jax.nn.sigmoid(ga) * up_a + jax.nn.sigmoid(gb) * up_b
    x1 = x + jnp.dot(mrg.astype(BF16), wo_ref[...], preferred_element_type=F32)
    h2 = _rms(x1, g2_ref[...]).astype(BF16)
    return x1, h2, jnp.dot(h2, wq_ref[...], preferred_element_type=F32)


def _combine_groups(os, ls):
    m = jnp.maximum(jnp.maximum(ls[0], ls[1]), ls[2])
    es = [jnp.exp(l - m) for l in ls]
    return (es[0] * os[0] + es[1] * os[1] + es[2] * os[2]) / (es[0] + es[1] + es[2])


def _merge_prompt_kernel(x_ref, o0_ref, o1_ref, o2_ref, l0_ref, l1_ref, l2_ref, ub_ref, vb_ref, ga_ref, gb_ref,
                         ws_ref, bs_ref, wua_ref, wub_ref, wo_ref, g2_ref, wq_ref,
                         x1_ref, h2_ref, pq_ref, scr_ref):
    tm = x_ref.shape[1]
    os = [_load_token_major(r, scr_ref, DILATIONS[g]) for g, r in enumerate((o0_ref, o1_ref, o2_ref))]
    ls = [_load_token_major(r, scr_ref, DILATIONS[g]) for g, r in enumerate((l0_ref, l1_ref, l2_ref))]
    a_out = _combine_groups(os, ls)

    gw = D_MODEL // B_GROUPS
    tri = lax.broadcasted_iota(jnp.int32, (CHUNK, CHUNK), 0) >= lax.broadcasted_iota(jnp.int32, (CHUNK, CHUNK), 1)
    parts = []
    for c in range(tm // CHUNK):
        rows = slice(c * CHUNK, (c + 1) * CHUNK)
        cols = []
        for g in range(B_GROUPS):
            w = jnp.where(tri, ws_ref[g], 0.0).astype(BF16)
            cols.append(jnp.dot(w, vb_ref[0, rows, g * gw:(g + 1) * gw], preferred_element_type=F32))
        mixed = jnp.concatenate(cols, axis=1) + bs_ref[...]
        parts.append(ub_ref[0, rows, :].astype(F32) * mixed)
    b_out = jnp.concatenate(parts, axis=0)

    x1, h2, pq = _merge_tail(x_ref[0], a_out, b_out, ga_ref[0], gb_ref[0], wua_ref, wub_ref, wo_ref, g2_ref, wq_ref)
    x1_ref[0] = x1
    h2_ref[0] = h2
    for h in range(PEER_HEADS):
        pq_ref[h, 0] = pq[:, h * 2 * PEER_HALF:(h + 1) * 2 * PEER_HALF].astype(pq_ref.dtype)


def _merge_sample_kernel(x_ref, o0_ref, o1_ref, o2_ref, l0_ref, l1_ref, l2_ref, ub_ref, vb_ref, ga_ref, gb_ref,
                         ws_ref, bs_ref, wua_ref, wub_ref, wo_ref, g2_ref, wq_ref, x1_ref, h2_ref, pq_ref):
    a_out = _combine_groups([o0_ref[...], o1_ref[...], o2_ref[...]], [l0_ref[...], l1_ref[...], l2_ref[...]])
    b_out = ub_ref[...].astype(F32) * (vb_ref[...] * ws_ref[...] + bs_ref[...])
    x1, h2, pq = _merge_tail(x_ref[...], a_out, b_out, ga_ref[...], gb_ref[...], wua_ref, wub_ref, wo_ref, g2_ref, wq_ref)
    x1_ref[...] = x1
    h2_ref[...] = h2
    for h in range(PEER_HEADS):
        pq_ref[h] = pq[:, h * 2 * PEER_HALF:(h + 1) * 2 * PEER_HALF].astype(pq_ref.dtype)


def _merge_prompt(x, o, lse, ub, vb, ga, gb, ws, bs, wua, wub, wo, g2, wq, tm):
    b, s, _ = x.shape
    row = lambda w: pl.BlockSpec((1, tm, w), lambda bi, ti: (bi, ti, 0))
    dilated = [pl.BlockSpec((1, d, tm // d, GROUP_W), lambda bi, ti: (bi, 0, ti, 0)) for d in DILATIONS]
    weights = (ws, bs, wua, wub, wo, g2, wq)
    return pl.pallas_call(
        _merge_prompt_kernel,
        out_shape=(jax.ShapeDtypeStruct((b, s, D_MODEL), F32), jax.ShapeDtypeStruct((b, s, D_MODEL), BF16),
                   jax.ShapeDtypeStruct((PEER_HEADS, b, s, 2 * PEER_HALF), BF16)),
        grid=(b, s // tm),
        in_specs=[row(D_MODEL)] + dilated + dilated + [row(D_MODEL)] * 4 + [_resident(w.shape) for w in weights],
        out_specs=(row(D_MODEL), row(D_MODEL),
                   pl.BlockSpec((PEER_HEADS, 1, tm, 2 * PEER_HALF), lambda bi, ti: (0, bi, ti, 0))),
        scratch_shapes=[pltpu.VMEM((SLABS, tm, LANES), F32)],
        compiler_params=_cparams(("parallel", "parallel"), V7X_VMEM_LIMIT),
        name="merge",
    )(x, *o, *lse, ub, vb, ga, gb, *weights)


def _merge_sample(x2d, o, lse, ub, vb, ga, gb, ws0, bs0, wua, wub, wo, g2, wq):
    t = x2d.shape[0]
    row = lambda w: pl.BlockSpec((t, w), lambda i: (0, 0))
    weights = (ws0, bs0, wua, wub, wo, g2, wq)
    pq_shape = (PEER_HEADS, t, 2 * PEER_HALF)
    return pl.pallas_call(
        _merge_sample_kernel,
        out_shape=(jax.ShapeDtypeStruct((t, D_MODEL), F32), jax.ShapeDtypeStruct((t, D_MODEL), BF16),
                   jax.ShapeDtypeStruct(pq_shape, BF16)),
        grid=(1,),
        in_specs=[row(D_MODEL)] + [row(GROUP_W)] * 6 + [row(D_MODEL)] * 4 + [_resident(w.shape) for w in weights],
        out_specs=(row(D_MODEL), row(D_MODEL), _resident(pq_shape)),
        compiler_params=_cparams(("arbitrary",), V7X_VMEM_LIMIT),
        name="merge_sample",
    )(x2d, *o, *lse, ub, vb, ga, gb, *weights)


def _top16_rows(s, idx_col):
    vals, idxs = [], []
    big = jnp.float32(1e9)
    for _ in range(PEER_TOPK):
        m = jnp.max(s, axis=0, keepdims=True)
        sel = jnp.min(jnp.where(s == m, idx_col, big), axis=0, keepdims=True)
        s = jnp.where(idx_col == sel, -jnp.inf, s)
        vals.append(m)
        idxs.append(sel)
    return vals, idxs


def _route_keys(qh, sk1, sk2):
    key_col = lax.broadcasted_iota(jnp.int32, (N_KEYS, LANES), 0).astype(F32)
    first = _top16_rows(_nt_dot(sk1, qh[:, :PEER_HALF]), key_col)
    return first + _top16_rows(_nt_dot(sk2, qh[:, PEER_HALF:]), key_col)


def _route_chunk(qh, sk1, sk2):
    return _route_experts(*_route_keys(qh, sk1, sk2))


def _route_experts(v1, i1, v2, i2):
    row8 = lax.broadcasted_iota(jnp.int32, (8, LANES), 0).astype(F32)
    row16 = lax.broadcasted_iota(jnp.int32, (PEER_TOPK, LANES), 0).astype(F32)
    v2a = jnp.concatenate(v2, axis=0)
    i2a = jnp.concatenate(i2, axis=0)
    cand = [v1[0] + v2a]
    flat = [row16]
    code = [i1[0] * N_KEYS + i2a]
    for j1 in range(1, 8):
        cand.append(v1[j1] + v2a[:8])
        flat.append(row8 + float(j1 * PEER_TOPK))
        code.append(i1[j1] * N_KEYS + i2a[:8])
    cand.append(jnp.concatenate(v1[8:], axis=0) + v2[0])
    flat.append((row8 + 8.0) * float(PEER_TOPK))
    code.append(jnp.concatenate(i1[8:], axis=0) * N_KEYS + i2[0])
    cand = jnp.concatenate(cand, axis=0)
    flat = jnp.concatenate(flat, axis=0)
    code = jnp.concatenate(code, axis=0)
    top_s, top_e = [], []
    for _ in range(PEER_TOPK):
        m = jnp.max(cand, axis=0, keepdims=True)
        sel = jnp.min(jnp.where(cand == m, flat, 1e9), axis=0, keepdims=True)
        hit = flat == sel
        top_e.append(jnp.max(jnp.where(hit, code, -1.0), axis=0, keepdims=True))
        cand = jnp.where(hit, -jnp.inf, cand)
        top_s.append(m)
    ts = jnp.concatenate(top_s, axis=0)
    te = jnp.concatenate(top_e, axis=0)
    e = jnp.exp(ts - ts[0:1])
    gate = e / jnp.sum(e, axis=0, keepdims=True)
    a = jnp.floor(te * (1.0 / N_KEYS))
    return a, te - a * N_KEYS, gate


def _route_kernel(pq_ref, sk1_ref, sk2_ref, i1_ref, i2_ref, gate_ref):
    for c in range(pq_ref.shape[0] // LANES):
        rows = slice(c * LANES, (c + 1) * LANES)
        a, b, gate = _route_chunk(pq_ref[rows, :], sk1_ref[...], sk2_ref[...])
        i1_ref[:, rows] = a
        i2_ref[:, rows] = b
        gate_ref[:, rows] = gate


def _route(pq3, sk1_bf, sk2_bf, tm, n_tokens):
    out = pl.BlockSpec((PEER_TOPK, tm), lambda i, h: (h, i))
    shp = jax.ShapeDtypeStruct((PEER_HEADS * PEER_TOPK, n_tokens), F32)
    return pl.pallas_call(
        _route_kernel,
        out_shape=(shp, shp, shp),
        grid=(n_tokens // tm, PEER_HEADS),
        in_specs=[pl.BlockSpec((None, tm, 2 * PEER_HALF), lambda i, h: (h, i, 0)),
                  _resident(sk1_bf.shape), _resident(sk2_bf.shape)],
        out_specs=(out, out, out),
        compiler_params=_cparams(("parallel", "parallel")),
        name="peer_route",
    )(pq3, sk1_bf, sk2_bf)


def _peer_kernel(h2_ref, x1_ref, i10_ref, i20_ref, gt0_ref, pqn_ref, sk1_ref, sk2_ref, ed_ref, eu_ref,
                 x2_ref, pk_ref, acc_ref, rows_ref, picks_ref, *, phase_units, units_per_step):
    tm = h2_ref.shape[0]
    i = pl.program_id(0)
    j = pl.program_id(1)
    half = N_KEYS // 2
    n_chunks = tm // LANES

    def route_keys(u):
        qh = pqn_ref[u % PEER_HEADS, pl.ds(pl.multiple_of((u // PEER_HEADS) * LANES, LANES), LANES), :]
        return _route_keys(qh, sk1_ref[...], sk2_ref[...])

    def route_store(u, keys):
        picks = _route_experts(*keys)
        r0 = pl.multiple_of((u % PEER_HEADS) * PEER_TOPK, PEER_TOPK)
        for q in range(3):
            picks_ref[q, u // PEER_HEADS, pl.ds(r0, PEER_TOPK), :] = picks[q]

    def route_unit(u):
        route_store(u, route_keys(u))

    @pl.when(j == 0)
    def _start_tile():
        @pl.when(i == 0)
        def _():
            for q, src in enumerate((i10_ref, i20_ref, gt0_ref)):
                for c in range(n_chunks):
                    rows_ref[q, c * LANES:(c + 1) * LANES, :] = src[:, c * LANES:(c + 1) * LANES].T

        @pl.when(i > 0)
        def _():
            for q in range(3):
                for c in range(n_chunks):
                    rows_ref[q, c * LANES:(c + 1) * LANES, :] = picks_ref[q, c].T

        acc_ref[...] = jnp.zeros_like(acc_ref)
        sub = lax.broadcasted_iota(jnp.int32, (N_KEYS, N_KEYS), 0)
        b_of_row = sub.astype(F32)
        a_of_row = jnp.where(sub < half, 2 * sub, 2 * (sub - half) + 1).astype(F32)
        tok = tm // phase_units

        def body(u, carry):
            route_unit(u)
            for t in range(tok):
                tt = u * tok + t
                r1 = rows_ref[0, pl.ds(tt, 1), :]
                r2 = rows_ref[1, pl.ds(tt, 1), :]
                rg = rows_ref[2, pl.ds(tt, 1), :]
                oat = jnp.where(a_of_row == r1, 1.0, 0.0).astype(BF16)
                gbt = jnp.where(b_of_row == r2, rg, 0.0).astype(BF16)
                g = _nt_dot(oat, gbt)
                packed = pltpu.pack_elementwise([g[:half], g[half:]], packed_dtype=BF16)
                pk_ref[pl.ds(pl.multiple_of(tt * G_PITCH, 8), half), :] = packed
            return carry

        lax.fori_loop(0, phase_units, body, 0)

    units = [phase_units + j * units_per_step + k for k in range(units_per_step)]
    keys = [route_keys(u) for u in units]
    s = _nt_dot(h2_ref[...], ed_ref[...])
    for u, kk in zip(units, keys):
        route_store(u, kk)
    gs = []
    for r in range(PAIRS_PER_STEP):
        w = pk_ref[pl.ds(j * PAIRS_PER_STEP + r, tm, stride=G_PITCH), :]
        gs.append(pltpu.unpack_elementwise(w, index=0, packed_dtype=BF16, unpacked_dtype=F32))
        gs.append(pltpu.unpack_elementwise(w, index=1, packed_dtype=BF16, unpacked_dtype=F32))
    gates = jnp.concatenate(gs, axis=1)
    act = (jax.nn.gelu(s) * gates).astype(BF16)
    acc_ref[...] += jnp.dot(act, eu_ref[...], preferred_element_type=F32)

    @pl.when(j == pl.num_programs(1) - 1)
    def _finish():
        x2_ref[...] = x1_ref[...] + acc_ref[...]


def _peer(h2, x1, picks0, pq3, sk1, sk2, ed, eu, tm):
    t = h2.shape[0]
    nt = t // tm
    nb = PAIRS_PER_STEP * 2 * N_KEYS
    steps = ed.shape[0] // nb
    n_units = (tm // LANES) * PEER_HEADS
    units_per_step = n_units // (2 * steps)
    phase_units = n_units - steps * units_per_step
    row = lambda w: pl.BlockSpec((tm, w), lambda i, j: (i, 0))
    table = pl.BlockSpec((nb, D_MODEL), lambda i, j: (j, 0))
    return pl.pallas_call(
        functools.partial(_peer_kernel, phase_units=phase_units, units_per_step=units_per_step),
        out_shape=jax.ShapeDtypeStruct((t, D_MODEL), F32),
        grid=(nt, steps),
        in_specs=[row(D_MODEL), row(D_MODEL)] + [_resident(p.shape) for p in picks0]
                 + [pl.BlockSpec((PEER_HEADS, tm, 2 * PEER_HALF), lambda i, j: (0, jnp.minimum(i + 1, nt - 1), 0)),
                    _resident(sk1.shape), _resident(sk2.shape), table, table],
        out_specs=row(D_MODEL),
        scratch_shapes=[pltpu.VMEM((tm * G_PITCH, N_KEYS), jnp.uint32),
                        pltpu.VMEM((tm, D_MODEL), F32),
                        pltpu.VMEM((3, tm, N_KEYS), F32),
                        pltpu.VMEM((3, tm // LANES, PEER_HEADS * PEER_TOPK, LANES), F32)],
        compiler_params=_cparams(("arbitrary", "arbitrary"), V7X_VMEM_LIMIT),
        name="peer",
    )(h2, x1, *picks0, pq3, sk1, sk2, ed, eu)


def _ple_kernel(x2_ref, p_ref, g3_ref, wpg_ref, wpp_ref, y_ref):
    x2 = x2_ref[...]
    h3 = _rms(x2, g3_ref[...]).astype(BF16)
    gate = jax.nn.sigmoid(jnp.dot(h3, wpg_ref[...], preferred_element_type=F32))
    y_ref[...] = x2 + gate * jnp.dot(p_ref[...].astype(BF16), wpp_ref[...], preferred_element_type=F32)


def _ple(x2, p2d, g3, wpg, wpp, tm):
    t = x2.shape[0]
    row = lambda w: pl.BlockSpec((tm, w), lambda i: (i, 0))
    return pl.pallas_call(
        _ple_kernel,
        out_shape=jax.ShapeDtypeStruct((t, D_MODEL), F32),
        grid=(t // tm,),
        in_specs=[row(D_MODEL), row(PLE_DIM), _resident(g3.shape), _resident(wpg.shape), _resident(wpp.shape)],
        out_specs=row(D_MODEL),
        compiler_params=_cparams(("parallel",)),
        name="ple",
    )(x2, p2d, g3, wpg, wpp)


def _t5_bucket(dist):
    max_exact = N_BUCKETS // 2
    df = jnp.maximum(dist, max_exact).astype(F32)
    large = max_exact + (jnp.log(df / max_exact) / math.log(MAX_DISTANCE / max_exact)
                         * (N_BUCKETS - max_exact)).astype(jnp.int32)
    return jnp.where(dist < max_exact, dist, jnp.minimum(large, N_BUCKETS - 1))


def _bias_tables(rel_bias, g):
    dil = DILATIONS[g]
    hi = lax.Precision.HIGHEST
    bucket = _t5_bucket(jnp.arange(BAND + 1) * dil)
    per_dist = rel_bias[:, g * HEADS:(g + 1) * HEADS][bucket].astype(F32)
    dsub = np.arange(BAND)[:, None] + BAND - np.arange(2 * BAND)[None, :]
    valid = (dsub >= 0) & (dsub <= WINDOWS[g] // dil)
    spread = (jnp.asarray(np.clip(dsub, 0, BAND).reshape(1, -1)) == jnp.arange(BAND + 1)[:, None]).astype(F32)
    band = jnp.dot(per_dist.T, spread, precision=hi).reshape(HEADS, BAND, 2 * BAND)
    add = jnp.where(jnp.asarray(valid)[None], band, NEG_INF)
    add = jnp.stack([add, jnp.where(jnp.asarray(np.arange(2 * BAND) < BAND)[None, None], NEG_INF, add)])
    addc = per_dist[::-1][:BAND].T
    addn = per_dist[0][:, None]
    return add, addc, addn


def kernel(x_prompt, x_sample, cache_kv_w128, cache_kv_w512, cache_kv_w2048, p_prompt, p_sample, rel_bias, norm1,
           w_in, q_norm, k_norm, b_v_norm, w_spatial, b_spatial, w_up_a, w_up_b, w_out, norm2, w_query,
           sub_keys_1, sub_keys_2, expert_down, expert_up, norm3, w_ple_gate, w_ple_proj):
    bsz, seq, _ = x_prompt.shape
    db, ds, _ = x_sample.shape
    depth = norm1.shape[0]
    assert depth == 1 and ds == 1 and seq % (BAND * DILATIONS[-1]) == 0 and seq >= WINDOWS[-1]
    caches = (cache_kv_w128, cache_kv_w512, cache_kv_w2048)
    for g in range(N_GROUPS):
        assert caches[g].shape[2] == WINDOWS[g]
    t_p = bsz * seq
    t_s = 128
    li = 0

    g1 = norm1[li][None, :]
    w_in_bf = w_in[li].astype(BF16)
    qg = jnp.tile(q_norm[li], (1, HEADS))
    kg = jnp.tile(k_norm[li], (1, HEADS))
    bvg = b_v_norm[li][None, :]
    blk = np.arange(V7X_MXU_DIM) // HEAD_DIM
    hsum = jnp.asarray(blk[:, None] == blk[None, :], dtype=BF16)
    wua, wub, wo = w_up_a[li].astype(BF16), w_up_b[li].astype(BF16), w_out[li].astype(BF16)
    g2, g3 = norm2[li][None, :], norm3[li][None, :]
    wq = w_query[li].astype(BF16)
    sk1, sk2 = sub_keys_1[li].astype(BF16), sub_keys_2[li].astype(BF16)
    ed = expert_down[li].astype(BF16)
    eu = expert_up[li].astype(BF16)
    wpg, wpp = w_ple_gate[li].astype(BF16), w_ple_proj[li].astype(BF16)
    bs_full = jnp.repeat(b_spatial[li].T, D_MODEL // B_GROUPS, axis=1)
    tables = [_bias_tables(rel_bias, g) for g in range(N_GROUPS)]

    res = _in_projection_prompt(x_prompt, g1, w_in_bf, qg, kg, bvg, hsum, TM_PROJ)
    qs, ks, vs, tails = res[0:3], res[3:6], res[6:9], res[9:12]
    ub, vb, ga, gb = res[12:16]
    outs, lses, kv_prompt = [], [], []
    for g in range(N_GROUPS):
        o, lse = _prompt_attention(qs[g], ks[g], vs[g], tables[g][0], g)
        outs.append(o)
        lses.append(lse)
        kv_prompt.append(tails[g].reshape(1, bsz, tails[g].shape[1], 2, HEADS, HEAD_DIM))
    x1, h2, pq = _merge_prompt(x_prompt, outs, lses, ub, vb, ga, gb, w_spatial[li], bs_full, wua, wub, wo, g2, wq,
                               TM_PROJ)
    x1, h2 = x1.reshape(t_p, D_MODEL), h2.reshape(t_p, D_MODEL)
    pq = pq.reshape(PEER_HEADS, t_p, 2 * PEER_HALF)
    picks0 = _route(pq, sk1, sk2, TM_PROJ, TM_PEER)
    x2 = _peer(h2, x1, picks0, pq, sk1, sk2, ed, eu, TM_PEER)
    y_prompt = _ple(x2, p_prompt[li].reshape(t_p, PLE_DIM), g3, wpg, wpp, TM_PEER).reshape(bsz, seq, D_MODEL)

    xs = jnp.pad(x_sample.reshape(db, D_MODEL), ((0, t_s - db), (0, 0)))
    q, k, v, ub, vb, ga, gb = _in_projection_sample(xs, g1, w_in_bf, qg, kg, bvg, hsum)
    nh = N_GROUPS * HEADS
    k3, v3 = k[:db].reshape(db, nh, HEAD_DIM), v[:db].reshape(db, nh, HEAD_DIM)
    caches_t = [jnp.transpose(caches[g], (0, 1, 3, 4, 5, 2))[li] for g in range(N_GROUPS)]
    adds = []
    for g in range(N_GROUPS):
        dil = DILATIONS[g]
        full = jnp.full((HEADS, BAND, dil), NEG_INF, F32).at[:, :, 0].set(tables[g][1])
        adds.append(full.reshape(HEADS, BAND * dil))
    addn = jnp.stack([tables[g][2] for g in range(N_GROUPS)])
    o3, lse3 = _sample_attention(q[:db, None, :].astype(F32), k[:db, None, :], v[:db, None, :], caches_t, adds, addn)
    pad = lambda a: jnp.pad(a, ((0, t_s - db), (0, 0)))
    o2, lse2 = o3.reshape(db, QKV_W), lse3.reshape(db, QKV_W)
    outs = [pad(o2[:, g * GROUP_W:(g + 1) * GROUP_W]) for g in range(N_GROUPS)]
    lses = [pad(lse2[:, g * GROUP_W:(g + 1) * GROUP_W]) for g in range(N_GROUPS)]
    kv_sample = [jnp.stack([k3[:, None, g * HEADS:(g + 1) * HEADS], v3[:, None, g * HEADS:(g + 1) * HEADS]],
                           axis=2)[None] for g in range(N_GROUPS)]
    ws0 = jnp.repeat(w_spatial[li][:, 0, 0], D_MODEL // B_GROUPS)[None, :]
    bs0 = bs_full[0:1, :]
    x1, h2, pq = _merge_sample(xs, outs, lses, ub, vb, ga, gb, ws0, bs0, wua, wub, wo, g2, wq)
    picks0 = _route(pq, sk1, sk2, t_s, t_s)
    ps = jnp.pad(p_sample[li].reshape(db, PLE_DIM), ((0, t_s - db), (0, 0)))
    x2 = _peer(h2, x1, picks0, pq, sk1, sk2, ed, eu, t_s)
    y_sample = _ple(x2, ps, g3, wpg, wpp, t_s)[:db].reshape(db, ds, D_MODEL)
    v_sample_chunk = vb[:db].reshape(1, db, ds, D_MODEL)

    return (y_prompt, y_sample, kv_prompt[0], kv_prompt[1], kv_prompt[2],
            kv_sample[0], kv_sample[1], kv_sample[2], v_sample_chunk)
```

```python
import functools
import math

import numpy as np
import jax
import jax.numpy as jnp
from jax import lax
from jax.experimental import pallas as pl
from jax.experimental.pallas import tpu as pltpu

F32 = jnp.float32
BF16 = jnp.bfloat16

D_MODEL = 1024
N_GROUPS = 3
HEADS = 8
HEAD_DIM = 64
GROUP_W = HEADS * HEAD_DIM
QKV_W = N_GROUPS * GROUP_W
WINDOWS = (128, 512, 2048)
DILATIONS = (1, 4, 16)
BAND = 128
N_BUCKETS = 32
MAX_DISTANCE = 2048
CHUNK = 128
B_GROUPS = 4
N_KEYS = 128
PEER_HEADS = 8
PEER_TOPK = 16
PEER_HALF = 128
PLE_DIM = 256
EPS = 1e-6
NEG_INF = -1e30
IN_OFFS = (0, QKV_W, 2 * QKV_W, 3 * QKV_W, 3 * QKV_W + D_MODEL, 3 * QKV_W + 2 * D_MODEL,
           3 * QKV_W + 3 * D_MODEL, 3 * QKV_W + 4 * D_MODEL)

LANES = 128
SLABS = GROUP_W // LANES
V7X_VMEM_LIMIT = 56 * 1024 * 1024
V7X_MXU_DIM = 256
G_PITCH = 72
TM_PROJ = 256
TM_INPROJ = 512
TM_PEER = 512
PAIRS_PER_STEP = 4

def _cparams(sem, vmem=None):
    return pltpu.CompilerParams(dimension_semantics=sem, vmem_limit_bytes=vmem)


def _resident(shape):
    nd = len(shape)
    return pl.BlockSpec(shape, lambda *_: (0,) * nd)


def _nt_dot(a, b):
    return lax.dot_general(a, b, (((1,), (1,)), ((), ())), preferred_element_type=F32)


def _rms(x, gain):
    return x * lax.rsqrt(jnp.mean(x * x, axis=-1, keepdims=True) + EPS) * gain


def _head_norm(y, hsum_ref, gain_row):
    y2 = (y * y).astype(BF16)
    w = hsum_ref.shape[0]
    ss = jnp.concatenate([jnp.dot(y2[:, c:c + w], hsum_ref[...], preferred_element_type=F32)
                          for c in range(0, GROUP_W, w)], axis=1)
    return y * lax.rsqrt(ss * (1.0 / HEAD_DIM) + EPS) * gain_row


def _store_residue_major(out_ref, y, scr_ref, dil):
    if dil == 1:
        out_ref[0, 0] = y.astype(out_ref.dtype)
        return
    n = y.shape[0] // dil
    for s in range(SLABS):
        scr_ref[s] = y[:, s * LANES:(s + 1) * LANES]
    for r in range(dil):
        for s in range(SLABS):
            out_ref[0, r, :, s * LANES:(s + 1) * LANES] = scr_ref[s, pl.ds(r, n, stride=dil), :].astype(out_ref.dtype)


def _gmlp_and_gates(proj, bvg_ref, ub_ref, vb_ref, ga_ref, gb_ref, st):
    half = D_MODEL // 2
    for c in range(2):
        st(ub_ref, c * half, jax.nn.gelu(proj(IN_OFFS[3] + c * half, half)))
    vb0 = jax.nn.gelu(proj(IN_OFFS[4], half))
    vb1 = jax.nn.gelu(proj(IN_OFFS[4] + half, half))
    ms = (jnp.sum(vb0 * vb0, axis=-1, keepdims=True) + jnp.sum(vb1 * vb1, axis=-1, keepdims=True)) * (1.0 / D_MODEL)
    r = lax.rsqrt(ms + EPS)
    st(vb_ref, 0, vb0 * r * bvg_ref[:, :half])
    st(vb_ref, half, vb1 * r * bvg_ref[:, half:])
    for c in range(2):
        st(ga_ref, c * half, jax.nn.sigmoid(proj(IN_OFFS[5] + c * half, half)))
        st(gb_ref, c * half, jax.nn.sigmoid(proj(IN_OFFS[6] + c * half, half)))


def _inproj_prompt_kernel(x_ref, g1_ref, w_ref, qg_ref, kg_ref, bvg_ref, hsum_ref,
                          q0_ref, q1_ref, q2_ref, k0_ref, k1_ref, k2_ref, v0_ref, v1_ref, v2_ref,
                          t0_ref, t1_ref, t2_ref, ub_ref, vb_ref, ga_ref, gb_ref, scr_ref, *, tail_first):
    ti = pl.program_id(1)
    tm = x_ref.shape[1]
    hb = _rms(x_ref[0], g1_ref[...]).astype(BF16)

    def proj(c0, width):
        return jnp.dot(hb, w_ref[:, c0:c0 + width], preferred_element_type=F32)

    q_refs, k_refs, v_refs, t_refs = (q0_ref, q1_ref, q2_ref), (k0_ref, k1_ref, k2_ref), (v0_ref, v1_ref, v2_ref), \
        (t0_ref, t1_ref, t2_ref)
    for g in range(N_GROUPS):
        lo = g * GROUP_W
        dil = DILATIONS[g]
        qn = _head_norm(proj(IN_OFFS[0] + lo, GROUP_W), hsum_ref, qg_ref[g:g + 1, :]) * (HEAD_DIM ** -0.5)
        _store_residue_major(q_refs[g], qn, scr_ref, dil)
        kn = _head_norm(proj(IN_OFFS[1] + lo, GROUP_W), hsum_ref, kg_ref[g:g + 1, :])
        _store_residue_major(k_refs[g], kn, scr_ref, dil)
        vv = proj(IN_OFFS[2] + lo, GROUP_W)
        _store_residue_major(v_refs[g], vv, scr_ref, dil)
        rows = t_refs[g].shape[1]

        @pl.when(ti >= tail_first[g])
        def _():
            t_refs[g][0, :, :GROUP_W] = kn[tm - rows:, :]
            t_refs[g][0, :, GROUP_W:] = vv[tm - rows:, :]

    def st(ref, c0, val):
        ref[0, :, c0:c0 + val.shape[1]] = val.astype(ref.dtype)

    _gmlp_and_gates(proj, bvg_ref, ub_ref, vb_ref, ga_ref, gb_ref, st)


def _inproj_sample_kernel(x_ref, g1_ref, w_ref, qg_ref, kg_ref, bvg_ref, hsum_ref,
                          q_ref, k_ref, v_ref, ub_ref, vb_ref, ga_ref, gb_ref):
    hb = _rms(x_ref[...], g1_ref[...]).astype(BF16)

    def proj(c0, width):
        return jnp.dot(hb, w_ref[:, c0:c0 + width], preferred_element_type=F32)

    for g in range(N_GROUPS):
        lo = g * GROUP_W
        qn = _head_norm(proj(IN_OFFS[0] + lo, GROUP_W), hsum_ref, qg_ref[g:g + 1, :]) * (HEAD_DIM ** -0.5)
        q_ref[:, lo:lo + GROUP_W] = qn.astype(q_ref.dtype)
        k_ref[:, lo:lo + GROUP_W] = _head_norm(proj(IN_OFFS[1] + lo, GROUP_W), hsum_ref, kg_ref[g:g + 1, :])
        v_ref[:, lo:lo + GROUP_W] = proj(IN_OFFS[2] + lo, GROUP_W)

    def st(ref, c0, val):
        ref[:, c0:c0 + val.shape[1]] = val.astype(ref.dtype)

    _gmlp_and_gates(proj, bvg_ref, ub_ref, vb_ref, ga_ref, gb_ref, st)


def _proj_weight_specs(w_in_bf):
    return [_resident((1, D_MODEL)),
            pl.BlockSpec(w_in_bf.shape, lambda *_: (0, 0), pipeline_mode=pl.Buffered(1)),
            _resident((N_GROUPS, GROUP_W)), _resident((N_GROUPS, GROUP_W)), _resident((1, D_MODEL)),
            _resident((V7X_MXU_DIM, V7X_MXU_DIM))]


def _in_projection_prompt(x, g1, w_in_bf, qg, kg, bvg, hsum, tm):
    b, s, _ = x.shape
    nt = s // tm
    row = pl.BlockSpec((1, tm, D_MODEL), lambda bi, ti: (bi, ti, 0))
    qkv_shapes, qkv_specs, tail_shapes, tail_specs, tail_first = [], [], [], [], []
    for g in range(N_GROUPS):
        dil = DILATIONS[g]
        qkv_shapes.append(jax.ShapeDtypeStruct((b, dil, s // dil, GROUP_W), BF16))
        qkv_specs.append(pl.BlockSpec((1, dil, tm // dil, GROUP_W), lambda bi, ti: (bi, 0, ti, 0)))
        ln = min(WINDOWS[g], s)
        rows = min(ln, tm)
        first = (s - ln) // tm if ln >= tm else nt - 1
        tail_first.append(first)
        tail_shapes.append(jax.ShapeDtypeStruct((b, ln, 2 * GROUP_W), F32))
        tail_specs.append(pl.BlockSpec((1, rows, 2 * GROUP_W),
                                       functools.partial(lambda bi, ti, f: (bi, jnp.maximum(ti - f, 0), 0), f=first)))
    act = jax.ShapeDtypeStruct((b, s, D_MODEL), BF16)
    gate = jax.ShapeDtypeStruct((b, s, D_MODEL), BF16)
    return pl.pallas_call(
        functools.partial(_inproj_prompt_kernel, tail_first=tuple(tail_first)),
        out_shape=tuple(qkv_shapes * 3 + tail_shapes + [act, act, gate, gate]),
        grid=(b, nt),
        in_specs=[row] + _proj_weight_specs(w_in_bf),
        out_specs=tuple(qkv_specs * 3 + tail_specs + [row] * 4),
        scratch_shapes=[pltpu.VMEM((SLABS, tm, LANES), F32)],
        compiler_params=_cparams(("arbitrary", "arbitrary"), V7X_VMEM_LIMIT),
        name="in_projection",
    )(x, g1, w_in_bf, qg, kg, bvg, hsum)


def _in_projection_sample(x2d, g1, w_in_bf, qg, kg, bvg, hsum):
    t = x2d.shape[0]
    row = lambda w: pl.BlockSpec((t, w), lambda i: (0, 0))
    outs = (jax.ShapeDtypeStruct((t, QKV_W), BF16), jax.ShapeDtypeStruct((t, QKV_W), F32),
            jax.ShapeDtypeStruct((t, QKV_W), F32), jax.ShapeDtypeStruct((t, D_MODEL), BF16),
            jax.ShapeDtypeStruct((t, D_MODEL), F32), jax.ShapeDtypeStruct((t, D_MODEL), BF16),
            jax.ShapeDtypeStruct((t, D_MODEL), BF16))
    return pl.pallas_call(
        _inproj_sample_kernel,
        out_shape=outs,
        grid=(1,),
        in_specs=[row(D_MODEL)] + _proj_weight_specs(w_in_bf),
        out_specs=(row(QKV_W), row(QKV_W), row(QKV_W), row(D_MODEL), row(D_MODEL), row(D_MODEL), row(D_MODEL)),
        compiler_params=_cparams(("arbitrary",), V7X_VMEM_LIMIT),
        name="in_projection_sample",
    )(x2d, g1, w_in_bf, qg, kg, bvg, hsum)


def _attn_kernel(q_ref, kp_ref, kc_ref, vp_ref, vc_ref, add_ref, o_ref, lse_ref):
    i = pl.program_id(2)
    q = q_ref[0, 0]
    kk = jnp.concatenate([kp_ref[0, 0], kc_ref[0, 0]], axis=0)
    vv = jnp.concatenate([vp_ref[0, 0], vc_ref[0, 0]], axis=0)
    lane = lax.broadcasted_iota(jnp.int32, (BAND, LANES), 1)
    kcol = lax.broadcasted_iota(jnp.int32, (BAND, 2 * BAND), 1)
    no_prev = kcol < jnp.where(i == 0, BAND, 0)
    for pair in range(HEADS // 2):
        sl = slice(pair * LANES, (pair + 1) * LANES)
        qp, kp, vp = q[:, sl].astype(F32), kk[:, sl], vv[:, sl]
        o_pair = jnp.zeros((BAND, LANES), F32)
        l_pair = jnp.zeros((BAND, LANES), F32)
        for hh in range(2):
            mine = (lane >= HEAD_DIM) if hh else (lane < HEAD_DIM)
            s = _nt_dot(jnp.where(mine, qp, 0.0).astype(BF16), kp) + add_ref[pair * 2 + hh]
            s = jnp.where(no_prev, NEG_INF, s)
            m = jnp.max(s, axis=-1, keepdims=True)
            p = jnp.exp(s - m)
            den = jnp.sum(p, axis=-1, keepdims=True)
            o = jnp.dot(p.astype(BF16), vp, preferred_element_type=F32) / den
            o_pair = jnp.where(mine, o, o_pair)
            l_pair = jnp.where(mine, m + jnp.log(den), l_pair)
        o_ref[0, 0, :, sl] = o_pair.astype(o_ref.dtype)
        lse_ref[0, 0, :, sl] = l_pair


def _prompt_attention(q4, k4, v4, add, g):
    b, dil, n, _ = q4.shape
    cur = pl.BlockSpec((1, 1, BAND, GROUP_W), lambda bi, r, i: (bi, r, i, 0))
    prev = pl.BlockSpec((1, 1, BAND, GROUP_W), lambda bi, r, i: (bi, r, jnp.maximum(i - 1, 0), 0))
    return pl.pallas_call(
        _attn_kernel,
        out_shape=(jax.ShapeDtypeStruct(q4.shape, BF16), jax.ShapeDtypeStruct(q4.shape, F32)),
        grid=(b, dil, n // BAND),
        in_specs=[cur, prev, cur, prev, cur, _resident(add.shape)],
        out_specs=(cur, cur),
        compiler_params=_cparams(("parallel", "parallel", "arbitrary")),
        name=f"attn_g{g}",
    )(q4, k4, k4, v4, v4, add)


def _sample_attn_kernel(q_ref, kn_ref, vn_ref, c0_ref, c1_ref, c2_ref, a0_ref, a1_ref, a2_ref, addn_ref,
                        o_ref, lse_ref):
    rounded = lambda a: a.astype(BF16).astype(F32)
    head_of_lane = lax.broadcasted_iota(jnp.int32, (HEADS, GROUP_W), 1) // HEAD_DIM
    mine = head_of_lane == lax.broadcasted_iota(jnp.int32, (HEADS, GROUP_W), 0)
    for g, (c_ref, a_ref) in enumerate(((c0_ref, a0_ref), (c1_ref, a1_ref), (c2_ref, a2_ref))):
        cols = slice(g * GROUP_W, (g + 1) * GROUP_W)
        ln = c_ref.shape[-1]
        qm = jnp.where(mine, jnp.broadcast_to(q_ref[0, :, cols], (HEADS, GROUP_W)), 0.0)
        kt = c_ref[0].reshape(GROUP_W, ln).astype(BF16)
        vt = c_ref[1].reshape(GROUP_W, ln).astype(BF16)
        kn, vn = rounded(kn_ref[0, :, cols]), rounded(vn_ref[0, :, cols])
        s_c = jnp.dot(qm.astype(BF16), kt, preferred_element_type=F32) + a_ref[...]
        s_n = jnp.sum(qm * kn, axis=-1, keepdims=True) + addn_ref[g]
        m = jnp.maximum(jnp.max(s_c, axis=-1, keepdims=True), s_n)
        p_c = jnp.exp(s_c - m)
        p_n = jnp.exp(s_n - m)
        den = jnp.sum(p_c, axis=-1, keepdims=True) + p_n
        o_all = (_nt_dot(p_c.astype(BF16), vt) + rounded(p_n) * vn) / den
        lse = jnp.broadcast_to(m + jnp.log(den), (HEADS, GROUP_W))
        o_ref[0, :, cols] = jnp.sum(jnp.where(mine, o_all, 0.0), axis=0, keepdims=True)
        lse_ref[0, :, cols] = jnp.sum(jnp.where(mine, lse, 0.0), axis=0, keepdims=True)


def _sample_attention(q_s, k_s, v_s, caches_t, adds, addn):
    db = q_s.shape[0]
    tok = pl.BlockSpec((1, 1, QKV_W), lambda bi: (bi, 0, 0))
    cache_specs = [pl.BlockSpec((None,) + c.shape[1:], lambda bi: (bi, 0, 0, 0, 0)) for c in caches_t]
    shp = jax.ShapeDtypeStruct(q_s.shape, F32)
    return pl.pallas_call(
        _sample_attn_kernel,
        out_shape=(shp, shp),
        grid=(db,),
        in_specs=[tok, tok, tok] + cache_specs + [_resident(a.shape) for a in adds] + [_resident(addn.shape)],
        out_specs=(tok, tok),
        compiler_params=_cparams(("parallel",), V7X_VMEM_LIMIT),
        name="sample_attn",
    )(q_s, k_s, v_s, *caches_t, *adds, addn)


def _load_token_major(ref, scr_ref, dil):
    if dil == 1:
        return ref[0, 0].astype(F32)
    n = ref.shape[2]
    for r in range(dil):
        blk = ref[0, r].astype(F32)
        for s in range(SLABS):
            scr_ref[s, pl.ds(r, n, stride=dil), :] = blk[:, s * LANES:(s + 1) * LANES]
    return jnp.concatenate([scr_ref[s] for s in range(SLABS)], axis=1)


def _merge_tail(x, a_out, b_out, ga, gb, wua_ref, wub_ref, wo_ref, g2_ref, wq_ref):
    up_a = jnp.dot(a_out.astype(BF16), wua_ref[...], preferred_element_type=F32)
    up_b = jnp.dot(b_out.astype(BF16), wub_ref[...], preferred_element_type=F32)
    mrg = ga.astype(F32) * up_a + gb.astype(F32) * up_b
    x1 = x + jnp.dot(mrg.astype(BF16), wo_ref[...], preferred_element_type=F32)
    h2 = _rms(x1, g2_ref[...]).astype(BF16)
    return x1, h2, jnp.dot(h2, wq_ref[...], preferred_element_type=F32)


def _combine_groups(os, ls):
    m = jnp.maximum(jnp.maximum(ls[0], ls[1]), ls[2])
    es = [jnp.exp(l - m) for l in ls]
    return (es[0] * os[0] + es[1] * os[1] + es[2] * os[2]) / (es[0] + es[1] + es[2])


def _merge_prompt_kernel(x_ref, o0_ref, o1_ref, o2_ref, l0_ref, l1_ref, l2_ref, ub_ref, vb_ref, ga_ref, gb_ref,
                         ws_ref, bs_ref, wua_ref, wub_ref, wo_ref, g2_ref, wq_ref,
                         x1_ref, h2_ref, pq_ref, scr_ref):
    tm = x_ref.shape[1]
    os = [_load_token_major(r, scr_ref, DILATIONS[g]) for g, r in enumerate((o0_ref, o1_ref, o2_ref))]
    ls = [_load_token_major(r, scr_ref, DILATIONS[g]) for g, r in enumerate((l0_ref, l1_ref, l2_ref))]
    a_out = _combine_groups(os, ls)

    gw = D_MODEL // B_GROUPS
    tri = lax.broadcasted_iota(jnp.int32, (CHUNK, CHUNK), 0) >= lax.broadcasted_iota(jnp.int32, (CHUNK, CHUNK), 1)
    parts = []
    for c in range(tm // CHUNK):
        rows = slice(c * CHUNK, (c + 1) * CHUNK)
        cols = []
        for g in range(B_GROUPS):
            w = jnp.where(tri, ws_ref[g], 0.0).astype(BF16)
            cols.append(jnp.dot(w, vb_ref[0, rows, g * gw:(g + 1) * gw], preferred_element_type=F32))
        mixed = jnp.concatenate(cols, axis=1) + bs_ref[...]
        parts.append(ub_ref[0, rows, :].astype(F32) * mixed)
    b_out = jnp.concatenate(parts, axis=0)

    x1, h2, pq = _merge_tail(x_ref[0], a_out, b_out, ga_ref[0], gb_ref[0], wua_ref, wub_ref, wo_ref, g2_ref, wq_ref)
    x1_ref[0] = x1
    h2_ref[0] = h2
    for h in range(PEER_HEADS):
        pq_ref[h, 0] = pq[:, h * 2 * PEER_HALF:(h + 1) * 2 * PEER_HALF].astype(pq_ref.dtype)


def _merge_sample_kernel(x_ref, o0_ref, o1_ref, o2_ref, l0_ref, l1_ref, l2_ref, ub_ref, vb_ref, ga_ref, gb_ref,
                         ws_ref, bs_ref, wua_ref, wub_ref, wo_ref, g2_ref, wq_ref, x1_ref, h2_ref, pq_ref):
    a_out = _combine_groups([o0_ref[...], o1_ref[...], o2_ref[...]], [l0_ref[...], l1_ref[...], l2_ref[...]])
    b_out = ub_ref[...].astype(F32) * (vb_ref[...] * ws_ref[...] + bs_ref[...])
    x1, h2, pq = _merge_tail(x_ref[...], a_out, b_out, ga_ref[...], gb_ref[...], wua_ref, wub_ref, wo_ref, g2_ref, wq_ref)
    x1_ref[...] = x1
    h2_ref[...] = h2
    for h in range(PEER_HEADS):
        pq_ref[h] = pq[:, h * 2 * PEER_HALF:(h + 1) * 2 * PEER_HALF].astype(pq_ref.dtype)


def _merge_prompt(x, o, lse, ub, vb, ga, gb, ws, bs, wua, wub, wo, g2, wq, tm):
    b, s, _ = x.shape
    row = lambda w: pl.BlockSpec((1, tm, w), lambda bi, ti: (bi, ti, 0))
    dilated = [pl.BlockSpec((1, d, tm // d, GROUP_W), lambda bi, ti: (bi, 0, ti, 0)) for d in DILATIONS]
    weights = (ws, bs, wua, wub, wo, g2, wq)
    return pl.pallas_call(
        _merge_prompt_kernel,
        out_shape=(jax.ShapeDtypeStruct((b, s, D_MODEL), F32), jax.ShapeDtypeStruct((b, s, D_MODEL), BF16),
                   jax.ShapeDtypeStruct((PEER_HEADS, b, s, 2 * PEER_HALF), BF16)),
        grid=(b, s // tm),
        in_specs=[row(D_MODEL)] + dilated + dilated + [row(D_MODEL)] * 4
                 + [pl.BlockSpec(w.shape, functools.partial(lambda *_, nd: (0,) * nd, nd=w.ndim),
                                 pipeline_mode=pl.Buffered(1)) for w in weights],
        out_specs=(row(D_MODEL), row(D_MODEL),
                   pl.BlockSpec((PEER_HEADS, 1, tm, 2 * PEER_HALF), lambda bi, ti: (0, bi, ti, 0))),
        scratch_shapes=[pltpu.VMEM((SLABS, tm, LANES), F32)],
        compiler_params=_cparams(("parallel", "parallel"), V7X_VMEM_LIMIT),
        name="merge",
    )(x, *o, *lse, ub, vb, ga, gb, *weights)


def _merge_sample(x2d, o, lse, ub, vb, ga, gb, ws0, bs0, wua, wub, wo, g2, wq):
    t = x2d.shape[0]
    row = lambda w: pl.BlockSpec((t, w), lambda i: (0, 0))
    weights = (ws0, bs0, wua, wub, wo, g2, wq)
    pq_shape = (PEER_HEADS, t, 2 * PEER_HALF)
    return pl.pallas_call(
        _merge_sample_kernel,
        out_shape=(jax.ShapeDtypeStruct((t, D_MODEL), F32), jax.ShapeDtypeStruct((t, D_MODEL), BF16),
                   jax.ShapeDtypeStruct(pq_shape, BF16)),
        grid=(1,),
        in_specs=[row(D_MODEL)] + [row(GROUP_W)] * 6 + [row(D_MODEL)] * 4 + [_resident(w.shape) for w in weights],
        out_specs=(row(D_MODEL), row(D_MODEL), _resident(pq_shape)),
        compiler_params=_cparams(("arbitrary",), V7X_VMEM_LIMIT),
        name="merge_sample",
    )(x2d, *o, *lse, ub, vb, ga, gb, *weights)


def _top16_rows(s, idx_col):
    vals, idxs = [], []
    big = jnp.float32(1e9)
    for _ in range(PEER_TOPK):
        m = jnp.max(s, axis=0, keepdims=True)
        sel = jnp.min(jnp.where(s == m, idx_col, big), axis=0, keepdims=True)
        s = jnp.where(idx_col == sel, -jnp.inf, s)
        vals.append(m)
        idxs.append(sel)
    return vals, idxs


def _route_chunk(qh, sk1, sk2):
    key_col = lax.broadcasted_iota(jnp.int32, (N_KEYS, LANES), 0).astype(F32)
    row8 = lax.broadcasted_iota(jnp.int32, (8, LANES), 0).astype(F32)
    row16 = lax.broadcasted_iota(jnp.int32, (PEER_TOPK, LANES), 0).astype(F32)
    s1 = _nt_dot(sk1, qh[:, :PEER_HALF])
    s2 = _nt_dot(sk2, qh[:, PEER_HALF:])
    v1, i1 = _top16_rows(s1, key_col)
    v2, i2 = _top16_rows(s2, key_col)
    v2a = jnp.concatenate(v2, axis=0)
    i2a = jnp.concatenate(i2, axis=0)
    cand = [v1[0] + v2a]
    flat = [row16]
    code = [i1[0] * N_KEYS + i2a]
    for j1 in range(1, 8):
        cand.append(v1[j1] + v2a[:8])
        flat.append(row8 + float(j1 * PEER_TOPK))
        code.append(i1[j1] * N_KEYS + i2a[:8])
    cand.append(jnp.concatenate(v1[8:], axis=0) + v2[0])
    flat.append((row8 + 8.0) * float(PEER_TOPK))
    code.append(jnp.concatenate(i1[8:], axis=0) * N_KEYS + i2[0])
    cand = jnp.concatenate(cand, axis=0)
    flat = jnp.concatenate(flat, axis=0)
    code = jnp.concatenate(code, axis=0)
    top_s, top_e = [], []
    for _ in range(PEER_TOPK):
        m = jnp.max(cand, axis=0, keepdims=True)
        sel = jnp.min(jnp.where(cand == m, flat, 1e9), axis=0, keepdims=True)
        hit = flat == sel
        top_e.append(jnp.max(jnp.where(hit, code, -1.0), axis=0, keepdims=True))
        cand = jnp.where(hit, -jnp.inf, cand)
        top_s.append(m)
    ts = jnp.concatenate(top_s, axis=0)
    te = jnp.concatenate(top_e, axis=0)
    e = jnp.exp(ts - ts[0:1])
    gate = e / jnp.sum(e, axis=0, keepdims=True)
    a = jnp.floor(te * (1.0 / N_KEYS))
    return a, te - a * N_KEYS, gate


def _route_kernel(pq_ref, sk1_ref, sk2_ref, i1_ref, i2_ref, gate_ref):
    for c in range(pq_ref.shape[0] // LANES):
        rows = slice(c * LANES, (c + 1) * LANES)
        a, b, gate = _route_chunk(pq_ref[rows, :], sk1_ref[...], sk2_ref[...])
        i1_ref[:, rows] = a
        i2_ref[:, rows] = b
        gate_ref[:, rows] = gate


def _route(pq3, sk1_bf, sk2_bf, tm, n_tokens):
    out = pl.BlockSpec((PEER_TOPK, tm), lambda i, h: (h, i))
    shp = jax.ShapeDtypeStruct((PEER_HEADS * PEER_TOPK, n_tokens), F32)
    return pl.pallas_call(
        _route_kernel,
        out_shape=(shp, shp, shp),
        grid=(n_tokens // tm, PEER_HEADS),
        in_specs=[pl.BlockSpec((None, tm, 2 * PEER_HALF), lambda i, h: (h, i, 0)),
                  _resident(sk1_bf.shape), _resident(sk2_bf.shape)],
        out_specs=(out, out, out),
        compiler_params=_cparams(("parallel", "parallel")),
        name="peer_route",
    )(pq3, sk1_bf, sk2_bf)


def _peer_kernel(h2_ref, x1_ref, i10_ref, i20_ref, gt0_ref, pqn_ref, sk1_ref, sk2_ref, ed_ref, eu_ref,
                 x2_ref, pk_ref, acc_ref, rows_ref, picks_ref, *, phase_units, units_per_step):
    tm = h2_ref.shape[0]
    i = pl.program_id(0)
    j = pl.program_id(1)
    half = N_KEYS // 2
    n_chunks = tm // LANES

    def route_unit(u):
        c = u // PEER_HEADS
        h = u % PEER_HEADS
        qh = pqn_ref[h, pl.ds(pl.multiple_of(c * LANES, LANES), LANES), :]
        picks = _route_chunk(qh, sk1_ref[...], sk2_ref[...])
        r0 = pl.multiple_of(h * PEER_TOPK, PEER_TOPK)
        for q in range(3):
            picks_ref[q, c, pl.ds(r0, PEER_TOPK), :] = picks[q]

    @pl.when(j == 0)
    def _start_tile():
        @pl.when(i == 0)
        def _():
            for q, src in enumerate((i10_ref, i20_ref, gt0_ref)):
                for c in range(n_chunks):
                    rows_ref[q, c * LANES:(c + 1) * LANES, :] = src[:, c * LANES:(c + 1) * LANES].T

        @pl.when(i > 0)
        def _():
            for q in range(3):
                for c in range(n_chunks):
                    rows_ref[q, c * LANES:(c + 1) * LANES, :] = picks_ref[q, c].T

        acc_ref[...] = jnp.zeros_like(acc_ref)
        sub = lax.broadcasted_iota(jnp.int32, (N_KEYS, N_KEYS), 0)
        b_of_row = sub.astype(F32)
        a_of_row = jnp.where(sub < half, 2 * sub, 2 * (sub - half) + 1).astype(F32)
        tok = tm // phase_units

        def body(u, carry):
            route_unit(u)
            for t in range(tok):
                tt = u * tok + t
                r1 = rows_ref[0, pl.ds(tt, 1), :]
                r2 = rows_ref[1, pl.ds(tt, 1), :]
                rg = rows_ref[2, pl.ds(tt, 1), :]
                oat = jnp.where(a_of_row == r1, 1.0, 0.0).astype(BF16)
                gbt = jnp.where(b_of_row == r2, rg, 0.0).astype(BF16)
                g = _nt_dot(oat, gbt)
                packed = pltpu.pack_elementwise([g[:half], g[half:]], packed_dtype=BF16)
                pk_ref[pl.ds(pl.multiple_of(tt * G_PITCH, 8), half), :] = packed
            return carry

        lax.fori_loop(0, phase_units, body, 0)

    for k in range(units_per_step):
        route_unit(phase_units + j * units_per_step + k)
    gs = []
    for r in range(PAIRS_PER_STEP):
        w = pk_ref[pl.ds(j * PAIRS_PER_STEP + r, tm, stride=G_PITCH), :]
        gs.append(pltpu.unpack_elementwise(w, index=0, packed_dtype=BF16, unpacked_dtype=F32))
        gs.append(pltpu.unpack_elementwise(w, index=1, packed_dtype=BF16, unpacked_dtype=F32))
    gates = jnp.concatenate(gs, axis=1)
    s = _nt_dot(h2_ref[...], ed_ref[...])
    act = (jax.nn.gelu(s) * gates).astype(BF16)
    acc_ref[...] += jnp.dot(act, eu_ref[...], preferred_element_type=F32)

    @pl.when(j == pl.num_programs(1) - 1)
    def _finish():
        x2_ref[...] = x1_ref[...] + acc_ref[...]


def _peer(h2, x1, picks0, pq3, sk1, sk2, ed, eu, tm):
    t = h2.shape[0]
    nt = t // tm
    nb = PAIRS_PER_STEP * 2 * N_KEYS
    steps = ed.shape[0] // nb
    n_units = (tm // LANES) * PEER_HEADS
    units_per_step = n_units // (2 * steps)
    phase_units = n_units - steps * units_per_step
    row = lambda w: pl.BlockSpec((tm, w), lambda i, j: (i, 0))
    table = pl.BlockSpec((nb, D_MODEL), lambda i, j: (j, 0))
    return pl.pallas_call(
        functools.partial(_peer_kernel, phase_units=phase_units, units_per_step=units_per_step),
        out_shape=jax.ShapeDtypeStruct((t, D_MODEL), F32),
        grid=(nt, steps),
        in_specs=[row(D_MODEL), row(D_MODEL)] + [_resident(p.shape) for p in picks0]
                 + [pl.BlockSpec((PEER_HEADS, tm, 2 * PEER_HALF), lambda i, j: (0, jnp.minimum(i + 1, nt - 1), 0)),
                    _resident(sk1.shape), _resident(sk2.shape), table, table],
        out_specs=row(D_MODEL),
        scratch_shapes=[pltpu.VMEM((tm * G_PITCH, N_KEYS), jnp.uint32),
                        pltpu.VMEM((tm, D_MODEL), F32),
                        pltpu.VMEM((3, tm, N_KEYS), F32),
                        pltpu.VMEM((3, tm // LANES, PEER_HEADS * PEER_TOPK, LANES), F32)],
        compiler_params=_cparams(("arbitrary", "arbitrary"), V7X_VMEM_LIMIT),
        name="peer",
    )(h2, x1, *picks0, pq3, sk1, sk2, ed, eu)


def _ple_kernel(x2_ref, p_ref, g3_ref, wpg_ref, wpp_ref, y_ref):
    x2 = x2_ref[...]
    h3 = _rms(x2, g3_ref[...]).astype(BF16)
    gate = jax.nn.sigmoid(jnp.dot(h3, wpg_ref[...], preferred_element_type=F32))
    y_ref[...] = x2 + gate * jnp.dot(p_ref[...].astype(BF16), wpp_ref[...], preferred_element_type=F32)


def _ple(x2, p2d, g3, wpg, wpp, tm):
    t = x2.shape[0]
    row = lambda w: pl.BlockSpec((tm, w), lambda i: (i, 0))
    return pl.pallas_call(
        _ple_kernel,
        out_shape=jax.ShapeDtypeStruct((t, D_MODEL), F32),
        grid=(t // tm,),
        in_specs=[row(D_MODEL), row(PLE_DIM), _resident(g3.shape), _resident(wpg.shape), _resident(wpp.shape)],
        out_specs=row(D_MODEL),
        compiler_params=_cparams(("parallel",)),
        name="ple",
    )(x2, p2d, g3, wpg, wpp)


def _t5_bucket(dist):
    max_exact = N_BUCKETS // 2
    df = jnp.maximum(dist, max_exact).astype(F32)
    large = max_exact + (jnp.log(df / max_exact) / math.log(MAX_DISTANCE / max_exact)
                         * (N_BUCKETS - max_exact)).astype(jnp.int32)
    return jnp.where(dist < max_exact, dist, jnp.minimum(large, N_BUCKETS - 1))


def _bias_tables(rel_bias, g):
    dil = DILATIONS[g]
    hi = lax.Precision.HIGHEST
    bucket = _t5_bucket(jnp.arange(BAND + 1) * dil)
    per_dist = rel_bias[:, g * HEADS:(g + 1) * HEADS][bucket].astype(F32)
    dsub = np.arange(BAND)[:, None] + BAND - np.arange(2 * BAND)[None, :]
    valid = (dsub >= 0) & (dsub <= WINDOWS[g] // dil)
    spread = (jnp.asarray(np.clip(dsub, 0, BAND).reshape(1, -1)) == jnp.arange(BAND + 1)[:, None]).astype(F32)
    band = jnp.dot(per_dist.T, spread, precision=hi).reshape(HEADS, BAND, 2 * BAND)
    add = jnp.where(jnp.asarray(valid)[None], band, NEG_INF)
    addc = per_dist[::-1][:BAND].T
    addn = per_dist[0][:, None]
    return add, addc, addn


def kernel(x_prompt, x_sample, cache_kv_w128, cache_kv_w512, cache_kv_w2048, p_prompt, p_sample, rel_bias, norm1,
           w_in, q_norm, k_norm, b_v_norm, w_spatial, b_spatial, w_up_a, w_up_b, w_out, norm2, w_query,
           sub_keys_1, sub_keys_2, expert_down, expert_up, norm3, w_ple_gate, w_ple_proj):
    bsz, seq, _ = x_prompt.shape
    db, ds, _ = x_sample.shape
    depth = norm1.shape[0]
    assert depth == 1 and ds == 1 and seq % (BAND * DILATIONS[-1]) == 0 and seq >= WINDOWS[-1]
    caches = (cache_kv_w128, cache_kv_w512, cache_kv_w2048)
    for g in range(N_GROUPS):
        assert caches[g].shape[2] == WINDOWS[g]
    t_p = bsz * seq
    t_s = 128
    li = 0

    g1 = norm1[li][None, :]
    w_in_bf = w_in[li].astype(BF16)
    qg = jnp.tile(q_norm[li], (1, HEADS))
    kg = jnp.tile(k_norm[li], (1, HEADS))
    bvg = b_v_norm[li][None, :]
    blk = np.arange(V7X_MXU_DIM) // HEAD_DIM
    hsum = jnp.asarray(blk[:, None] == blk[None, :], dtype=BF16)
    wua, wub, wo = w_up_a[li].astype(BF16), w_up_b[li].astype(BF16), w_out[li].astype(BF16)
    g2, g3 = norm2[li][None, :], norm3[li][None, :]
    wq = w_query[li].astype(BF16)
    sk1, sk2 = sub_keys_1[li].astype(BF16), sub_keys_2[li].astype(BF16)
    ed = expert_down[li].astype(BF16)
    eu = expert_up[li].astype(BF16)
    wpg, wpp = w_ple_gate[li].astype(BF16), w_ple_proj[li].astype(BF16)
    bs_full = jnp.repeat(b_spatial[li].T, D_MODEL // B_GROUPS, axis=1)
    tables = [_bias_tables(rel_bias, g) for g in range(N_GROUPS)]

    res = _in_projection_prompt(x_prompt, g1, w_in_bf, qg, kg, bvg, hsum, TM_INPROJ)
    qs, ks, vs, tails = res[0:3], res[3:6], res[6:9], res[9:12]
    ub, vb, ga, gb = res[12:16]
    outs, lses, kv_prompt = [], [], []
    for g in range(N_GROUPS):
        o, lse = _prompt_attention(qs[g], ks[g], vs[g], tables[g][0], g)
        outs.append(o)
        lses.append(lse)
        kv_prompt.append(tails[g].reshape(1, bsz, tails[g].shape[1], 2, HEADS, HEAD_DIM))
    x1, h2, pq = _merge_prompt(x_prompt, outs, lses, ub, vb, ga, gb, w_spatial[li], bs_full, wua, wub, wo, g2, wq,
                               TM_INPROJ)
    x1, h2 = x1.reshape(t_p, D_MODEL), h2.reshape(t_p, D_MODEL)
    pq = pq.reshape(PEER_HEADS, t_p, 2 * PEER_HALF)
    picks0 = _route(pq, sk1, sk2, TM_PROJ, TM_PEER)
    x2 = _peer(h2, x1, picks0, pq, sk1, sk2, ed, eu, TM_PEER)
    y_prompt = _ple(x2, p_prompt[li].reshape(t_p, PLE_DIM), g3, wpg, wpp, TM_PEER).reshape(bsz, seq, D_MODEL)

    xs = jnp.pad(x_sample.reshape(db, D_MODEL), ((0, t_s - db), (0, 0)))
    q, k, v, ub, vb, ga, gb = _in_projection_sample(xs, g1, w_in_bf, qg, kg, bvg, hsum)
    nh = N_GROUPS * HEADS
    k3, v3 = k[:db].reshape(db, nh, HEAD_DIM), v[:db].reshape(db, nh, HEAD_DIM)
    caches_t = [jnp.transpose(caches[g], (0, 1, 3, 4, 5, 2))[li] for g in range(N_GROUPS)]
    adds = []
    for g in range(N_GROUPS):
        dil = DILATIONS[g]
        full = jnp.full((HEADS, BAND, dil), NEG_INF, F32).at[:, :, 0].set(tables[g][1])
        adds.append(full.reshape(HEADS, BAND * dil))
    addn = jnp.stack([tables[g][2] for g in range(N_GROUPS)])
    o3, lse3 = _sample_attention(q[:db, None, :].astype(F32), k[:db, None, :], v[:db, None, :], caches_t, adds, addn)
    pad = lambda a: jnp.pad(a, ((0, t_s - db), (0, 0)))
    o2, lse2 = o3.reshape(db, QKV_W), lse3.reshape(db, QKV_W)
    outs = [pad(o2[:, g * GROUP_W:(g + 1) * GROUP_W]) for g in range(N_GROUPS)]
    lses = [pad(lse2[:, g * GROUP_W:(g + 1) * GROUP_W]) for g in range(N_GROUPS)]
    kv_sample = [jnp.stack([k3[:, None, g * HEADS:(g + 1) * HEADS], v3[:, None, g * HEADS:(g + 1) * HEADS]],
                           axis=2)[None] for g in range(N_GROUPS)]
    ws0 = jnp.repeat(w_spatial[li][:, 0, 0], D_MODEL // B_GROUPS)[None, :]
    bs0 = bs_full[0:1, :]
    x1, h2, pq = _merge_sample(xs, outs, lses, ub, vb, ga, gb, ws0, bs0, wua, wub, wo, g2, wq)
    picks0 = _route(pq, sk1, sk2, t_s, t_s)
    ps = jnp.pad(p_sample[li].reshape(db, PLE_DIM), ((0, t_s - db), (0, 0)))
    x2 = _peer(h2, x1, picks0, pq, sk1, sk2, ed, eu, t_s)
    y_sample = _ple(x2, ps, g3, wpg, wpp, t_s)[:db].reshape(db, ds, D_MODEL)
    v_sample_chunk = vb[:db].reshape(1, db, ds, D_MODEL)

    return (y_prompt, y_sample, kv_prompt[0], kv_prompt[1], kv_prompt[2],
            kv_sample[0], kv_sample[1], kv_sample[2], v_sample_chunk)
```

```python
import functools
import math

import numpy as np
import jax
import jax.numpy as jnp
from jax import lax
from jax.experimental import pallas as pl
from jax.experimental.pallas import tpu as pltpu

F32 = jnp.float32
BF16 = jnp.bfloat16

D_MODEL = 1024
N_GROUPS = 3
HEADS = 8
HEAD_DIM = 64
GROUP_W = HEADS * HEAD_DIM
QKV_W = N_GROUPS * GROUP_W
WINDOWS = (128, 512, 2048)
DILATIONS = (1, 4, 16)
BAND = 128
N_BUCKETS = 32
MAX_DISTANCE = 2048
CHUNK = 128
B_GROUPS = 4
N_KEYS = 128
PEER_HEADS = 8
PEER_TOPK = 16
PEER_HALF = 128
PLE_DIM = 256
EPS = 1e-6
NEG_INF = -1e30
IN_OFFS = (0, QKV_W, 2 * QKV_W, 3 * QKV_W, 3 * QKV_W + D_MODEL, 3 * QKV_W + 2 * D_MODEL,
           3 * QKV_W + 3 * D_MODEL, 3 * QKV_W + 4 * D_MODEL)

LANES = 128
SLABS = GROUP_W // LANES
V7X_VMEM_LIMIT = 56 * 1024 * 1024
V7X_MXU_DIM = 256
G_PITCH = 72
TM_PROJ = 256
TM_INPROJ = 512
ATTN_Q_ROWS = (512, 256, 128)
TM_PEER = 512
PAIRS_PER_STEP = 4

def _cparams(sem, vmem=None):
    return pltpu.CompilerParams(dimension_semantics=sem, vmem_limit_bytes=vmem)


def _resident(shape):
    nd = len(shape)
    return pl.BlockSpec(shape, lambda *_: (0,) * nd)


def _nt_dot(a, b):
    return lax.dot_general(a, b, (((1,), (1,)), ((), ())), preferred_element_type=F32)


def _rms(x, gain):
    return x * lax.rsqrt(jnp.mean(x * x, axis=-1, keepdims=True) + EPS) * gain


def _head_norm(y, hsum_ref, gain_row):
    y2 = (y * y).astype(BF16)
    w = hsum_ref.shape[0]
    ss = jnp.concatenate([jnp.dot(y2[:, c:c + w], hsum_ref[...], preferred_element_type=F32)
                          for c in range(0, GROUP_W, w)], axis=1)
    return y * lax.rsqrt(ss * (1.0 / HEAD_DIM) + EPS) * gain_row


def _store_residue_major(out_ref, y, scr_ref, dil):
    if dil == 1:
        out_ref[0, 0] = y.astype(out_ref.dtype)
        return
    n = y.shape[0] // dil
    for s in range(SLABS):
        scr_ref[s] = y[:, s * LANES:(s + 1) * LANES]
    for r in range(dil):
        for s in range(SLABS):
            out_ref[0, r, :, s * LANES:(s + 1) * LANES] = scr_ref[s, pl.ds(r, n, stride=dil), :].astype(out_ref.dtype)


def _gmlp_and_gates(proj, bvg_ref, ub_ref, vb_ref, ga_ref, gb_ref, st):
    half = D_MODEL // 2
    for c in range(2):
        st(ub_ref, c * half, jax.nn.gelu(proj(IN_OFFS[3] + c * half, half)))
    vb0 = jax.nn.gelu(proj(IN_OFFS[4], half))
    vb1 = jax.nn.gelu(proj(IN_OFFS[4] + half, half))
    ms = (jnp.sum(vb0 * vb0, axis=-1, keepdims=True) + jnp.sum(vb1 * vb1, axis=-1, keepdims=True)) * (1.0 / D_MODEL)
    r = lax.rsqrt(ms + EPS)
    st(vb_ref, 0, vb0 * r * bvg_ref[:, :half])
    st(vb_ref, half, vb1 * r * bvg_ref[:, half:])
    for c in range(2):
        st(ga_ref, c * half, jax.nn.sigmoid(proj(IN_OFFS[5] + c * half, half)))
        st(gb_ref, c * half, jax.nn.sigmoid(proj(IN_OFFS[6] + c * half, half)))


def _inproj_prompt_kernel(x_ref, g1_ref, w_ref, qg_ref, kg_ref, bvg_ref, hsum_ref,
                          q0_ref, q1_ref, q2_ref, k0_ref, k1_ref, k2_ref, v0_ref, v1_ref, v2_ref,
                          t0_ref, t1_ref, t2_ref, ub_ref, vb_ref, ga_ref, gb_ref, scr_ref, *, tail_first):
    ti = pl.program_id(1)
    tm = x_ref.shape[1]
    hb = _rms(x_ref[0], g1_ref[...]).astype(BF16)

    def proj(c0, width):
        return jnp.dot(hb, w_ref[:, c0:c0 + width], preferred_element_type=F32)

    q_refs, k_refs, v_refs, t_refs = (q0_ref, q1_ref, q2_ref), (k0_ref, k1_ref, k2_ref), (v0_ref, v1_ref, v2_ref), \
        (t0_ref, t1_ref, t2_ref)
    for g in range(N_GROUPS):
        lo = g * GROUP_W
        dil = DILATIONS[g]
        qn = _head_norm(proj(IN_OFFS[0] + lo, GROUP_W), hsum_ref, qg_ref[g:g + 1, :]) * (HEAD_DIM ** -0.5)
        _store_residue_major(q_refs[g], qn, scr_ref, dil)
        kn = _head_norm(proj(IN_OFFS[1] + lo, GROUP_W), hsum_ref, kg_ref[g:g + 1, :])
        _store_residue_major(k_refs[g], kn, scr_ref, dil)
        vv = proj(IN_OFFS[2] + lo, GROUP_W)
        _store_residue_major(v_refs[g], vv, scr_ref, dil)
        rows = t_refs[g].shape[1]

        @pl.when(ti >= tail_first[g])
        def _():
            t_refs[g][0, :, :GROUP_W] = kn[tm - rows:, :]
            t_refs[g][0, :, GROUP_W:] = vv[tm - rows:, :]

    def st(ref, c0, val):
        ref[0, :, c0:c0 + val.shape[1]] = val.astype(ref.dtype)

    _gmlp_and_gates(proj, bvg_ref, ub_ref, vb_ref, ga_ref, gb_ref, st)


def _inproj_sample_kernel(x_ref, g1_ref, w_ref, qg_ref, kg_ref, bvg_ref, hsum_ref,
                          q_ref, k_ref, v_ref, ub_ref, vb_ref, ga_ref, gb_ref):
    hb = _rms(x_ref[...], g1_ref[...]).astype(BF16)

    def proj(c0, width):
        return jnp.dot(hb, w_ref[:, c0:c0 + width], preferred_element_type=F32)

    for g in range(N_GROUPS):
        lo = g * GROUP_W
        qn = _head_norm(proj(IN_OFFS[0] + lo, GROUP_W), hsum_ref, qg_ref[g:g + 1, :]) * (HEAD_DIM ** -0.5)
        q_ref[:, lo:lo + GROUP_W] = qn.astype(q_ref.dtype)
        k_ref[:, lo:lo + GROUP_W] = _head_norm(proj(IN_OFFS[1] + lo, GROUP_W), hsum_ref, kg_ref[g:g + 1, :])
        v_ref[:, lo:lo + GROUP_W] = proj(IN_OFFS[2] + lo, GROUP_W)

    def st(ref, c0, val):
        ref[:, c0:c0 + val.shape[1]] = val.astype(ref.dtype)

    _gmlp_and_gates(proj, bvg_ref, ub_ref, vb_ref, ga_ref, gb_ref, st)


def _proj_weight_specs(w_in_bf):
    return [_resident((1, D_MODEL)),
            pl.BlockSpec(w_in_bf.shape, lambda *_: (0, 0), pipeline_mode=pl.Buffered(1)),
            _resident((N_GROUPS, GROUP_W)), _resident((N_GROUPS, GROUP_W)), _resident((1, D_MODEL)),
            _resident((V7X_MXU_DIM, V7X_MXU_DIM))]


def _in_projection_prompt(x, g1, w_in_bf, qg, kg, bvg, hsum, tm):
    b, s, _ = x.shape
    nt = s // tm
    row = pl.BlockSpec((1, tm, D_MODEL), lambda bi, ti: (bi, ti, 0))
    qkv_shapes, qkv_specs, tail_shapes, tail_specs, tail_first = [], [], [], [], []
    for g in range(N_GROUPS):
        dil = DILATIONS[g]
        qkv_shapes.append(jax.ShapeDtypeStruct((b, dil, s // dil, GROUP_W), BF16))
        qkv_specs.append(pl.BlockSpec((1, dil, tm // dil, GROUP_W), lambda bi, ti: (bi, 0, ti, 0)))
        ln = min(WINDOWS[g], s)
        rows = min(ln, tm)
        first = (s - ln) // tm if ln >= tm else nt - 1
        tail_first.append(first)
        tail_shapes.append(jax.ShapeDtypeStruct((b, ln, 2 * GROUP_W), F32))
        tail_specs.append(pl.BlockSpec((1, rows, 2 * GROUP_W),
                                       functools.partial(lambda bi, ti, f: (bi, jnp.maximum(ti - f, 0), 0), f=first)))
    act = jax.ShapeDtypeStruct((b, s, D_MODEL), BF16)
    gate = jax.ShapeDtypeStruct((b, s, D_MODEL), BF16)
    return pl.pallas_call(
        functools.partial(_inproj_prompt_kernel, tail_first=tuple(tail_first)),
        out_shape=tuple(qkv_shapes * 3 + tail_shapes + [act, act, gate, gate]),
        grid=(b, nt),
        in_specs=[row] + _proj_weight_specs(w_in_bf),
        out_specs=tuple(qkv_specs * 3 + tail_specs + [row] * 4),
        scratch_shapes=[pltpu.VMEM((SLABS, tm, LANES), F32)],
        compiler_params=_cparams(("arbitrary", "arbitrary"), V7X_VMEM_LIMIT),
        name="in_projection",
    )(x, g1, w_in_bf, qg, kg, bvg, hsum)


def _in_projection_sample(x2d, g1, w_in_bf, qg, kg, bvg, hsum):
    t = x2d.shape[0]
    row = lambda w: pl.BlockSpec((t, w), lambda i: (0, 0))
    outs = (jax.ShapeDtypeStruct((t, QKV_W), BF16), jax.ShapeDtypeStruct((t, QKV_W), F32),
            jax.ShapeDtypeStruct((t, QKV_W), F32), jax.ShapeDtypeStruct((t, D_MODEL), BF16),
            jax.ShapeDtypeStruct((t, D_MODEL), F32), jax.ShapeDtypeStruct((t, D_MODEL), BF16),
            jax.ShapeDtypeStruct((t, D_MODEL), BF16))
    return pl.pallas_call(
        _inproj_sample_kernel,
        out_shape=outs,
        grid=(1,),
        in_specs=[row(D_MODEL)] + _proj_weight_specs(w_in_bf),
        out_specs=(row(QKV_W), row(QKV_W), row(QKV_W), row(D_MODEL), row(D_MODEL), row(D_MODEL), row(D_MODEL)),
        compiler_params=_cparams(("arbitrary",), V7X_VMEM_LIMIT),
        name="in_projection_sample",
    )(x2d, g1, w_in_bf, qg, kg, bvg, hsum)


def _attn_kernel(q_ref, kp_ref, kc_ref, vp_ref, vc_ref, add_ref, o_ref, lse_ref):
    i = pl.program_id(2)
    kall = jnp.concatenate([kp_ref[0, 0], kc_ref[0, 0]], axis=0)
    vall = jnp.concatenate([vp_ref[0, 0], vc_ref[0, 0]], axis=0)
    lane = lax.broadcasted_iota(jnp.int32, (BAND, LANES), 1)
    kcol = lax.broadcasted_iota(jnp.int32, (BAND, 2 * BAND), 1)
    no_prev = kcol < jnp.where(i == 0, BAND, 0)
    for sb in range(q_ref.shape[2] // BAND):
        rows = slice(sb * BAND, (sb + 1) * BAND)
        q = q_ref[0, 0, rows, :]
        kk, vv = kall[sb * BAND:(sb + 2) * BAND], vall[sb * BAND:(sb + 2) * BAND]
        for pair in range(HEADS // 2):
            sl = slice(pair * LANES, (pair + 1) * LANES)
            qp, kp, vp = q[:, sl].astype(F32), kk[:, sl], vv[:, sl]
            o_pair = jnp.zeros((BAND, LANES), F32)
            l_pair = jnp.zeros((BAND, LANES), F32)
            for hh in range(2):
                mine = (lane >= HEAD_DIM) if hh else (lane < HEAD_DIM)
                s = _nt_dot(jnp.where(mine, qp, 0.0).astype(BF16), kp) + add_ref[pair * 2 + hh]
                if sb == 0:
                    s = jnp.where(no_prev, NEG_INF, s)
                m = jnp.max(s, axis=-1, keepdims=True)
                p = jnp.exp(s - m)
                den = jnp.sum(p, axis=-1, keepdims=True)
                o = jnp.dot(p.astype(BF16), vp, preferred_element_type=F32) / den
                o_pair = jnp.where(mine, o, o_pair)
                l_pair = jnp.where(mine, m + jnp.log(den), l_pair)
            o_ref[0, 0, rows, sl] = o_pair.astype(o_ref.dtype)
            lse_ref[0, 0, rows, sl] = l_pair


def _prompt_attention(q4, k4, v4, add, g):
    b, dil, n, _ = q4.shape
    qb = next(c for c in ATTN_Q_ROWS if n % c == 0)
    per = qb // BAND
    cur = pl.BlockSpec((1, 1, qb, GROUP_W), lambda bi, r, i: (bi, r, i, 0))
    prev = pl.BlockSpec((1, 1, BAND, GROUP_W), lambda bi, r, i: (bi, r, jnp.maximum(i * per - 1, 0), 0))
    return pl.pallas_call(
        _attn_kernel,
        out_shape=(jax.ShapeDtypeStruct(q4.shape, BF16), jax.ShapeDtypeStruct(q4.shape, F32)),
        grid=(b, dil, n // qb),
        in_specs=[cur, prev, cur, prev, cur, _resident(add.shape)],
        out_specs=(cur, cur),
        compiler_params=_cparams(("parallel", "parallel", "arbitrary")),
        name=f"attn_g{g}",
    )(q4, k4, k4, v4, v4, add)


def _sample_attn_kernel(q_ref, kn_ref, vn_ref, c0_ref, c1_ref, c2_ref, a0_ref, a1_ref, a2_ref, addn_ref,
                        o_ref, lse_ref):
    rounded = lambda a: a.astype(BF16).astype(F32)
    head_of_lane = lax.broadcasted_iota(jnp.int32, (HEADS, GROUP_W), 1) // HEAD_DIM
    mine = head_of_lane == lax.broadcasted_iota(jnp.int32, (HEADS, GROUP_W), 0)
    for g, (c_ref, a_ref) in enumerate(((c0_ref, a0_ref), (c1_ref, a1_ref), (c2_ref, a2_ref))):
        cols = slice(g * GROUP_W, (g + 1) * GROUP_W)
        ln = c_ref.shape[-1]
        qm = jnp.where(mine, jnp.broadcast_to(q_ref[0, :, cols], (HEADS, GROUP_W)), 0.0)
        kt = c_ref[0].reshape(GROUP_W, ln).astype(BF16)
        vt = c_ref[1].reshape(GROUP_W, ln).astype(BF16)
        kn, vn = rounded(kn_ref[0, :, cols]), rounded(vn_ref[0, :, cols])
        s_c = jnp.dot(qm.astype(BF16), kt, preferred_element_type=F32) + a_ref[...]
        s_n = jnp.sum(qm * kn, axis=-1, keepdims=True) + addn_ref[g]
        m = jnp.maximum(jnp.max(s_c, axis=-1, keepdims=True), s_n)
        p_c = jnp.exp(s_c - m)
        p_n = jnp.exp(s_n - m)
        den = jnp.sum(p_c, axis=-1, keepdims=True) + p_n
        o_all = (_nt_dot(p_c.astype(BF16), vt) + rounded(p_n) * vn) / den
        lse = jnp.broadcast_to(m + jnp.log(den), (HEADS, GROUP_W))
        o_ref[0, :, cols] = jnp.sum(jnp.where(mine, o_all, 0.0), axis=0, keepdims=True)
        lse_ref[0, :, cols] = jnp.sum(jnp.where(mine, lse, 0.0), axis=0, keepdims=True)


def _sample_attention(q_s, k_s, v_s, caches_t, adds, addn):
    db = q_s.shape[0]
    tok = pl.BlockSpec((1, 1, QKV_W), lambda bi: (bi, 0, 0))
    cache_specs = [pl.BlockSpec((None,) + c.shape[1:], lambda bi: (bi, 0, 0, 0, 0)) for c in caches_t]
    shp = jax.ShapeDtypeStruct(q_s.shape, F32)
    return pl.pallas_call(
        _sample_attn_kernel,
        out_shape=(shp, shp),
        grid=(db,),
        in_specs=[tok, tok, tok] + cache_specs + [_resident(a.shape) for a in adds] + [_resident(addn.shape)],
        out_specs=(tok, tok),
        compiler_params=_cparams(("parallel",), V7X_VMEM_LIMIT),
        name="sample_attn",
    )(q_s, k_s, v_s, *caches_t, *adds, addn)


def _load_token_major(ref, scr_ref, dil):
    if dil == 1:
        return ref[0, 0].astype(F32)
    n = ref.shape[2]
    for r in range(dil):
        blk = ref[0, r].astype(F32)
        for s in range(SLABS):
            scr_ref[s, pl.ds(r, n, stride=dil), :] = blk[:, s * LANES:(s + 1) * LANES]
    return jnp.concatenate([scr_ref[s] for s in range(SLABS)], axis=1)


def _merge_tail(x, a_out, b_out, ga, gb, wua_ref, wub_ref, wo_ref, g2_ref, wq_ref):
    up_a = jnp.dot(a_out.astype(BF16), wua_ref[...], preferred_element_type=F32)
    up_b = jnp.dot(b_out.astype(BF16), wub_ref[...], preferred_element_type=F32)
    mrg = ga.astype(F32) * up_a + gb.astype(F32) * up_b
    x1 = x + jnp.dot(mrg.astype(BF16), wo_ref[...], preferred_element_type=F32)
    h2 = _rms(x1, g2_ref[...]).astype(BF16)
    return x1, h2, jnp.dot(h2, wq_ref[...], preferred_element_type=F32)


def _combine_groups(os, ls):
    m = jnp.maximum(jnp.maximum(ls[0], ls[1]), ls[2])
    es = [jnp.exp(l - m) for l in ls]
    return (es[0] * os[0] + es[1] * os[1] + es[2] * os[2]) / (es[0] + es[1] + es[2])


def _merge_prompt_kernel(x_ref, o0_ref, o1_ref, o2_ref, l0_ref, l1_ref, l2_ref, ub_ref, vb_ref, ga_ref, gb_ref,
                         ws_ref, bs_ref, wua_ref, wub_ref, wo_ref, g2_ref, wq_ref,
                         x1_ref, h2_ref, pq_ref, scr_ref):
    tm = x_ref.shape[1]
    os = [_load_token_major(r, scr_ref, DILATIONS[g]) for g, r in enumerate((o0_ref, o1_ref, o2_ref))]
    ls = [_load_token_major(r, scr_ref, DILATIONS[g]) for g, r in enumerate((l0_ref, l1_ref, l2_ref))]
    a_out = _combine_groups(os, ls)

    gw = D_MODEL // B_GROUPS
    tri = lax.broadcasted_iota(jnp.int32, (CHUNK, CHUNK), 0) >= lax.broadcasted_iota(jnp.int32, (CHUNK, CHUNK), 1)
    parts = []
    for c in range(tm // CHUNK):
        rows = slice(c * CHUNK, (c + 1) * CHUNK)
        cols = []
        for g in range(B_GROUPS):
            w = jnp.where(tri, ws_ref[g], 0.0).astype(BF16)
            cols.append(jnp.dot(w, vb_ref[0, rows, g * gw:(g + 1) * gw], preferred_element_type=F32))
        mixed = jnp.concatenate(cols, axis=1) + bs_ref[...]
        parts.append(ub_ref[0, rows, :].astype(F32) * mixed)
    b_out = jnp.concatenate(parts, axis=0)

    x1, h2, pq = _merge_tail(x_ref[0], a_out, b_out, ga_ref[0], gb_ref[0], wua_ref, wub_ref, wo_ref, g2_ref, wq_ref)
    x1_ref[0] = x1
    h2_ref[0] = h2
    for h in range(PEER_HEADS):
        pq_ref[h, 0] = pq[:, h * 2 * PEER_HALF:(h + 1) * 2 * PEER_HALF].astype(pq_ref.dtype)


def _merge_sample_kernel(x_ref, o0_ref, o1_ref, o2_ref, l0_ref, l1_ref, l2_ref, ub_ref, vb_ref, ga_ref, gb_ref,
                         ws_ref, bs_ref, wua_ref, wub_ref, wo_ref, g2_ref, wq_ref, x1_ref, h2_ref, pq_ref):
    a_out = _combine_groups([o0_ref[...], o1_ref[...], o2_ref[...]], [l0_ref[...], l1_ref[...], l2_ref[...]])
    b_out = ub_ref[...].astype(F32) * (vb_ref[...] * ws_ref[...] + bs_ref[...])
    x1, h2, pq = _merge_tail(x_ref[...], a_out, b_out, ga_ref[...], gb_ref[...], wua_ref, wub_ref, wo_ref, g2_ref, wq_ref)
    x1_ref[...] = x1
    h2_ref[...] = h2
    for h in range(PEER_HEADS):
        pq_ref[h] = pq[:, h * 2 * PEER_HALF:(h + 1) * 2 * PEER_HALF].astype(pq_ref.dtype)


def _merge_prompt(x, o, lse, ub, vb, ga, gb, ws, bs, wua, wub, wo, g2, wq, tm):
    b, s, _ = x.shape
    row = lambda w: pl.BlockSpec((1, tm, w), lambda bi, ti: (bi, ti, 0))
    dilated = [pl.BlockSpec((1, d, tm // d, GROUP_W), lambda bi, ti: (bi, 0, ti, 0)) for d in DILATIONS]
    weights = (ws, bs, wua, wub, wo, g2, wq)
    return pl.pallas_call(
        _merge_prompt_kernel,
        out_shape=(jax.ShapeDtypeStruct((b, s, D_MODEL), F32), jax.ShapeDtypeStruct((b, s, D_MODEL), BF16),
                   jax.ShapeDtypeStruct((PEER_HEADS, b, s, 2 * PEER_HALF), BF16)),
        grid=(b, s // tm),
        in_specs=[row(D_MODEL)] + dilated + dilated + [row(D_MODEL)] * 4
                 + [pl.BlockSpec(w.shape, functools.partial(lambda *_, nd: (0,) * nd, nd=w.ndim),
                                 pipeline_mode=pl.Buffered(1)) for w in weights],
        out_specs=(row(D_MODEL), row(D_MODEL),
                   pl.BlockSpec((PEER_HEADS, 1, tm, 2 * PEER_HALF), lambda bi, ti: (0, bi, ti, 0))),
        scratch_shapes=[pltpu.VMEM((SLABS, tm, LANES), F32)],
        compiler_params=_cparams(("parallel", "parallel"), V7X_VMEM_LIMIT),
        name="merge",
    )(x, *o, *lse, ub, vb, ga, gb, *weights)


def _merge_sample(x2d, o, lse, ub, vb, ga, gb, ws0, bs0, wua, wub, wo, g2, wq):
    t = x2d.shape[0]
    row = lambda w: pl.BlockSpec((t, w), lambda i: (0, 0))
    weights = (ws0, bs0, wua, wub, wo, g2, wq)
    pq_shape = (PEER_HEADS, t, 2 * PEER_HALF)
    return pl.pallas_call(
        _merge_sample_kernel,
        out_shape=(jax.ShapeDtypeStruct((t, D_MODEL), F32), jax.ShapeDtypeStruct((t, D_MODEL), BF16),
                   jax.ShapeDtypeStruct(pq_shape, BF16)),
        grid=(1,),
        in_specs=[row(D_MODEL)] + [row(GROUP_W)] * 6 + [row(D_MODEL)] * 4 + [_resident(w.shape) for w in weights],
        out_specs=(row(D_MODEL), row(D_MODEL), _resident(pq_shape)),
        compiler_params=_cparams(("arbitrary",), V7X_VMEM_LIMIT),
        name="merge_sample",
    )(x2d, *o, *lse, ub, vb, ga, gb, *weights)


def _top16_rows(s, idx_col):
    vals, idxs = [], []
    big = jnp.float32(1e9)
    for _ in range(PEER_TOPK):
        m = jnp.max(s, axis=0, keepdims=True)
        sel = jnp.min(jnp.where(s == m, idx_col, big), axis=0, keepdims=True)
        s = jnp.where(idx_col == sel, -jnp.inf, s)
        vals.append(m)
        idxs.append(sel)
    return vals, idxs


def _route_chunk(qh, sk1, sk2):
    key_col = lax.broadcasted_iota(jnp.int32, (N_KEYS, LANES), 0).astype(F32)
    row8 = lax.broadcasted_iota(jnp.int32, (8, LANES), 0).astype(F32)
    row16 = lax.broadcasted_iota(jnp.int32, (PEER_TOPK, LANES), 0).astype(F32)
    s1 = _nt_dot(sk1, qh[:, :PEER_HALF])
    s2 = _nt_dot(sk2, qh[:, PEER_HALF:])
    v1, i1 = _top16_rows(s1, key_col)
    v2, i2 = _top16_rows(s2, key_col)
    v2a = jnp.concatenate(v2, axis=0)
    i2a = jnp.concatenate(i2, axis=0)
    cand = [v1[0] + v2a]
    flat = [row16]
    code = [i1[0] * N_KEYS + i2a]
    for j1 in range(1, 8):
        cand.append(v1[j1] + v2a[:8])
        flat.append(row8 + float(j1 * PEER_TOPK))
        code.append(i1[j1] * N_KEYS + i2a[:8])
    cand.append(jnp.concatenate(v1[8:], axis=0) + v2[0])
    flat.append((row8 + 8.0) * float(PEER_TOPK))
    code.append(jnp.concatenate(i1[8:], axis=0) * N_KEYS + i2[0])
    cand = jnp.concatenate(cand, axis=0)
    flat = jnp.concatenate(flat, axis=0)
    code = jnp.concatenate(code, axis=0)
    top_s, top_e = [], []
    for _ in range(PEER_TOPK):
        m = jnp.max(cand, axis=0, keepdims=True)
        sel = jnp.min(jnp.where(cand == m, flat, 1e9), axis=0, keepdims=True)
        hit = flat == sel
        top_e.append(jnp.max(jnp.where(hit, code, -1.0), axis=0, keepdims=True))
        cand = jnp.where(hit, -jnp.inf, cand)
        top_s.append(m)
    ts = jnp.concatenate(top_s, axis=0)
    te = jnp.concatenate(top_e, axis=0)
    e = jnp.exp(ts - ts[0:1])
    gate = e / jnp.sum(e, axis=0, keepdims=True)
    a = jnp.floor(te * (1.0 / N_KEYS))
    return a, te - a * N_KEYS, gate


def _route_kernel(pq_ref, sk1_ref, sk2_ref, i1_ref, i2_ref, gate_ref):
    for c in range(pq_ref.shape[0] // LANES):
        rows = slice(c * LANES, (c + 1) * LANES)
        a, b, gate = _route_chunk(pq_ref[rows, :], sk1_ref[...], sk2_ref[...])
        i1_ref[:, rows] = a
        i2_ref[:, rows] = b
        gate_ref[:, rows] = gate


def _route(pq3, sk1_bf, sk2_bf, tm, n_tokens):
    out = pl.BlockSpec((PEER_TOPK, tm), lambda i, h: (h, i))
    shp = jax.ShapeDtypeStruct((PEER_HEADS * PEER_TOPK, n_tokens), F32)
    return pl.pallas_call(
        _route_kernel,
        out_shape=(shp, shp, shp),
        grid=(n_tokens // tm, PEER_HEADS),
        in_specs=[pl.BlockSpec((None, tm, 2 * PEER_HALF), lambda i, h: (h, i, 0)),
                  _resident(sk1_bf.shape), _resident(sk2_bf.shape)],
        out_specs=(out, out, out),
        compiler_params=_cparams(("parallel", "parallel")),
        name="peer_route",
    )(pq3, sk1_bf, sk2_bf)


def _peer_kernel(h2_ref, x1_ref, i10_ref, i20_ref, gt0_ref, pqn_ref, sk1_ref, sk2_ref, ed_ref, eu_ref,
                 x2_ref, pk_ref, acc_ref, rows_ref, picks_ref, *, phase_units, units_per_step):
    tm = h2_ref.shape[0]
    i = pl.program_id(0)
    j = pl.program_id(1)
    half = N_KEYS // 2
    n_chunks = tm // LANES

    def route_unit(u):
        c = u // PEER_HEADS
        h = u % PEER_HEADS
        qh = pqn_ref[h, pl.ds(pl.multiple_of(c * LANES, LANES), LANES), :]
        picks = _route_chunk(qh, sk1_ref[...], sk2_ref[...])
        r0 = pl.multiple_of(h * PEER_TOPK, PEER_TOPK)
        for q in range(3):
            picks_ref[q, c, pl.ds(r0, PEER_TOPK), :] = picks[q]

    @pl.when(j == 0)
    def _start_tile():
        @pl.when(i == 0)
        def _():
            for q, src in enumerate((i10_ref, i20_ref, gt0_ref)):
                for c in range(n_chunks):
                    rows_ref[q, c * LANES:(c + 1) * LANES, :] = src[:, c * LANES:(c + 1) * LANES].T

        @pl.when(i > 0)
        def _():
            for q in range(3):
                for c in range(n_chunks):
                    rows_ref[q, c * LANES:(c + 1) * LANES, :] = picks_ref[q, c].T

        acc_ref[...] = jnp.zeros_like(acc_ref)
        sub = lax.broadcasted_iota(jnp.int32, (N_KEYS, N_KEYS), 0)
        b_of_row = sub.astype(F32)
        a_of_row = jnp.where(sub < half, 2 * sub, 2 * (sub - half) + 1).astype(F32)
        tok = tm // phase_units

        def body(u, carry):
            route_unit(u)
            for t in range(tok):
                tt = u * tok + t
                r1 = rows_ref[0, pl.ds(tt, 1), :]
                r2 = rows_ref[1, pl.ds(tt, 1), :]
                rg = rows_ref[2, pl.ds(tt, 1), :]
                oat = jnp.where(a_of_row == r1, 1.0, 0.0).astype(BF16)
                gbt = jnp.where(b_of_row == r2, rg, 0.0).astype(BF16)
                g = _nt_dot(oat, gbt)
                packed = pltpu.pack_elementwise([g[:half], g[half:]], packed_dtype=BF16)
                pk_ref[pl.ds(pl.multiple_of(tt * G_PITCH, 8), half), :] = packed
            return carry

        lax.fori_loop(0, phase_units, body, 0)

    for k in range(units_per_step):
        route_unit(phase_units + j * units_per_step + k)
    gs = []
    for r in range(PAIRS_PER_STEP):
        w = pk_ref[pl.ds(j * PAIRS_PER_STEP + r, tm, stride=G_PITCH), :]
        gs.append(pltpu.unpack_elementwise(w, index=0, packed_dtype=BF16, unpacked_dtype=F32))
        gs.append(pltpu.unpack_elementwise(w, index=1, packed_dtype=BF16, unpacked_dtype=F32))
    gates = jnp.concatenate(gs, axis=1)
    s = _nt_dot(h2_ref[...], ed_ref[...])
    act = (jax.nn.gelu(s) * gates).astype(BF16)
    acc_ref[...] += jnp.dot(act, eu_ref[...], preferred_element_type=F32)

    @pl.when(j == pl.num_programs(1) - 1)
    def _finish():
        x2_ref[...] = x1_ref[...] + acc_ref[...]


def _peer(h2, x1, picks0, pq3, sk1, sk2, ed, eu, tm):
    t = h2.shape[0]
    nt = t // tm
    nb = PAIRS_PER_STEP * 2 * N_KEYS
    steps = ed.shape[0] // nb
    n_units = (tm // LANES) * PEER_HEADS
    units_per_step = n_units // (2 * steps)
    phase_units = n_units - steps * units_per_step
    row = lambda w: pl.BlockSpec((tm, w), lambda i, j: (i, 0))
    table = pl.BlockSpec((nb, D_MODEL), lambda i, j: (j, 0))
    return pl.pallas_call(
        functools.partial(_peer_kernel, phase_units=phase_units, units_per_step=units_per_step),
        out_shape=jax.ShapeDtypeStruct((t, D_MODEL), F32),
        grid=(nt, steps),
        in_specs=[row(D_MODEL), row(D_MODEL)] + [_resident(p.shape) for p in picks0]
                 + [pl.BlockSpec((PEER_HEADS, tm, 2 * PEER_HALF), lambda i, j: (0, jnp.minimum(i + 1, nt - 1), 0)),
                    _resident(sk1.shape), _resident(sk2.shape), table, table],
        out_specs=row(D_MODEL),
        scratch_shapes=[pltpu.VMEM((tm * G_PITCH, N_KEYS), jnp.uint32),
                        pltpu.VMEM((tm, D_MODEL), F32),
                        pltpu.VMEM((3, tm, N_KEYS), F32),
                        pltpu.VMEM((3, tm // LANES, PEER_HEADS * PEER_TOPK, LANES), F32)],
        compiler_params=_cparams(("arbitrary", "arbitrary"), V7X_VMEM_LIMIT),
        name="peer",
    )(h2, x1, *picks0, pq3, sk1, sk2, ed, eu)


def _ple_kernel(x2_ref, p_ref, g3_ref, wpg_ref, wpp_ref, y_ref):
    x2 = x2_ref[...]
    h3 = _rms(x2, g3_ref[...]).astype(BF16)
    gate = jax.nn.sigmoid(jnp.dot(h3, wpg_ref[...], preferred_element_type=F32))
    y_ref[...] = x2 + gate * jnp.dot(p_ref[...].astype(BF16), wpp_ref[...], preferred_element_type=F32)


def _ple(x2, p2d, g3, wpg, wpp, tm):
    t = x2.shape[0]
    row = lambda w: pl.BlockSpec((tm, w), lambda i: (i, 0))
    return pl.pallas_call(
        _ple_kernel,
        out_shape=jax.ShapeDtypeStruct((t, D_MODEL), F32),
        grid=(t // tm,),
        in_specs=[row(D_MODEL), row(PLE_DIM), _resident(g3.shape), _resident(wpg.shape), _resident(wpp.shape)],
        out_specs=row(D_MODEL),
        compiler_params=_cparams(("parallel",)),
        name="ple",
    )(x2, p2d, g3, wpg, wpp)


def _t5_bucket(dist):
    max_exact = N_BUCKETS // 2
    df = jnp.maximum(dist, max_exact).astype(F32)
    large = max_exact + (jnp.log(df / max_exact) / math.log(MAX_DISTANCE / max_exact)
                         * (N_BUCKETS - max_exact)).astype(jnp.int32)
    return jnp.where(dist < max_exact, dist, jnp.minimum(large, N_BUCKETS - 1))


def _bias_tables(rel_bias, g):
    dil = DILATIONS[g]
    hi = lax.Precision.HIGHEST
    bucket = _t5_bucket(jnp.arange(BAND + 1) * dil)
    per_dist = rel_bias[:, g * HEADS:(g + 1) * HEADS][bucket].astype(F32)
    dsub = np.arange(BAND)[:, None] + BAND - np.arange(2 * BAND)[None, :]
    valid = (dsub >= 0) & (dsub <= WINDOWS[g] // dil)
    spread = (jnp.asarray(np.clip(dsub, 0, BAND).reshape(1, -1)) == jnp.arange(BAND + 1)[:, None]).astype(F32)
    band = jnp.dot(per_dist.T, spread, precision=hi).reshape(HEADS, BAND, 2 * BAND)
    add = jnp.where(jnp.asarray(valid)[None], band, NEG_INF)
    addc = per_dist[::-1][:BAND].T
    addn = per_dist[0][:, None]
    return add, addc, addn


def kernel(x_prompt, x_sample, cache_kv_w128, cache_kv_w512, cache_kv_w2048, p_prompt, p_sample, rel_bias, norm1,
           w_in, q_norm, k_norm, b_v_norm, w_spatial, b_spatial, w_up_a, w_up_b, w_out, norm2, w_query,
           sub_keys_1, sub_keys_2, expert_down, expert_up, norm3, w_ple_gate, w_ple_proj):
    bsz, seq, _ = x_prompt.shape
    db, ds, _ = x_sample.shape
    depth = norm1.shape[0]
    assert depth == 1 and ds == 1 and seq % (BAND * DILATIONS[-1]) == 0 and seq >= WINDOWS[-1]
    caches = (cache_kv_w128, cache_kv_w512, cache_kv_w2048)
    for g in range(N_GROUPS):
        assert caches[g].shape[2] == WINDOWS[g]
    t_p = bsz * seq
    t_s = 128
    li = 0

    g1 = norm1[li][None, :]
    w_in_bf = w_in[li].astype(BF16)
    qg = jnp.tile(q_norm[li], (1, HEADS))
    kg = jnp.tile(k_norm[li], (1, HEADS))
    bvg = b_v_norm[li][None, :]
    blk = np.arange(V7X_MXU_DIM) // HEAD_DIM
    hsum = jnp.asarray(blk[:, None] == blk[None, :], dtype=BF16)
    wua, wub, wo = w_up_a[li].astype(BF16), w_up_b[li].astype(BF16), w_out[li].astype(BF16)
    g2, g3 = norm2[li][None, :], norm3[li][None, :]
    wq = w_query[li].astype(BF16)
    sk1, sk2 = sub_keys_1[li].astype(BF16), sub_keys_2[li].astype(BF16)
    ed = expert_down[li].astype(BF16)
    eu = expert_up[li].astype(BF16)
    wpg, wpp = w_ple_gate[li].astype(BF16), w_ple_proj[li].astype(BF16)
    bs_full = jnp.repeat(b_spatial[li].T, D_MODEL // B_GROUPS, axis=1)
    tables = [_bias_tables(rel_bias, g) for g in range(N_GROUPS)]

    res = _in_projection_prompt(x_prompt, g1, w_in_bf, qg, kg, bvg, hsum, TM_INPROJ)
    qs, ks, vs, tails = res[0:3], res[3:6], res[6:9], res[9:12]
    ub, vb, ga, gb = res[12:16]
    outs, lses, kv_prompt = [], [], []
    for g in range(N_GROUPS):
        o, lse = _prompt_attention(qs[g], ks[g], vs[g], tables[g][0], g)
        outs.append(o)
        lses.append(lse)
        kv_prompt.append(tails[g].reshape(1, bsz, tails[g].shape[1], 2, HEADS, HEAD_DIM))
    x1, h2, pq = _merge_prompt(x_prompt, outs, lses, ub, vb, ga, gb, w_spatial[li], bs_full, wua, wub, wo, g2, wq,
                               TM_INPROJ)
    x1, h2 = x1.reshape(t_p, D_MODEL), h2.reshape(t_p, D_MODEL)
    pq = pq.reshape(PEER_HEADS, t_p, 2 * PEER_HALF)
    picks0 = _route(pq, sk1, sk2, TM_PROJ, TM_PEER)
    x2 = _peer(h2, x1, picks0, pq, sk1, sk2, ed, eu, TM_PEER)
    y_prompt = _ple(x2, p_prompt[li].reshape(t_p, PLE_DIM), g3, wpg, wpp, TM_PEER).reshape(bsz, seq, D_MODEL)

    xs = jnp.pad(x_sample.reshape(db, D_MODEL), ((0, t_s - db), (0, 0)))
    q, k, v, ub, vb, ga, gb = _in_projection_sample(xs, g1, w_in_bf, qg, kg, bvg, hsum)
    nh = N_GROUPS * HEADS
    k3, v3 = k[:db].reshape(db, nh, HEAD_DIM), v[:db].reshape(db, nh, HEAD_DIM)
    caches_t = [jnp.transpose(caches[g], (0, 1, 3, 4, 5, 2))[li] for g in range(N_GROUPS)]
    adds = []
    for g in range(N_GROUPS):
        dil = DILATIONS[g]
        full = jnp.full((HEADS, BAND, dil), NEG_INF, F32).at[:, :, 0].set(tables[g][1])
        adds.append(full.reshape(HEADS, BAND * dil))
    addn = jnp.stack([tables[g][2] for g in range(N_GROUPS)])
    o3, lse3 = _sample_attention(q[:db, None, :].astype(F32), k[:db, None, :], v[:db, None, :], caches_t, adds, addn)
    pad = lambda a: jnp.pad(a, ((0, t_s - db), (0, 0)))
    o2, lse2 = o3.reshape(db, QKV_W), lse3.reshape(db, QKV_W)
    outs = [pad(o2[:, g * GROUP_W:(g + 1) * GROUP_W]) for g in range(N_GROUPS)]
    lses = [pad(lse2[:, g * GROUP_W:(g + 1) * GROUP_W]) for g in range(N_GROUPS)]
    kv_sample = [jnp.stack([k3[:, None, g * HEADS:(g + 1) * HEADS], v3[:, None, g * HEADS:(g + 1) * HEADS]],
                           axis=2)[None] for g in range(N_GROUPS)]
    ws0 = jnp.repeat(w_spatial[li][:, 0, 0], D_MODEL // B_GROUPS)[None, :]
    bs0 = bs_full[0:1, :]
    x1, h2, pq = _merge_sample(xs, outs, lses, ub, vb, ga, gb, ws0, bs0, wua, wub, wo, g2, wq)
    picks0 = _route(pq, sk1, sk2, t_s, t_s)
    ps = jnp.pad(p_sample[li].reshape(db, PLE_DIM), ((0, t_s - db), (0, 0)))
    x2 = _peer(h2, x1, picks0, pq, sk1, sk2, ed, eu, t_s)
    y_sample = _ple(x2, ps, g3, wpg, wpp, t_s)[:db].reshape(db, ds, D_MODEL)
    v_sample_chunk = vb[:db].reshape(1, db, ds, D_MODEL)

    return (y_prompt, y_sample, kv_prompt[0], kv_prompt[1], kv_prompt[2],
            kv_sample[0], kv_sample[1], kv_sample[2], v_sample_chunk)
```

```python
import functools
import math

import numpy as np
import jax
import jax.numpy as jnp
from jax import lax
from jax.experimental import pallas as pl
from jax.experimental.pallas import tpu as pltpu

F32 = jnp.float32
BF16 = jnp.bfloat16

D_MODEL = 1024
N_GROUPS = 3
HEADS = 8
HEAD_DIM = 64
GROUP_W = HEADS * HEAD_DIM
QKV_W = N_GROUPS * GROUP_W
WINDOWS = (128, 512, 2048)
DILATIONS = (1, 4, 16)
BAND = 128
N_BUCKETS = 32
MAX_DISTANCE = 2048
CHUNK = 128
B_GROUPS = 4
N_KEYS = 128
PEER_HEADS = 8
PEER_TOPK = 16
PEER_HALF = 128
PLE_DIM = 256
EPS = 1e-6
NEG_INF = -1e30
IN_OFFS = (0, QKV_W, 2 * QKV_W, 3 * QKV_W, 3 * QKV_W + D_MODEL, 3 * QKV_W + 2 * D_MODEL,
           3 * QKV_W + 3 * D_MODEL, 3 * QKV_W + 4 * D_MODEL)

LANES = 128
SLABS = GROUP_W // LANES
V7X_VMEM_LIMIT = 56 * 1024 * 1024
V7X_MXU_DIM = 256
G_PITCH = 72
TM_PROJ = 256
TM_INPROJ = 512
ATTN_Q_ROWS = (512, 256, 128)
TM_PEER = 512
PAIRS_PER_STEP = 4

def _cparams(sem, vmem=None):
    return pltpu.CompilerParams(dimension_semantics=sem, vmem_limit_bytes=vmem)


def _resident(shape):
    nd = len(shape)
    return pl.BlockSpec(shape, lambda *_: (0,) * nd)


def _nt_dot(a, b):
    return lax.dot_general(a, b, (((1,), (1,)), ((), ())), preferred_element_type=F32)


def _rms(x, gain):
    return x * lax.rsqrt(jnp.mean(x * x, axis=-1, keepdims=True) + EPS) * gain


def _head_norm(y, hsum_ref, gain_row):
    y2 = (y * y).astype(BF16)
    w = hsum_ref.shape[0]
    ss = jnp.concatenate([jnp.dot(y2[:, c:c + w], hsum_ref[...], preferred_element_type=F32)
                          for c in range(0, GROUP_W, w)], axis=1)
    return y * lax.rsqrt(ss * (1.0 / HEAD_DIM) + EPS) * gain_row


def _store_residue_major(out_ref, y, scr_ref, dil):
    if dil == 1:
        out_ref[0, 0] = y.astype(out_ref.dtype)
        return
    n = y.shape[0] // dil
    for s in range(SLABS):
        scr_ref[s] = y[:, s * LANES:(s + 1) * LANES]
    for r in range(dil):
        for s in range(SLABS):
            out_ref[0, r, :, s * LANES:(s + 1) * LANES] = scr_ref[s, pl.ds(r, n, stride=dil), :].astype(out_ref.dtype)


def _gmlp_and_gates(proj, bvg_ref, ub_ref, vb_ref, ga_ref, gb_ref, st):
    half = D_MODEL // 2
    for c in range(2):
        st(ub_ref, c * half, jax.nn.gelu(proj(IN_OFFS[3] + c * half, half)))
    vb0 = jax.nn.gelu(proj(IN_OFFS[4], half))
    vb1 = jax.nn.gelu(proj(IN_OFFS[4] + half, half))
    ms = (jnp.sum(vb0 * vb0, axis=-1, keepdims=True) + jnp.sum(vb1 * vb1, axis=-1, keepdims=True)) * (1.0 / D_MODEL)
    r = lax.rsqrt(ms + EPS)
    st(vb_ref, 0, vb0 * r * bvg_ref[:, :half])
    st(vb_ref, half, vb1 * r * bvg_ref[:, half:])
    for c in range(2):
        st(ga_ref, c * half, jax.nn.sigmoid(proj(IN_OFFS[5] + c * half, half)))
        st(gb_ref, c * half, jax.nn.sigmoid(proj(IN_OFFS[6] + c * half, half)))


def _inproj_prompt_kernel(x_ref, g1_ref, w_ref, qg_ref, kg_ref, bvg_ref, hsum_ref,
                          q0_ref, q1_ref, q2_ref, k0_ref, k1_ref, k2_ref, v0_ref, v1_ref, v2_ref,
                          t0_ref, t1_ref, t2_ref, ub_ref, vb_ref, ga_ref, gb_ref, scr_ref, *, tail_first):
    ti = pl.program_id(1)
    tm = x_ref.shape[1]
    hb = _rms(x_ref[0], g1_ref[...]).astype(BF16)

    def proj(c0, width):
        return jnp.dot(hb, w_ref[:, c0:c0 + width], preferred_element_type=F32)

    q_refs, k_refs, v_refs, t_refs = (q0_ref, q1_ref, q2_ref), (k0_ref, k1_ref, k2_ref), (v0_ref, v1_ref, v2_ref), \
        (t0_ref, t1_ref, t2_ref)
    for g in range(N_GROUPS):
        lo = g * GROUP_W
        dil = DILATIONS[g]
        qn = _head_norm(proj(IN_OFFS[0] + lo, GROUP_W), hsum_ref, qg_ref[g:g + 1, :]) * (HEAD_DIM ** -0.5)
        _store_residue_major(q_refs[g], qn, scr_ref, dil)
        kn = _head_norm(proj(IN_OFFS[1] + lo, GROUP_W), hsum_ref, kg_ref[g:g + 1, :])
        _store_residue_major(k_refs[g], kn, scr_ref, dil)
        vv = proj(IN_OFFS[2] + lo, GROUP_W)
        _store_residue_major(v_refs[g], vv, scr_ref, dil)
        rows = t_refs[g].shape[1]

        @pl.when(ti >= tail_first[g])
        def _():
            t_refs[g][0, :, :GROUP_W] = kn[tm - rows:, :]
            t_refs[g][0, :, GROUP_W:] = vv[tm - rows:, :]

    def st(ref, c0, val):
        ref[0, :, c0:c0 + val.shape[1]] = val.astype(ref.dtype)

    _gmlp_and_gates(proj, bvg_ref, ub_ref, vb_ref, ga_ref, gb_ref, st)


def _inproj_sample_kernel(x_ref, g1_ref, w_ref, qg_ref, kg_ref, bvg_ref, hsum_ref,
                          q_ref, k_ref, v_ref, ub_ref, vb_ref, ga_ref, gb_ref):
    hb = _rms(x_ref[...], g1_ref[...]).astype(BF16)

    def proj(c0, width):
        return jnp.dot(hb, w_ref[:, c0:c0 + width], preferred_element_type=F32)

    for g in range(N_GROUPS):
        lo = g * GROUP_W
        qn = _head_norm(proj(IN_OFFS[0] + lo, GROUP_W), hsum_ref, qg_ref[g:g + 1, :]) * (HEAD_DIM ** -0.5)
        q_ref[:, lo:lo + GROUP_W] = qn.astype(q_ref.dtype)
        k_ref[:, lo:lo + GROUP_W] = _head_norm(proj(IN_OFFS[1] + lo, GROUP_W), hsum_ref, kg_ref[g:g + 1, :])
        v_ref[:, lo:lo + GROUP_W] = proj(IN_OFFS[2] + lo, GROUP_W)

    def st(ref, c0, val):
        ref[:, c0:c0 + val.shape[1]] = val.astype(ref.dtype)

    _gmlp_and_gates(proj, bvg_ref, ub_ref, vb_ref, ga_ref, gb_ref, st)


def _proj_weight_specs(w_in_bf):
    return [_resident((1, D_MODEL)),
            pl.BlockSpec(w_in_bf.shape, lambda *_: (0, 0), pipeline_mode=pl.Buffered(1)),
            _resident((N_GROUPS, GROUP_W)), _resident((N_GROUPS, GROUP_W)), _resident((1, D_MODEL)),
            _resident((V7X_MXU_DIM, V7X_MXU_DIM))]


def _in_projection_prompt(x, g1, w_in_bf, qg, kg, bvg, hsum, tm):
    b, s, _ = x.shape
    nt = s // tm
    row = pl.BlockSpec((1, tm, D_MODEL), lambda bi, ti: (bi, ti, 0))
    qkv_shapes, qkv_specs, tail_shapes, tail_specs, tail_first = [], [], [], [], []
    for g in range(N_GROUPS):
        dil = DILATIONS[g]
        qkv_shapes.append(jax.ShapeDtypeStruct((b, dil, s // dil, GROUP_W), BF16))
        qkv_specs.append(pl.BlockSpec((1, dil, tm // dil, GROUP_W), lambda bi, ti: (bi, 0, ti, 0)))
        ln = min(WINDOWS[g], s)
        rows = min(ln, tm)
        first = (s - ln) // tm if ln >= tm else nt - 1
        tail_first.append(first)
        tail_shapes.append(jax.ShapeDtypeStruct((b, ln, 2 * GROUP_W), F32))
        tail_specs.append(pl.BlockSpec((1, rows, 2 * GROUP_W),
                                       functools.partial(lambda bi, ti, f: (bi, jnp.maximum(ti - f, 0), 0), f=first)))
    act = jax.ShapeDtypeStruct((b, s, D_MODEL), BF16)
    gate = jax.ShapeDtypeStruct((b, s, D_MODEL), BF16)
    return pl.pallas_call(
        functools.partial(_inproj_prompt_kernel, tail_first=tuple(tail_first)),
        out_shape=tuple(qkv_shapes * 3 + tail_shapes + [act, act, gate, gate]),
        grid=(b, nt),
        in_specs=[row] + _proj_weight_specs(w_in_bf),
        out_specs=tuple(qkv_specs * 3 + tail_specs + [row] * 4),
        scratch_shapes=[pltpu.VMEM((SLABS, tm, LANES), F32)],
        compiler_params=_cparams(("arbitrary", "arbitrary"), V7X_VMEM_LIMIT),
        name="in_projection",
    )(x, g1, w_in_bf, qg, kg, bvg, hsum)


def _in_projection_sample(x2d, g1, w_in_bf, qg, kg, bvg, hsum):
    t = x2d.shape[0]
    row = lambda w: pl.BlockSpec((t, w), lambda i: (0, 0))
    outs = (jax.ShapeDtypeStruct((t, QKV_W), BF16), jax.ShapeDtypeStruct((t, QKV_W), F32),
            jax.ShapeDtypeStruct((t, QKV_W), F32), jax.ShapeDtypeStruct((t, D_MODEL), BF16),
            jax.ShapeDtypeStruct((t, D_MODEL), F32), jax.ShapeDtypeStruct((t, D_MODEL), BF16),
            jax.ShapeDtypeStruct((t, D_MODEL), BF16))
    return pl.pallas_call(
        _inproj_sample_kernel,
        out_shape=outs,
        grid=(1,),
        in_specs=[row(D_MODEL)] + _proj_weight_specs(w_in_bf),
        out_specs=(row(QKV_W), row(QKV_W), row(QKV_W), row(D_MODEL), row(D_MODEL), row(D_MODEL), row(D_MODEL)),
        compiler_params=_cparams(("arbitrary",), V7X_VMEM_LIMIT),
        name="in_projection_sample",
    )(x2d, g1, w_in_bf, qg, kg, bvg, hsum)


def _attn_kernel(q_ref, kp_ref, kc_ref, vp_ref, vc_ref, add_ref, o_ref, lse_ref):
    i = pl.program_id(2)
    lane = lax.broadcasted_iota(jnp.int32, (BAND, LANES), 1)
    kcol = lax.broadcasted_iota(jnp.int32, (BAND, 2 * BAND), 1)
    no_prev = kcol < jnp.where(i == 0, BAND, 0)
    bands = [(rr, sb) for rr in range(q_ref.shape[1]) for sb in range(q_ref.shape[2] // BAND)]
    for rr, sb in bands:
        rows = slice(sb * BAND, (sb + 1) * BAND)
        q = q_ref[0, rr, rows, :]
        kall = jnp.concatenate([kp_ref[0, rr], kc_ref[0, rr]], axis=0)
        vall = jnp.concatenate([vp_ref[0, rr], vc_ref[0, rr]], axis=0)
        kk, vv = kall[sb * BAND:(sb + 2) * BAND], vall[sb * BAND:(sb + 2) * BAND]
        for pair in range(HEADS // 2):
            sl = slice(pair * LANES, (pair + 1) * LANES)
            qp, kp, vp = q[:, sl].astype(F32), kk[:, sl], vv[:, sl]
            o_pair = jnp.zeros((BAND, LANES), F32)
            l_pair = jnp.zeros((BAND, LANES), F32)
            for hh in range(2):
                mine = (lane >= HEAD_DIM) if hh else (lane < HEAD_DIM)
                s = _nt_dot(jnp.where(mine, qp, 0.0).astype(BF16), kp) + add_ref[pair * 2 + hh]
                if sb == 0:
                    s = jnp.where(no_prev, NEG_INF, s)
                m = jnp.max(s, axis=-1, keepdims=True)
                p = jnp.exp(s - m)
                den = jnp.sum(p, axis=-1, keepdims=True)
                o = jnp.dot(p.astype(BF16), vp, preferred_element_type=F32) / den
                o_pair = jnp.where(mine, o, o_pair)
                l_pair = jnp.where(mine, m + jnp.log(den), l_pair)
            o_ref[0, rr, rows, sl] = o_pair.astype(o_ref.dtype)
            lse_ref[0, rr, rows, sl] = l_pair


def _prompt_attention(q4, k4, v4, add, g):
    b, dil, n, _ = q4.shape
    qb = next(c for c in ATTN_Q_ROWS if n % c == 0)
    per = qb // BAND
    res = min(dil, max(1, ATTN_Q_ROWS[0] // qb))
    cur = pl.BlockSpec((1, res, qb, GROUP_W), lambda bi, r, i: (bi, r, i, 0))
    prev = pl.BlockSpec((1, res, BAND, GROUP_W), lambda bi, r, i: (bi, r, jnp.maximum(i * per - 1, 0), 0))
    return pl.pallas_call(
        _attn_kernel,
        out_shape=(jax.ShapeDtypeStruct(q4.shape, BF16), jax.ShapeDtypeStruct(q4.shape, F32)),
        grid=(b, dil // res, n // qb),
        in_specs=[cur, prev, cur, prev, cur, _resident(add.shape)],
        out_specs=(cur, cur),
        compiler_params=_cparams(("parallel", "parallel", "arbitrary")),
        name=f"attn_g{g}",
    )(q4, k4, k4, v4, v4, add)


def _sample_attn_kernel(q_ref, kn_ref, vn_ref, c0_ref, c1_ref, c2_ref, a0_ref, a1_ref, a2_ref, addn_ref,
                        o_ref, lse_ref):
    rounded = lambda a: a.astype(BF16).astype(F32)
    head_of_lane = lax.broadcasted_iota(jnp.int32, (HEADS, GROUP_W), 1) // HEAD_DIM
    mine = head_of_lane == lax.broadcasted_iota(jnp.int32, (HEADS, GROUP_W), 0)
    for g, (c_ref, a_ref) in enumerate(((c0_ref, a0_ref), (c1_ref, a1_ref), (c2_ref, a2_ref))):
        cols = slice(g * GROUP_W, (g + 1) * GROUP_W)
        ln = c_ref.shape[-1]
        qm = jnp.where(mine, jnp.broadcast_to(q_ref[0, :, cols], (HEADS, GROUP_W)), 0.0)
        kt = c_ref[0].reshape(GROUP_W, ln).astype(BF16)
        vt = c_ref[1].reshape(GROUP_W, ln).astype(BF16)
        kn, vn = rounded(kn_ref[0, :, cols]), rounded(vn_ref[0, :, cols])
        s_c = jnp.dot(qm.astype(BF16), kt, preferred_element_type=F32) + a_ref[...]
        s_n = jnp.sum(qm * kn, axis=-1, keepdims=True) + addn_ref[g]
        m = jnp.maximum(jnp.max(s_c, axis=-1, keepdims=True), s_n)
        p_c = jnp.exp(s_c - m)
        p_n = jnp.exp(s_n - m)
        den = jnp.sum(p_c, axis=-1, keepdims=True) + p_n
        o_all = (_nt_dot(p_c.astype(BF16), vt) + rounded(p_n) * vn) / den
        lse = jnp.broadcast_to(m + jnp.log(den), (HEADS, GROUP_W))
        o_ref[0, :, cols] = jnp.sum(jnp.where(mine, o_all, 0.0), axis=0, keepdims=True)
        lse_ref[0, :, cols] = jnp.sum(jnp.where(mine, lse, 0.0), axis=0, keepdims=True)


def _sample_attention(q_s, k_s, v_s, caches_t, adds, addn):
    db = q_s.shape[0]
    tok = pl.BlockSpec((1, 1, QKV_W), lambda bi: (bi, 0, 0))
    cache_specs = [pl.BlockSpec((None,) + c.shape[1:], lambda bi: (bi, 0, 0, 0, 0)) for c in caches_t]
    shp = jax.ShapeDtypeStruct(q_s.shape, F32)
    return pl.pallas_call(
        _sample_attn_kernel,
        out_shape=(shp, shp),
        grid=(db,),
        in_specs=[tok, tok, tok] + cache_specs + [_resident(a.shape) for a in adds] + [_resident(addn.shape)],
        out_specs=(tok, tok),
        compiler_params=_cparams(("parallel",), V7X_VMEM_LIMIT),
        name="sample_attn",
    )(q_s, k_s, v_s, *caches_t, *adds, addn)


def _load_token_major(ref, scr_ref, dil):
    if dil == 1:
        return ref[0, 0].astype(F32)
    n = ref.shape[2]
    for r in range(dil):
        blk = ref[0, r].astype(F32)
        for s in range(SLABS):
            scr_ref[s, pl.ds(r, n, stride=dil), :] = blk[:, s * LANES:(s + 1) * LANES]
    return jnp.concatenate([scr_ref[s] for s in range(SLABS)], axis=1)


def _merge_tail(x, a_out, b_out, ga, gb, wua_ref, wub_ref, wo_ref, g2_ref, wq_ref):
    up_a = jnp.dot(a_out.astype(BF16), wua_ref[...], preferred_element_type=F32)
    up_b = jnp.dot(b_out.astype(BF16), wub_ref[...], preferred_element_type=F32)
    mrg = ga.astype(F32) * up_a + gb.astype(F32) * up_b
    x1 = x + jnp.dot(mrg.astype(BF16), wo_ref[...], preferred_element_type=F32)
    h2 = _rms(x1, g2_ref[...]).astype(BF16)
    return x1, h2, jnp.dot(h2, wq_ref[...], preferred_element_type=F32)


def _combine_groups(os, ls):
    m = jnp.maximum(jnp.maximum(ls[0], ls[1]), ls[2])
    es = [jnp.exp(l - m) for l in ls]
    return (es[0] * os[0] + es[1] * os[1] + es[2] * os[2]) / (es[0] + es[1] + es[2])


def _merge_prompt_kernel(x_ref, o0_ref, o1_ref, o2_ref, l0_ref, l1_ref, l2_ref, ub_ref, vb_ref, ga_ref, gb_ref,
                         ws_ref, bs_ref, wua_ref, wub_ref, wo_ref, g2_ref, wq_ref,
                         x1_ref, h2_ref, pq_ref, scr_ref):
    tm = x_ref.shape[1]
    os = [_load_token_major(r, scr_ref, DILATIONS[g]) for g, r in enumerate((o0_ref, o1_ref, o2_ref))]
    ls = [_load_token_major(r, scr_ref, DILATIONS[g]) for g, r in enumerate((l0_ref, l1_ref, l2_ref))]
    a_out = _combine_groups(os, ls)

    gw = D_MODEL // B_GROUPS
    tri = lax.broadcasted_iota(jnp.int32, (CHUNK, CHUNK), 0) >= lax.broadcasted_iota(jnp.int32, (CHUNK, CHUNK), 1)
    parts = []
    for c in range(tm // CHUNK):
        rows = slice(c * CHUNK, (c + 1) * CHUNK)
        cols = []
        for g in range(B_GROUPS):
            w = jnp.where(tri, ws_ref[g], 0.0).astype(BF16)
            cols.append(jnp.dot(w, vb_ref[0, rows, g * gw:(g + 1) * gw], preferred_element_type=F32))
        mixed = jnp.concatenate(cols, axis=1) + bs_ref[...]
        parts.append(ub_ref[0, rows, :].astype(F32) * mixed)
    b_out = jnp.concatenate(parts, axis=0)

    x1, h2, pq = _merge_tail(x_ref[0], a_out, b_out, ga_ref[0], gb_ref[0], wua_ref, wub_ref, wo_ref, g2_ref, wq_ref)
    x1_ref[0] = x1
    h2_ref[0] = h2
    for h in range(PEER_HEADS):
        pq_ref[h, 0] = pq[:, h * 2 * PEER_HALF:(h + 1) * 2 * PEER_HALF].astype(pq_ref.dtype)


def _merge_sample_kernel(x_ref, o0_ref, o1_ref, o2_ref, l0_ref, l1_ref, l2_ref, ub_ref, vb_ref, ga_ref, gb_ref,
                         ws_ref, bs_ref, wua_ref, wub_ref, wo_ref, g2_ref, wq_ref, x1_ref, h2_ref, pq_ref):
    a_out = _combine_groups([o0_ref[...], o1_ref[...], o2_ref[...]], [l0_ref[...], l1_ref[...], l2_ref[...]])
    b_out = ub_ref[...].astype(F32) * (vb_ref[...] * ws_ref[...] + bs_ref[...])
    x1, h2, pq = _merge_tail(x_ref[...], a_out, b_out, ga_ref[...], gb_ref[...], wua_ref, wub_ref, wo_ref, g2_ref, wq_ref)
    x1_ref[...] = x1
    h2_ref[...] = h2
    for h in range(PEER_HEADS):
        pq_ref[h] = pq[:, h * 2 * PEER_HALF:(h + 1) * 2 * PEER_HALF].astype(pq_ref.dtype)


def _merge_prompt(x, o, lse, ub, vb, ga, gb, ws, bs, wua, wub, wo, g2, wq, tm):
    b, s, _ = x.shape
    row = lambda w: pl.BlockSpec((1, tm, w), lambda bi, ti: (bi, ti, 0))
    dilated = [pl.BlockSpec((1, d, tm // d, GROUP_W), lambda bi, ti: (bi, 0, ti, 0)) for d in DILATIONS]
    weights = (ws, bs, wua, wub, wo, g2, wq)
    return pl.pallas_call(
        _merge_prompt_kernel,
        out_shape=(jax.ShapeDtypeStruct((b, s, D_MODEL), F32), jax.ShapeDtypeStruct((b, s, D_MODEL), BF16),
                   jax.ShapeDtypeStruct((PEER_HEADS, b, s, 2 * PEER_HALF), BF16)),
        grid=(b, s // tm),
        in_specs=[row(D_MODEL)] + dilated + dilated + [row(D_MODEL)] * 4
                 + [pl.BlockSpec(w.shape, functools.partial(lambda *_, nd: (0,) * nd, nd=w.ndim),
                                 pipeline_mode=pl.Buffered(1)) for w in weights],
        out_specs=(row(D_MODEL), row(D_MODEL),
                   pl.BlockSpec((PEER_HEADS, 1, tm, 2 * PEER_HALF), lambda bi, ti: (0, bi, ti, 0))),
        scratch_shapes=[pltpu.VMEM((SLABS, tm, LANES), F32)],
        compiler_params=_cparams(("parallel", "parallel"), V7X_VMEM_LIMIT),
        name="merge",
    )(x, *o, *lse, ub, vb, ga, gb, *weights)


def _merge_sample(x2d, o, lse, ub, vb, ga, gb, ws0, bs0, wua, wub, wo, g2, wq):
    t = x2d.shape[0]
    row = lambda w: pl.BlockSpec((t, w), lambda i: (0, 0))
    weights = (ws0, bs0, wua, wub, wo, g2, wq)
    pq_shape = (PEER_HEADS, t, 2 * PEER_HALF)
    return pl.pallas_call(
        _merge_sample_kernel,
        out_shape=(jax.ShapeDtypeStruct((t, D_MODEL), F32), jax.ShapeDtypeStruct((t, D_MODEL), BF16),
                   jax.ShapeDtypeStruct(pq_shape, BF16)),
        grid=(1,),
        in_specs=[row(D_MODEL)] + [row(GROUP_W)] * 6 + [row(D_MODEL)] * 4 + [_resident(w.shape) for w in weights],
        out_specs=(row(D_MODEL), row(D_MODEL), _resident(pq_shape)),
        compiler_params=_cparams(("arbitrary",), V7X_VMEM_LIMIT),
        name="merge_sample",
    )(x2d, *o, *lse, ub, vb, ga, gb, *weights)


def _top16_rows(s, idx_col):
    vals, idxs = [], []
    big = jnp.float32(1e9)
    for _ in range(PEER_TOPK):
        m = jnp.max(s, axis=0, keepdims=True)
        sel = jnp.min(jnp.where(s == m, idx_col, big), axis=0, keepdims=True)
        s = jnp.where(idx_col == sel, -jnp.inf, s)
        vals.append(m)
        idxs.append(sel)
    return vals, idxs


def _route_chunk(qh, sk1, sk2):
    key_col = lax.broadcasted_iota(jnp.int32, (N_KEYS, LANES), 0).astype(F32)
    row8 = lax.broadcasted_iota(jnp.int32, (8, LANES), 0).astype(F32)
    row16 = lax.broadcasted_iota(jnp.int32, (PEER_TOPK, LANES), 0).astype(F32)
    s1 = _nt_dot(sk1, qh[:, :PEER_HALF])
    s2 = _nt_dot(sk2, qh[:, PEER_HALF:])
    v1, i1 = _top16_rows(s1, key_col)
    v2, i2 = _top16_rows(s2, key_col)
    v2a = jnp.concatenate(v2, axis=0)
    i2a = jnp.concatenate(i2, axis=0)
    cand = [v1[0] + v2a]
    flat = [row16]
    code = [i1[0] * N_KEYS + i2a]
    for j1 in range(1, 8):
        cand.append(v1[j1] + v2a[:8])
        flat.append(row8 + float(j1 * PEER_TOPK))
        code.append(i1[j1] * N_KEYS + i2a[:8])
    cand.append(jnp.concatenate(v1[8:], axis=0) + v2[0])
    flat.append((row8 + 8.0) * float(PEER_TOPK))
    code.append(jnp.concatenate(i1[8:], axis=0) * N_KEYS + i2[0])
    cand = jnp.concatenate(cand, axis=0)
    flat = jnp.concatenate(flat, axis=0)
    code = jnp.concatenate(code, axis=0)
    top_s, top_e = [], []
    for _ in range(PEER_TOPK):
        m = jnp.max(cand, axis=0, keepdims=True)
        sel = jnp.min(jnp.where(cand == m, flat, 1e9), axis=0, keepdims=True)
        hit = flat == sel
        top_e.append(jnp.max(jnp.where(hit, code, -1.0), axis=0, keepdims=True))
        cand = jnp.where(hit, -jnp.inf, cand)
        top_s.append(m)
    ts = jnp.concatenate(top_s, axis=0)
    te = jnp.concatenate(top_e, axis=0)
    e = jnp.exp(ts - ts[0:1])
    gate = e / jnp.sum(e, axis=0, keepdims=True)
    a = jnp.floor(te * (1.0 / N_KEYS))
    return a, te - a * N_KEYS, gate


def _route_kernel(pq_ref, sk1_ref, sk2_ref, i1_ref, i2_ref, gate_ref):
    for c in range(pq_ref.shape[0] // LANES):
        rows = slice(c * LANES, (c + 1) * LANES)
        a, b, gate = _route_chunk(pq_ref[rows, :], sk1_ref[...], sk2_ref[...])
        i1_ref[:, rows] = a
        i2_ref[:, rows] = b
        gate_ref[:, rows] = gate


def _route(pq3, sk1_bf, sk2_bf, tm, n_tokens):
    out = pl.BlockSpec((PEER_TOPK, tm), lambda i, h: (h, i))
    shp = jax.ShapeDtypeStruct((PEER_HEADS * PEER_TOPK, n_tokens), F32)
    return pl.pallas_call(
        _route_kernel,
        out_shape=(shp, shp, shp),
        grid=(n_tokens // tm, PEER_HEADS),
        in_specs=[pl.BlockSpec((None, tm, 2 * PEER_HALF), lambda i, h: (h, i, 0)),
                  _resident(sk1_bf.shape), _resident(sk2_bf.shape)],
        out_specs=(out, out, out),
        compiler_params=_cparams(("parallel", "parallel")),
        name="peer_route",
    )(pq3, sk1_bf, sk2_bf)


def _peer_kernel(h2_ref, x1_ref, i10_ref, i20_ref, gt0_ref, pqn_ref, sk1_ref, sk2_ref, ed_ref, eu_ref,
                 x2_ref, pk_ref, acc_ref, rows_ref, picks_ref, *, phase_units, units_per_step):
    tm = h2_ref.shape[0]
    i = pl.program_id(0)
    j = pl.program_id(1)
    half = N_KEYS // 2
    n_chunks = tm // LANES

    def route_unit(u):
        c = u // PEER_HEADS
        h = u % PEER_HEADS
        qh = pqn_ref[h, pl.ds(pl.multiple_of(c * LANES, LANES), LANES), :]
        picks = _route_chunk(qh, sk1_ref[...], sk2_ref[...])
        r0 = pl.multiple_of(h * PEER_TOPK, PEER_TOPK)
        for q in range(3):
            picks_ref[q, c, pl.ds(r0, PEER_TOPK), :] = picks[q]

    @pl.when(j == 0)
    def _start_tile():
        @pl.when(i == 0)
        def _():
            for q, src in enumerate((i10_ref, i20_ref, gt0_ref)):
                for c in range(n_chunks):
                    rows_ref[q, c * LANES:(c + 1) * LANES, :] = src[:, c * LANES:(c + 1) * LANES].T

        @pl.when(i > 0)
        def _():
            for q in range(3):
                for c in range(n_chunks):
                    rows_ref[q, c * LANES:(c + 1) * LANES, :] = picks_ref[q, c].T

        acc_ref[...] = jnp.zeros_like(acc_ref)
        sub = lax.broadcasted_iota(jnp.int32, (N_KEYS, N_KEYS), 0)
        b_of_row = sub.astype(F32)
        a_of_row = jnp.where(sub < half, 2 * sub, 2 * (sub - half) + 1).astype(F32)
        tok = tm // phase_units

        def body(u, carry):
            route_unit(u)
            for t in range(tok):
                tt = u * tok + t
                r1 = rows_ref[0, pl.ds(tt, 1), :]
                r2 = rows_ref[1, pl.ds(tt, 1), :]
                rg = rows_ref[2, pl.ds(tt, 1), :]
                oat = jnp.where(a_of_row == r1, 1.0, 0.0).astype(BF16)
                gbt = jnp.where(b_of_row == r2, rg, 0.0).astype(BF16)
                g = _nt_dot(oat, gbt)
                packed = pltpu.pack_elementwise([g[:half], g[half:]], packed_dtype=BF16)
                pk_ref[pl.ds(pl.multiple_of(tt * G_PITCH, 8), half), :] = packed
            return carry

        lax.fori_loop(0, phase_units, body, 0)

    for k in range(units_per_step):
        route_unit(phase_units + j * units_per_step + k)
    gs = []
    for r in range(PAIRS_PER_STEP):
        w = pk_ref[pl.ds(j * PAIRS_PER_STEP + r, tm, stride=G_PITCH), :]
        gs.append(pltpu.unpack_elementwise(w, index=0, packed_dtype=BF16, unpacked_dtype=F32))
        gs.append(pltpu.unpack_elementwise(w, index=1, packed_dtype=BF16, unpacked_dtype=F32))
    gates = jnp.concatenate(gs, axis=1)
    s = _nt_dot(h2_ref[...], ed_ref[...])
    act = (jax.nn.gelu(s) * gates).astype(BF16)
    acc_ref[...] += jnp.dot(act, eu_ref[...], preferred_element_type=F32)

    @pl.when(j == pl.num_programs(1) - 1)
    def _finish():
        x2_ref[...] = x1_ref[...] + acc_ref[...]


def _peer(h2, x1, picks0, pq3, sk1, sk2, ed, eu, tm):
    t = h2.shape[0]
    nt = t // tm
    nb = PAIRS_PER_STEP * 2 * N_KEYS
    steps = ed.shape[0] // nb
    n_units = (tm // LANES) * PEER_HEADS
    units_per_step = n_units // (2 * steps)
    phase_units = n_units - steps * units_per_step
    row = lambda w: pl.BlockSpec((tm, w), lambda i, j: (i, 0))
    table = pl.BlockSpec((nb, D_MODEL), lambda i, j: (j, 0))
    return pl.pallas_call(
        functools.partial(_peer_kernel, phase_units=phase_units, units_per_step=units_per_step),
        out_shape=jax.ShapeDtypeStruct((t, D_MODEL), F32),
        grid=(nt, steps),
        in_specs=[row(D_MODEL), row(D_MODEL)] + [_resident(p.shape) for p in picks0]
                 + [pl.BlockSpec((PEER_HEADS, tm, 2 * PEER_HALF), lambda i, j: (0, jnp.minimum(i + 1, nt - 1), 0)),
                    _resident(sk1.shape), _resident(sk2.shape), table, table],
        out_specs=row(D_MODEL),
        scratch_shapes=[pltpu.VMEM((tm * G_PITCH, N_KEYS), jnp.uint32),
                        pltpu.VMEM((tm, D_MODEL), F32),
                        pltpu.VMEM((3, tm, N_KEYS), F32),
                        pltpu.VMEM((3, tm // LANES, PEER_HEADS * PEER_TOPK, LANES), F32)],
        compiler_params=_cparams(("arbitrary", "arbitrary"), V7X_VMEM_LIMIT),
        name="peer",
    )(h2, x1, *picks0, pq3, sk1, sk2, ed, eu)


def _ple_kernel(x2_ref, p_ref, g3_ref, wpg_ref, wpp_ref, y_ref):
    x2 = x2_ref[...]
    h3 = _rms(x2, g3_ref[...]).astype(BF16)
    gate = jax.nn.sigmoid(jnp.dot(h3, wpg_ref[...], preferred_element_type=F32))
    y_ref[...] = x2 + gate * jnp.dot(p_ref[...].astype(BF16), wpp_ref[...], preferred_element_type=F32)


def _ple(x2, p2d, g3, wpg, wpp, tm):
    t = x2.shape[0]
    row = lambda w: pl.BlockSpec((tm, w), lambda i: (i, 0))
    return pl.pallas_call(
        _ple_kernel,
        out_shape=jax.ShapeDtypeStruct((t, D_MODEL), F32),
        grid=(t // tm,),
        in_specs=[row(D_MODEL), row(PLE_DIM), _resident(g3.shape), _resident(wpg.shape), _resident(wpp.shape)],
        out_specs=row(D_MODEL),
        compiler_params=_cparams(("parallel",)),
        name="ple",
    )(x2, p2d, g3, wpg, wpp)


def _t5_bucket(dist):
    max_exact = N_BUCKETS // 2
    df = jnp.maximum(dist, max_exact).astype(F32)
    large = max_exact + (jnp.log(df / max_exact) / math.log(MAX_DISTANCE / max_exact)
                         * (N_BUCKETS - max_exact)).astype(jnp.int32)
    return jnp.where(dist < max_exact, dist, jnp.minimum(large, N_BUCKETS - 1))


def _bias_tables(rel_bias, g):
    dil = DILATIONS[g]
    hi = lax.Precision.HIGHEST
    bucket = _t5_bucket(jnp.arange(BAND + 1) * dil)
    per_dist = rel_bias[:, g * HEADS:(g + 1) * HEADS][bucket].astype(F32)
    dsub = np.arange(BAND)[:, None] + BAND - np.arange(2 * BAND)[None, :]
    valid = (dsub >= 0) & (dsub <= WINDOWS[g] // dil)
    spread = (jnp.asarray(np.clip(dsub, 0, BAND).reshape(1, -1)) == jnp.arange(BAND + 1)[:, None]).astype(F32)
    band = jnp.dot(per_dist.T, spread, precision=hi).reshape(HEADS, BAND, 2 * BAND)
    add = jnp.where(jnp.asarray(valid)[None], band, NEG_INF)
    addc = per_dist[::-1][:BAND].T
    addn = per_dist[0][:, None]
    return add, addc, addn


def kernel(x_prompt, x_sample, cache_kv_w128, cache_kv_w512, cache_kv_w2048, p_prompt, p_sample, rel_bias, norm1,
           w_in, q_norm, k_norm, b_v_norm, w_spatial, b_spatial, w_up_a, w_up_b, w_out, norm2, w_query,
           sub_keys_1, sub_keys_2, expert_down, expert_up, norm3, w_ple_gate, w_ple_proj):
    bsz, seq, _ = x_prompt.shape
    db, ds, _ = x_sample.shape
    depth = norm1.shape[0]
    assert depth == 1 and ds == 1 and seq % (BAND * DILATIONS[-1]) == 0 and seq >= WINDOWS[-1]
    caches = (cache_kv_w128, cache_kv_w512, cache_kv_w2048)
    for g in range(N_GROUPS):
        assert caches[g].shape[2] == WINDOWS[g]
    t_p = bsz * seq
    t_s = 128
    li = 0

    g1 = norm1[li][None, :]
    w_in_bf = w_in[li].astype(BF16)
    qg = jnp.tile(q_norm[li], (1, HEADS))
    kg = jnp.tile(k_norm[li], (1, HEADS))
    bvg = b_v_norm[li][None, :]
    blk = np.arange(V7X_MXU_DIM) // HEAD_DIM
    hsum = jnp.asarray(blk[:, None] == blk[None, :], dtype=BF16)
    wua, wub, wo = w_up_a[li].astype(BF16), w_up_b[li].astype(BF16), w_out[li].astype(BF16)
    g2, g3 = norm2[li][None, :], norm3[li][None, :]
    wq = w_query[li].astype(BF16)
    sk1, sk2 = sub_keys_1[li].astype(BF16), sub_keys_2[li].astype(BF16)
    ed = expert_down[li].astype(BF16)
    eu = expert_up[li].astype(BF16)
    wpg, wpp = w_ple_gate[li].astype(BF16), w_ple_proj[li].astype(BF16)
    bs_full = jnp.repeat(b_spatial[li].T, D_MODEL // B_GROUPS, axis=1)
    tables = [_bias_tables(rel_bias, g) for g in range(N_GROUPS)]

    res = _in_projection_prompt(x_prompt, g1, w_in_bf, qg, kg, bvg, hsum, TM_INPROJ)
    qs, ks, vs, tails = res[0:3], res[3:6], res[6:9], res[9:12]
    ub, vb, ga, gb = res[12:16]
    outs, lses, kv_prompt = [], [], []
    for g in range(N_GROUPS):
        o, lse = _prompt_attention(qs[g], ks[g], vs[g], tables[g][0], g)
        outs.append(o)
        lses.append(lse)
        kv_prompt.append(tails[g].reshape(1, bsz, tails[g].shape[1], 2, HEADS, HEAD_DIM))
    x1, h2, pq = _merge_prompt(x_prompt, outs, lses, ub, vb, ga, gb, w_spatial[li], bs_full, wua, wub, wo, g2, wq,
                               TM_INPROJ)
    x1, h2 = x1.reshape(t_p, D_MODEL), h2.reshape(t_p, D_MODEL)
    pq = pq.reshape(PEER_HEADS, t_p, 2 * PEER_HALF)
    picks0 = _route(pq, sk1, sk2, TM_PEER, TM_PEER)
    x2 = _peer(h2, x1, picks0, pq, sk1, sk2, ed, eu, TM_PEER)
    y_prompt = _ple(x2, p_prompt[li].reshape(t_p, PLE_DIM), g3, wpg, wpp, TM_PEER).reshape(bsz, seq, D_MODEL)

    xs = jnp.pad(x_sample.reshape(db, D_MODEL), ((0, t_s - db), (0, 0)))
    q, k, v, ub, vb, ga, gb = _in_projection_sample(xs, g1, w_in_bf, qg, kg, bvg, hsum)
    nh = N_GROUPS * HEADS
    k3, v3 = k[:db].reshape(db, nh, HEAD_DIM), v[:db].reshape(db, nh, HEAD_DIM)
    caches_t = [jnp.transpose(caches[g], (0, 1, 3, 4, 5, 2))[li] for g in range(N_GROUPS)]
    adds = []
    for g in range(N_GROUPS):
        dil = DILATIONS[g]
        full = jnp.full((HEADS, BAND, dil), NEG_INF, F32).at[:, :, 0].set(tables[g][1])
        adds.append(full.reshape(HEADS, BAND * dil))
    addn = jnp.stack([tables[g][2] for g in range(N_GROUPS)])
    o3, lse3 = _sample_attention(q[:db, None, :].astype(F32), k[:db, None, :], v[:db, None, :], caches_t, adds, addn)
    pad = lambda a: jnp.pad(a, ((0, t_s - db), (0, 0)))
    o2, lse2 = o3.reshape(db, QKV_W), lse3.reshape(db, QKV_W)
    outs = [pad(o2[:, g * GROUP_W:(g + 1) * GROUP_W]) for g in range(N_GROUPS)]
    lses = [pad(lse2[:, g * GROUP_W:(g + 1) * GROUP_W]) for g in range(N_GROUPS)]
    kv_sample = [jnp.stack([k3[:, None, g * HEADS:(g + 1) * HEADS], v3[:, None, g * HEADS:(g + 1) * HEADS]],
                           axis=2)[None] for g in range(N_GROUPS)]
    ws0 = jnp.repeat(w_spatial[li][:, 0, 0], D_MODEL // B_GROUPS)[None, :]
    bs0 = bs_full[0:1, :]
    x1, h2, pq = _merge_sample(xs, outs, lses, ub, vb, ga, gb, ws0, bs0, wua, wub, wo, g2, wq)
    picks0 = _route(pq, sk1, sk2, t_s, t_s)
    ps = jnp.pad(p_sample[li].reshape(db, PLE_DIM), ((0, t_s - db), (0, 0)))
    x2 = _peer(h2, x1, picks0, pq, sk1, sk2, ed, eu, t_s)
    y_sample = _ple(x2, ps, g3, wpg, wpp, t_s)[:db].reshape(db, ds, D_MODEL)
    v_sample_chunk = vb[:db].reshape(1, db, ds, D_MODEL)

    return (y_prompt, y_sample, kv_prompt[0], kv_prompt[1], kv_prompt[2],
            kv_sample[0], kv_sample[1], kv_sample[2], v_sample_chunk)
```
